```python
import math
import jax, jax.numpy as jnp
from jax import lax
import numpy as np

D_MODEL = 1024
BATCH = 4
SEQ = 4096
DEPTH = 2

H_MLA = 8
MLA_NOPE = 64
MLA_ROPE = 32
MLA_V = 64
Q_LORA = 384
KV_LORA = 256
H_DSA = 4
DSA_DIM = 64
H_IDX = 8
D_IDX = 32
TOPK_MAX = 256
H_DIFF = 4
DIFF_QK = 32
DIFF_V = 2 * DIFF_QK
D_MIX = H_MLA * MLA_V + H_DSA * DSA_DIM + H_DIFF * DIFF_V
D_FF = 4 * D_MODEL
N_BUCKETS = 32
MAX_DISTANCE = 128
N_BIAS_HEADS = H_DSA + H_DIFF
ROPE_BASE = 10000.0
Q_BLOCK = 128
EPS = 1e-6
NEG = -1e30

IN_SPLITS = (
    Q_LORA, KV_LORA, MLA_ROPE,
    H_DSA * DSA_DIM, DSA_DIM, DSA_DIM,
    H_IDX * D_IDX, D_IDX, H_IDX,
    H_DIFF * 2 * DIFF_QK, H_DIFF * 2 * DIFF_QK, H_DIFF * DIFF_V,
)
IN_OFFSETS = tuple(int(o) for o in np.cumsum(IN_SPLITS)[:-1])
N_IN = int(sum(IN_SPLITS))

kernel_name = "hymba_mla_dsa_diff_trunk"


def _rmsnorm(x, g):
    xf = x.astype(jnp.float32)
    y = xf * lax.rsqrt(jnp.mean(xf * xf, axis=-1, keepdims=True) + EPS)
    return (y * g.astype(jnp.float32)).astype(x.dtype)


def _rope(x, pos):
    half = x.shape[-1] // 2
    freqs = ROPE_BASE ** (-jnp.arange(half, dtype=jnp.float32) / half)
    ang = pos.astype(jnp.float32)[..., None] * freqs
    ang = ang.reshape(ang.shape[:2] + (1,) * (x.ndim - 3) + (half,))
    cos, sin = jnp.cos(ang), jnp.sin(ang)
    x1 = x[..., :half].astype(jnp.float32)
    x2 = x[..., half:].astype(jnp.float32)
    return jnp.concatenate([x1 * cos - x2 * sin, x1 * sin + x2 * cos], axis=-1).astype(x.dtype)


def _t5_bucket(rel):
    n = jnp.maximum(rel, 0)
    max_exact = N_BUCKETS // 2
    nf = jnp.maximum(n, 1).astype(jnp.float32)
    large = max_exact + (jnp.log(nf / max_exact) / math.log(MAX_DISTANCE / max_exact)
                         * (N_BUCKETS - max_exact)).astype(jnp.int32)
    large = jnp.minimum(large, N_BUCKETS - 1)
    return jnp.where(n < max_exact, n, large)


def _unblock(y):
    nb, b, qb = y.shape[:3]
    return jnp.moveaxis(y, 0, 1).reshape(b, nb * qb, -1)


def _mla(c_q, c_kv, k_rope_raw, pos, q_norm, w_uq, kv_norm, w_ukv):
    B, S, _ = c_q.shape
    q = (_rmsnorm(c_q, q_norm) @ w_uq).reshape(B, S, H_MLA, MLA_NOPE + MLA_ROPE)
    q_nope = q[..., :MLA_NOPE]
    q_rope = _rope(q[..., MLA_NOPE:], pos)
    kv = (_rmsnorm(c_kv, kv_norm) @ w_ukv).reshape(B, S, H_MLA, MLA_NOPE + MLA_V)
    k_nope, v = kv[..., :MLA_NOPE], kv[..., MLA_NOPE:]
    k_rope = _rope(k_rope_raw, pos)
    scale = (MLA_NOPE + MLA_ROPE) ** -0.5
    kidx = jnp.arange(S)

    def block(i):
        start = i * Q_BLOCK
        qn = lax.dynamic_slice_in_dim(q_nope, start, Q_BLOCK, 1)
        qr = lax.dynamic_slice_in_dim(q_rope, start, Q_BLOCK, 1)
        logits = (jnp.einsum('bqhd,bkhd->bhqk', qn, k_nope)
                  + jnp.einsum('bqhr,bkr->bhqk', qr, k_rope)).astype(jnp.float32) * scale
        qidx = start + jnp.arange(Q_BLOCK)
        logits = jnp.where(kidx[None, :] <= qidx[:, None], logits, NEG)
        p = jax.nn.softmax(logits, axis=-1).astype(v.dtype)
        return jnp.einsum('bhqk,bkhd->bqhd', p, v)

    return _unblock(lax.map(block, jnp.arange(S // Q_BLOCK)))


def _dsa(q, k, v, q_idx, k_idx, w_idx, pos, table):
    B, S, _ = q.shape
    q = q.reshape(B, S, H_DSA, DSA_DIM)
    q_idx = q_idx.reshape(B, S, H_IDX, D_IDX)
    n_sel = min(TOPK_MAX, S // 4)
    idx_scale = (D_IDX ** -0.5) * (H_IDX ** -0.5)
    scale = DSA_DIM ** -0.5
    kidx = jnp.arange(S)
    gather = jax.vmap(lambda a, i: a[i])

    def block(i):
        start = i * Q_BLOCK
        qb = lax.dynamic_slice_in_dim(q, start, Q_BLOCK, 1)
        qib = lax.dynamic_slice_in_dim(q_idx, start, Q_BLOCK, 1)
        wb = lax.dynamic_slice_in_dim(w_idx, start, Q_BLOCK, 1)
        pq = lax.dynamic_slice_in_dim(pos, start, Q_BLOCK, 1)
        qidx = start + jnp.arange(Q_BLOCK)
        s = jax.nn.relu(jnp.einsum('bqhd,bkd->bqhk', qib, k_idx).astype(jnp.float32))
        score = jnp.einsum('bqh,bqhk->bqk', wb.astype(jnp.float32), s) * idx_scale
        score = jnp.where((kidx[None, :] <= qidx[:, None])[None], score, NEG)
        _, sel = lax.top_k(score, n_sel)
        valid = sel <= qidx[None, :, None]
        k_sel = gather(k, sel)
        v_sel = gather(v, sel)
        p_sel = gather(pos, sel)
        bias = table[_t5_bucket(pq[:, :, None] - p_sel)][..., :H_DSA]
        logits = (jnp.einsum('bqhd,bqnd->bhqn', qb, k_sel).astype(jnp.float32) * scale
                  + jnp.transpose(bias, (0, 3, 1, 2)).astype(jnp.float32))
        logits = jnp.where(valid[:, None], logits, NEG)
        p = jax.nn.softmax(logits, axis=-1).astype(v.dtype)
        return jnp.einsum('bhqn,bqnd->bqhd', p, v_sel)

    return _unblock(lax.map(block, jnp.arange(S // Q_BLOCK)))


def _diff(q, k, v, pos, table, lam_params, subln, lam_init):
    B, S, _ = q.shape
    q = q.reshape(B, S, H_DIFF, 2, DIFF_QK)
    k = k.reshape(B, S, H_DIFF, 2, DIFF_QK)
    v = v.reshape(B, S, H_DIFF, DIFF_V)
    lp = lam_params.astype(jnp.float32)
    lam = jnp.exp(jnp.sum(lp[0] * lp[1])) - jnp.exp(jnp.sum(lp[2] * lp[3])) + lam_init
    scale = DIFF_QK ** -0.5
    kidx = jnp.arange(S)

    def block(i):
        start = i * Q_BLOCK
        qb = lax.dynamic_slice_in_dim(q, start, Q_BLOCK, 1)
        pq = lax.dynamic_slice_in_dim(pos, start, Q_BLOCK, 1)
        qidx = start + jnp.arange(Q_BLOCK)
        logits = jnp.einsum('bqhcd,bkhcd->bchqk', qb, k).astype(jnp.float32) * scale
        bias = table[_t5_bucket(pq[:, :, None] - pos[:, None, :])][..., H_DSA:]
        logits = logits + jnp.transpose(bias, (0, 3, 1, 2)).astype(jnp.float32)[:, None]
        logits = jnp.where(kidx[None, :] <= qidx[:, None], logits, NEG)
        p = jax.nn.softmax(logits, axis=-1)
        a = (p[:, 0] - lam * p[:, 1]).astype(v.dtype)
        return jnp.einsum('bhqk,bkhd->bqhd', a, v)

    o = _unblock(lax.map(block, jnp.arange(S // Q_BLOCK))).reshape(B, S, H_DIFF, DIFF_V)
    o = _rmsnorm(o, subln) * (1.0 - lam_init)
    return o.reshape(B, S, H_DIFF * DIFF_V)


def setup_inputs(seed: int = 0) -> dict:
    key = jax.random.key(seed)
    ks = jax.random.split(key, 16)
    f32 = jnp.float32

    def nrm(k, shape, fan_in):
        return jax.random.normal(k, shape, f32) * (fan_in ** -0.5)

    def gain(k, shape):
        return 1.0 + 0.05 * jax.random.normal(k, shape, f32)

    return {
        "x": jax.random.normal(ks[0], (BATCH, SEQ, D_MODEL), f32),
        "positions": jnp.broadcast_to(jnp.arange(SEQ, dtype=jnp.int32), (BATCH, SEQ)),
        "rel_bias": 0.5 * jax.random.normal(ks[1], (N_BUCKETS, N_BIAS_HEADS), f32),
        "norm_attn": gain(ks[2], (DEPTH, D_MODEL)),
        "w_in": nrm(ks[3], (DEPTH, D_MODEL, N_IN), D_MODEL),
        "q_norm": gain(ks[4], (DEPTH, Q_LORA)),
        "w_uq": nrm(ks[5], (DEPTH, Q_LORA, H_MLA * (MLA_NOPE + MLA_ROPE)), Q_LORA),
        "kv_norm": gain(ks[6], (DEPTH, KV_LORA)),
        "w_ukv": nrm(ks[7], (DEPTH, KV_LORA, H_MLA * (MLA_NOPE + MLA_V)), KV_LORA),
        "diff_lambda": 0.1 * jax.random.normal(ks[8], (DEPTH, 4, DIFF_QK), f32),
        "diff_subln": gain(ks[9], (DEPTH, DIFF_V)),
        "w_out": nrm(ks[10], (DEPTH, D_MIX, D_MODEL), D_MIX),
        "norm_mlp": gain(ks[11], (DEPTH, D_MODEL)),
        "w_up": nrm(ks[12], (DEPTH, D_MODEL, D_FF), D_MODEL),
        "w_down": nrm(ks[13], (DEPTH, D_FF, D_MODEL), D_FF),
        "norm_final": gain(ks[14], (D_MODEL,)),
    }


def reference(x, positions, rel_bias, norm_attn, w_in, q_norm, w_uq, kv_norm, w_ukv,
              diff_lambda, diff_subln, w_out, norm_mlp, w_up, w_down, norm_final):
    for l in range(DEPTH):
        h = _rmsnorm(x, norm_attn[l])
        proj = h @ w_in[l]
        (c_q, c_kv, k_rope_raw,
         q_b, k_b, v_b, qi_b, ki_b, wi_b,
         q_c, k_c, v_c) = jnp.split(proj, IN_OFFSETS, axis=-1)
        o_a = _mla(c_q, c_kv, k_rope_raw, positions, q_norm[l], w_uq[l], kv_norm[l], w_ukv[l])
        o_b = _dsa(q_b, k_b, v_b, qi_b, ki_b, wi_b, positions, rel_bias)
        lam_init = 0.8 - 0.6 * math.exp(-0.3 * l)
        o_c = _diff(q_c, k_c, v_c, positions, rel_bias, diff_lambda[l], diff_subln[l], lam_init)
        mix = jnp.concatenate([o_a, o_b, o_c], axis=-1)
        x = x + mix @ w_out[l]
        h = _rmsnorm(x, norm_mlp[l])
        x = x + jnp.square(jax.nn.relu(h @ w_up[l])) @ w_down[l]
    return _rmsnorm(x, norm_final)
```

```python
import functools
import math

import jax
import jax.numpy as jnp
import numpy as np
from jax import lax
from jax.experimental import pallas as pl
from jax.experimental.pallas import tpu as pltpu

D_MODEL = 1024
H_MLA, MLA_NOPE, MLA_ROPE, MLA_V = 8, 64, 32, 64
Q_LORA, KV_LORA = 384, 256
H_DSA, DSA_DIM, H_IDX, D_IDX, TOPK_MAX = 4, 64, 8, 32, 256
H_DIFF, DIFF_QK = 4, 32
DIFF_V = 2 * DIFF_QK
D_FF = 4 * D_MODEL
N_BUCKETS, MAX_DISTANCE = 32, 128
ROPE_BASE = 10000.0
EPS = 1e-6
NEG = -1e30
LOG2E = math.log2(math.e)

_SPLITS = (Q_LORA, KV_LORA, MLA_ROPE, H_DSA * DSA_DIM, DSA_DIM, DSA_DIM, H_IDX * D_IDX, D_IDX, H_IDX,
           H_DIFF * 2 * DIFF_QK, H_DIFF * 2 * DIFF_QK, H_DIFF * DIFF_V)
_OFF = tuple(int(o) for o in np.concatenate([[0], np.cumsum(_SPLITS)]))

LANES = 128
MXU_DTYPE = jnp.bfloat16
VMEM_LIMIT = 56 * 1024 * 1024

C_CQ = 0
C_CKV = C_CQ + Q_LORA
C_KR = C_CKV + KV_LORA
C_KRR = C_KR + LANES
C_QB = C_KRR + LANES
C_KB = C_QB + H_DSA * LANES
C_VB = C_KB + LANES
C_QI = C_VB + LANES
C_KI = C_QI + H_IDX * D_IDX
C_WI = C_KI + H_IDX * D_IDX
C_QC = C_WI + LANES
C_KC = C_QC + H_DIFF * 2 * DIFF_QK
C_VC = C_KC + H_DIFF * 2 * DIFF_QK
N1 = C_VC + H_DIFF * LANES

ONES_LANE = 64

TM_PROJ = 256
TQ = 256
TM_MLP = 512
FF_CHUNK = 1024

_NT = (((1,), (1,)), ((), ()))


def _dot(a, b):
    return jnp.dot(a, b, preferred_element_type=jnp.float32)


def _dot_nt(a, b):
    return lax.dot_general(a, b, _NT, preferred_element_type=jnp.float32)


def _ones_row(width):
    lane = lax.broadcasted_iota(jnp.int32, (1, width), 1)
    return jnp.where(lane % LANES == ONES_LANE, 1.0, 0.0).astype(jnp.float32)


def _proj_kernel(x_ref, g_ref, w1_ref, qn_ref, wuq_ref, wuqr_ref, kvn_ref, wuk_ref, wuv_ref, c_ref, s_ref,
                 qm_ref, km_ref, vm_ref, qd_ref, kd_ref, vd_ref, qi_ref, ki_ref, wi_ref, qc_ref, kc_ref, vc_ref):
    x = x_ref[...]
    h = x * lax.rsqrt(jnp.mean(x * x, axis=-1, keepdims=True) + EPS)
    h = (h * g_ref[...]).astype(MXU_DTYPE)

    def seg(a, b):
        return _dot(h, w1_ref[:, a:b])

    cos = c_ref[...]
    sin = s_ref[...]

    c_q = seg(C_CQ, C_CKV)
    c_q = c_q * lax.rsqrt(jnp.mean(c_q * c_q, axis=-1, keepdims=True) + EPS)
    c_q = (c_q * qn_ref[...]).astype(MXU_DTYPE)
    qa = _dot(c_q, wuq_ref[...])
    qb = _dot(c_q, wuqr_ref[...])
    for hh in range(H_MLA):
        sl = slice(hh * LANES, (hh + 1) * LANES)
        qm_ref[:, sl] = (qa[:, sl] * cos + qb[:, sl] * sin).astype(qm_ref.dtype)

    c_kv = seg(C_CKV, C_KR)
    c_kv = c_kv * lax.rsqrt(jnp.mean(c_kv * c_kv, axis=-1, keepdims=True) + EPS)
    c_kv = (c_kv * kvn_ref[...]).astype(MXU_DTYPE)
    k_rope = seg(C_KR, C_KRR) * cos + seg(C_KRR, C_QB) * sin
    kk = _dot(c_kv, wuk_ref[...])
    for hh in range(H_MLA):
        sl = slice(hh * LANES, (hh + 1) * LANES)
        km_ref[:, sl] = (kk[:, sl] + k_rope).astype(km_ref.dtype)
    vm_ref[...] = (_dot(c_kv, wuv_ref[...]) + _ones_row(H_MLA * LANES)).astype(vm_ref.dtype)

    qd_ref[...] = seg(C_QB, C_KB).astype(qd_ref.dtype)
    kd_ref[...] = seg(C_KB, C_VB).astype(kd_ref.dtype)
    vd_ref[...] = (seg(C_VB, C_QI) + _ones_row(LANES)).astype(vd_ref.dtype)
    qi_ref[...] = seg(C_QI, C_KI).astype(qi_ref.dtype)
    ki_ref[...] = seg(C_KI, C_WI).astype(ki_ref.dtype)
    wi_ref[...] = seg(C_WI, C_QC)
    qc_ref[...] = seg(C_QC, C_KC).astype(qc_ref.dtype)
    kc_ref[...] = seg(C_KC, C_VC).astype(kc_ref.dtype)
    vc_ref[...] = (seg(C_VC, N1) + _ones_row(H_DIFF * LANES)).astype(vc_ref.dtype)


def _proj_call(x2d, g, w1, qn, wuq, wuqr, kvn, wuk, wuv, cos_t, sin_t):
    t = x2d.shape[0]
    tm = TM_PROJ
    assert t % tm == 0
    row = lambda w: pl.BlockSpec((tm, w), lambda i: (i, 0))
    full = lambda a: pl.BlockSpec(a.shape, lambda i: (0, 0))
    widths = (H_MLA * LANES, H_MLA * LANES, H_MLA * LANES, H_DSA * LANES, LANES, LANES,
              H_IDX * D_IDX, H_IDX * D_IDX, LANES, 2 * H_DIFF * DIFF_QK, 2 * H_DIFF * DIFF_QK, H_DIFF * LANES)
    dtypes = [MXU_DTYPE] * 12
    dtypes[8] = jnp.float32
    return pl.pallas_call(
        _proj_kernel,
        grid=(t // tm,),
        in_specs=[row(D_MODEL), full(g), full(w1), full(qn), full(wuq), full(wuqr), full(kvn), full(wuk),
                  full(wuv), row(LANES), row(LANES)],
        out_specs=[row(w) for w in widths],
        out_shape=[jax.ShapeDtypeStruct((t, w), d) for w, d in zip(widths, dtypes)],
        compiler_params=pltpu.CompilerParams(dimension_semantics=("arbitrary",), vmem_limit_bytes=VMEM_LIMIT),
        name="proj",
    )(x2d, g, w1, qn, wuq, wuqr, kvn, wuk, wuv, cos_t, sin_t)


def _flash_update(s, m_ref, acc_ref, v_blk):
    m_prev = m_ref[...]
    m_new = jnp.maximum(m_prev, jnp.max(s, axis=-1, keepdims=True))
    alpha = jnp.exp2(m_prev - m_new)
    p = jnp.exp2(s - m_new).astype(MXU_DTYPE)
    acc_ref[...] = alpha * acc_ref[...] + _dot(p, v_blk)
    m_ref[...] = m_new


def _normalized(acc, width):
    return acc[:, :width] / acc[:, ONES_LANE:ONES_LANE + 1]


def _causal_tile(tq):
    row = lax.broadcasted_iota(jnp.int32, (tq, tq), 0)
    col = lax.broadcasted_iota(jnp.int32, (tq, tq), 1)
    return col <= row


def _mla_kernel(q_ref, k_ref, v_ref, o_ref, m_ref, acc_ref):
    i = pl.program_id(2)
    tq = q_ref.shape[1]
    m_ref[...] = jnp.full(m_ref.shape, NEG, jnp.float32)
    acc_ref[...] = jnp.zeros(acc_ref.shape, jnp.float32)

    def step(j, diag):
        rows = pl.ds(pl.multiple_of(j * tq, tq), tq)
        for hh in range(2):
            sl = slice(hh * LANES, (hh + 1) * LANES)
            s = _dot_nt(q_ref[0, :, sl], k_ref[0, rows, sl])
            if diag:
                s = jnp.where(_causal_tile(tq), s, NEG)
            _flash_update(s, m_ref.at[hh], acc_ref.at[hh], v_ref[0, rows, sl])

    def body(j, carry):
        step(j, False)
        return carry

    lax.fori_loop(0, i, body, 0)
    step(i, True)
    o_ref[0] = jnp.concatenate([_normalized(acc_ref[hh], MLA_V) for hh in range(2)], axis=-1).astype(o_ref.dtype)


def _mla_call(qm, km, vm):
    b, s, _ = qm.shape
    tq = TQ
    assert s % tq == 0
    return pl.pallas_call(
        _mla_kernel,
        grid=(b, H_MLA // 2, s // tq),
        in_specs=[pl.BlockSpec((1, tq, 2 * LANES), lambda bb, hp, i: (bb, i, hp)),
                  pl.BlockSpec((1, s, 2 * LANES), lambda bb, hp, i: (bb, 0, hp)),
                  pl.BlockSpec((1, s, 2 * LANES), lambda bb, hp, i: (bb, 0, hp))],
        out_specs=pl.BlockSpec((1, tq, 2 * MLA_V), lambda bb, hp, i: (bb, i, hp)),
        out_shape=jax.ShapeDtypeStruct((b, s, H_MLA * MLA_V), MXU_DTYPE),
        scratch_shapes=[pltpu.VMEM((2, tq, 1), jnp.float32), pltpu.VMEM((2, tq, LANES), jnp.float32)],
        compiler_params=pltpu.CompilerParams(dimension_semantics=("arbitrary",) * 3, vmem_limit_bytes=VMEM_LIMIT),
        name="mla",
    )(qm, km, vm)


def _ordered_to_float(k):
    bits = k ^ ((k >> 31) & jnp.int32(0x7FFFFFFF))
    return lax.bitcast_convert_type(bits, jnp.float32)


def _dsa_kernel(n_sel, qd_ref, qi_ref, wi_ref, ki_ref, kd_ref, vd_ref, bd_ref, bp_ref, o_ref,
                sc_ref, qim_ref, wb_ref, m_ref, acc_ref):
    i = pl.program_id(1)
    tq = qd_ref.shape[1]
    n_chunks = i + 1
    causal = _causal_tile(tq)
    int_min = jnp.int32(-2 ** 31)

    qi = qi_ref[0]
    lane_head = lax.broadcasted_iota(jnp.int32, qi.shape, 1) // D_IDX
    w = wi_ref[0]
    for hh in range(H_IDX):
        qim_ref[hh] = jnp.where(lane_head == hh, qi, jnp.zeros_like(qi))
        wb_ref[hh] = jnp.broadcast_to(w[:, hh:hh + 1], (tq, tq))

    def score_chunk(jc, diag):
        kc = ki_ref[0, pl.ds(pl.multiple_of(jc * tq, tq), tq), :]
        acc = jnp.zeros((tq, tq), jnp.float32)
        for hh in range(H_IDX):
            acc = acc + jnp.maximum(_dot_nt(qim_ref[hh], kc), 0.0) * wb_ref[hh]
        if diag:
            acc = jnp.where(causal, acc, NEG)
        sc_ref[jc] = acc

    def score_body(jc, carry):
        score_chunk(jc, False)
        return carry

    lax.fori_loop(0, i, score_body, 0)
    score_chunk(i, True)

    def count_where(hits):
        def body(jc, acc):
            hit = hits(sc_ref[jc], jc)
            return acc + hit[:, :LANES] + hit[:, LANES:]
        acc = lax.fori_loop(0, n_chunks, body, jnp.zeros((tq, LANES), jnp.float32))
        return jnp.sum(acc, axis=-1, keepdims=True)

    def count_ge(thr):
        return count_where(lambda sc, jc: jnp.where(sc >= thr, 1.0, 0.0))

    k0 = jnp.where(count_ge(jnp.zeros((tq, 1), jnp.float32)) >= n_sel, jnp.int32(0), int_min)
    k0 = jnp.broadcast_to(k0, (tq, 1)).astype(jnp.int32)

    def bit_body(t, k):
        cand = k + lax.shift_left(jnp.int32(1), jnp.int32(30) - t)
        return jnp.where(count_ge(_ordered_to_float(cand)) >= n_sel, cand, k)

    kth = lax.fori_loop(0, 31, bit_body, k0)
    thr = _ordered_to_float(kth)
    m_ref[0] = thr
    c_ge = count_ge(thr)

    @pl.when(jnp.max(c_ge) > n_sel)
    def _():
        thr_ = m_ref[0]
        need = n_sel - count_where(lambda sc, jc: jnp.where(sc > thr_, 1.0, 0.0))

        def tie_before(bound):
            def hits(sc, jc):
                idx = jc * tq + lax.broadcasted_iota(jnp.int32, (tq, tq), 1)
                return jnp.where(idx < bound, jnp.where(sc == thr_, 1.0, 0.0), 0.0)
            return count_where(hits)

        def idx_body(t, jb):
            cand = jb + lax.shift_left(jnp.int32(1), jnp.int32(23) - t)
            return jnp.where(tie_before(cand) <= need, cand, jb)

        bound = lax.fori_loop(0, 24, idx_body, jnp.zeros((tq, 1), jnp.int32))

        def drop_body(jc, carry):
            sc = sc_ref[jc]
            idx = jc * tq + lax.broadcasted_iota(jnp.int32, (tq, tq), 1)
            sc_ref[jc] = jnp.where(idx >= bound, jnp.where(sc == thr_, -jnp.inf, sc), sc)
            return carry

        lax.fori_loop(0, n_chunks, drop_body, 0)

    thr = m_ref[0]

    m_ref[...] = jnp.full(m_ref.shape, NEG, jnp.float32)
    acc_ref[...] = jnp.zeros(acc_ref.shape, jnp.float32)

    def att_chunk(jc, kind):
        rows = pl.ds(pl.multiple_of(jc * tq, tq), tq)
        sel = sc_ref[jc] >= thr
        kb = kd_ref[0, rows, :]
        vb = vd_ref[0, rows, :]
        for hh in range(H_DSA):
            s = _dot_nt(qd_ref[0, :, hh * LANES:(hh + 1) * LANES], kb)
            if kind == "diag":
                s = jnp.where(causal, s + bd_ref[hh], NEG)
            elif kind == "prev":
                s = s + bp_ref[hh]
            s = jnp.where(sel, s, NEG)
            _flash_update(s, m_ref.at[hh], acc_ref.at[hh], vb)

    def att_body(jc, carry):
        att_chunk(jc, "far")
        return carry

    lax.fori_loop(0, i - 1, att_body, 0)

    @pl.when(i >= 1)
    def _():
        att_chunk(i - 1, "prev")

    att_chunk(i, "diag")
    o_ref[0] = jnp.concatenate([_normalized(acc_ref[hh], DSA_DIM) for hh in range(H_DSA)], axis=-1).astype(o_ref.dtype)


def _dsa_call(qd, qi, wi, ki, kd, vd, bias_diag, bias_prev):
    b, s, _ = qd.shape
    tq = TQ
    n_sel = min(TOPK_MAX, s // 4)
    assert s % tq == 0 and tq >= n_sel and tq + 1 >= MAX_DISTANCE
    qblk = lambda w: pl.BlockSpec((1, tq, w), lambda bb, i: (bb, i, 0))
    kblk = lambda w: pl.BlockSpec((1, s, w), lambda bb, i: (bb, 0, 0))
    bblk = pl.BlockSpec((H_DSA, tq, tq), lambda bb, i: (0, 0, 0))
    return pl.pallas_call(
        functools.partial(_dsa_kernel, n_sel),
        grid=(b, s // tq),
        in_specs=[qblk(H_DSA * LANES), qblk(H_IDX * D_IDX), qblk(LANES), kblk(H_IDX * D_IDX), kblk(LANES), kblk(LANES),
                  bblk, bblk],
        out_specs=qblk(H_DSA * DSA_DIM),
        out_shape=jax.ShapeDtypeStruct((b, s, H_DSA * DSA_DIM), MXU_DTYPE),
        scratch_shapes=[pltpu.VMEM((s // tq, tq, tq), jnp.float32),
                        pltpu.VMEM((H_IDX, tq, H_IDX * D_IDX), MXU_DTYPE),
                        pltpu.VMEM((H_IDX, tq, tq), jnp.float32),
                        pltpu.VMEM((H_DSA, tq, 1), jnp.float32),
                        pltpu.VMEM((H_DSA, tq, LANES), jnp.float32)],
        compiler_params=pltpu.CompilerParams(dimension_semantics=("arbitrary",) * 2, vmem_limit_bytes=VMEM_LIMIT),
        name="dsa",
    )(qd, qi, wi, ki, kd, vd, bias_diag, bias_prev)


def _diff_kernel(lam_scale, qc_ref, kc_ref, vc_ref, bd_ref, bp_ref, lam_ref, g_ref, o_ref, qm_ref, m_ref, acc_ref):
    i = pl.program_id(1)
    tq = qc_ref.shape[1]
    n_maps = 2 * H_DIFF
    causal = _causal_tile(tq)

    q = qc_ref[0]
    lane_map = lax.broadcasted_iota(jnp.int32, q.shape, 1) // DIFF_QK
    for mm in range(n_maps):
        qm_ref[mm] = jnp.where(lane_map == mm, q, jnp.zeros_like(q))
    m_ref[...] = jnp.full(m_ref.shape, NEG, jnp.float32)
    acc_ref[...] = jnp.zeros(acc_ref.shape, jnp.float32)

    def chunk(jc, kind):
        rows = pl.ds(pl.multiple_of(jc * tq, tq), tq)
        kb = kc_ref[0, rows, :]
        for mm in range(n_maps):
            hh = mm // 2
            s = _dot_nt(qm_ref[mm], kb)
            if kind == "diag":
                s = jnp.where(causal, s + bd_ref[hh], NEG)
            elif kind == "prev":
                s = s + bp_ref[hh]
            _flash_update(s, m_ref.at[mm], acc_ref.at[mm], vc_ref[0, rows, hh * LANES:(hh + 1) * LANES])

    def body(jc, carry):
        chunk(jc, "far")
        return carry

    lax.fori_loop(0, i - 1, body, 0)

    @pl.when(i >= 1)
    def _():
        chunk(i - 1, "prev")

    chunk(i, "diag")

    lam = lam_ref[...]
    outs = []
    for hh in range(H_DIFF):
        o = _normalized(acc_ref[2 * hh], DIFF_V) - lam * _normalized(acc_ref[2 * hh + 1], DIFF_V)
        o = o * lax.rsqrt(jnp.mean(o * o, axis=-1, keepdims=True) + EPS)
        outs.append(o * g_ref[...] * lam_scale)
    o_ref[0] = jnp.concatenate(outs, axis=-1).astype(o_ref.dtype)


def _diff_call(qc, kc, vc, bias_diag, bias_prev, lam, subln, lam_init):
    b, s, _ = qc.shape
    tq = TQ
    assert s % tq == 0 and tq + 1 >= MAX_DISTANCE
    width = 2 * H_DIFF * DIFF_QK
    qblk = lambda w: pl.BlockSpec((1, tq, w), lambda bb, i: (bb, i, 0))
    kblk = lambda w: pl.BlockSpec((1, s, w), lambda bb, i: (bb, 0, 0))
    bblk = pl.BlockSpec((H_DIFF, tq, tq), lambda bb, i: (0, 0, 0))
    return pl.pallas_call(
        functools.partial(_diff_kernel, 1.0 - lam_init),
        grid=(b, s // tq),
        in_specs=[qblk(width), kblk(width), kblk(H_DIFF * LANES), bblk, bblk,
                  pl.BlockSpec((1, 1), lambda bb, i: (0, 0)), pl.BlockSpec((1, DIFF_V), lambda bb, i: (0, 0))],
        out_specs=qblk(H_DIFF * DIFF_V),
        out_shape=jax.ShapeDtypeStruct((b, s, H_DIFF * DIFF_V), MXU_DTYPE),
        scratch_shapes=[pltpu.VMEM((2 * H_DIFF, tq, width), MXU_DTYPE),
                        pltpu.VMEM((2 * H_DIFF, tq, 1), jnp.float32),
                        pltpu.VMEM((2 * H_DIFF, tq, LANES), jnp.float32)],
        compiler_params=pltpu.CompilerParams(dimension_semantics=("arbitrary",) * 2, vmem_limit_bytes=VMEM_LIMIT),
        name="diff",
    )(qc, kc, vc, bias_diag, bias_prev, lam, subln)


def _weight_copies(wup_hbm, wdn_hbm, wup_ref, wdn_ref, sem):
    return (pltpu.make_async_copy(wup_hbm, wup_ref, sem.at[0]), pltpu.make_async_copy(wdn_hbm, wdn_ref, sem.at[1]))


def _out_mlp_kernel(final, x_ref, oa_ref, ob_ref, oc_ref, wo_ref, g_ref, wup_hbm, wdn_hbm, gf_ref, y_ref,
                    wup_ref, wdn_ref, sem):
    @pl.when(pl.program_id(0) == 0)
    def _():
        copies = _weight_copies(wup_hbm, wdn_hbm, wup_ref, wdn_ref, sem)
        for c in copies:
            c.start()
        for c in copies:
            c.wait()

    mix = jnp.concatenate([oa_ref[...], ob_ref[...], oc_ref[...]], axis=-1)
    x = x_ref[...] + _dot(mix, wo_ref[...])
    h = x * lax.rsqrt(jnp.mean(x * x, axis=-1, keepdims=True) + EPS)
    h = (h * g_ref[...]).astype(MXU_DTYPE)
    y = x
    for c in range(D_FF // FF_CHUNK):
        cs = slice(c * FF_CHUNK, (c + 1) * FF_CHUNK)
        u = jnp.maximum(_dot(h, wup_ref[:, cs]), 0.0)
        y = y + _dot((u * u).astype(MXU_DTYPE), wdn_ref[cs, :])
    if final:
        y = y * lax.rsqrt(jnp.mean(y * y, axis=-1, keepdims=True) + EPS) * gf_ref[...]
    y_ref[...] = y


def _out_mlp_call(x2d, oa, ob, oc, wo, g, wup, wdn, gf, final):
    t = x2d.shape[0]
    tm = TM_MLP
    assert t % tm == 0
    row = lambda a: pl.BlockSpec((tm, a.shape[1]), lambda i: (i, 0))
    full = lambda a: pl.BlockSpec(a.shape, lambda i: (0, 0))
    hbm = pl.BlockSpec(memory_space=pl.ANY)
    return pl.pallas_call(
        functools.partial(_out_mlp_kernel, final),
        grid=(t // tm,),
        in_specs=[row(x2d), row(oa), row(ob), row(oc), full(wo), full(g), hbm, hbm, full(gf)],
        out_specs=row(x2d),
        out_shape=jax.ShapeDtypeStruct(x2d.shape, jnp.float32),
        scratch_shapes=[pltpu.VMEM(wup.shape, wup.dtype), pltpu.VMEM(wdn.shape, wdn.dtype),
                        pltpu.SemaphoreType.DMA((2,))],
        compiler_params=pltpu.CompilerParams(dimension_semantics=("arbitrary",), vmem_limit_bytes=VMEM_LIMIT),
        name="out_mlp",
    )(x2d, oa, ob, oc, wo, g, wup, wdn, gf)


def _place(dst_cols, pieces):
    rows = pieces[0][1].shape[0]
    out = jnp.zeros((rows, dst_cols), jnp.float32)
    for off, a in pieces:
        out = lax.dynamic_update_slice(out, a.astype(jnp.float32), (0, off))
    return out


def _rot_half_cols(w):
    half = w.shape[1] // 2
    return jnp.concatenate([-w[:, half:], w[:, :half]], axis=1)


def _prep_in_proj(w_in):
    o = _OFF
    col = lambda k: w_in[:, o[k]:o[k + 1]]
    kr = col(2)
    s_dsa = (DSA_DIM ** -0.5) * LOG2E
    s_diff = (DIFF_QK ** -0.5) * LOG2E
    s_idx = (D_IDX ** -0.5) * (H_IDX ** -0.5)
    pieces = [(C_CQ, col(0)), (C_CKV, col(1)),
              (C_KR + MLA_NOPE, kr), (C_KRR + MLA_NOPE, _rot_half_cols(kr)),
              (C_KB, col(4)), (C_VB, col(5)), (C_QI, col(6)),
              (C_WI, col(8) * s_idx), (C_QC, col(9) * s_diff), (C_KC, col(10))]
    qb = col(3)
    for hh in range(H_DSA):
        pieces.append((C_QB + hh * LANES, qb[:, hh * DSA_DIM:(hh + 1) * DSA_DIM] * s_dsa))
    for hh in range(H_IDX):
        pieces.append((C_KI + hh * D_IDX, col(7)))
    vc = col(11)
    for hh in range(H_DIFF):
        pieces.append((C_VC + hh * LANES, vc[:, hh * DIFF_V:(hh + 1) * DIFF_V]))
    return _place(N1, pieces).astype(MXU_DTYPE)


def _prep_mla_up(w_uq, w_ukv):
    dq = MLA_NOPE + MLA_ROPE
    s_mla = (dq ** -0.5) * LOG2E
    q_p, qr_p, k_p, v_p = [], [], [], []
    for hh in range(H_MLA):
        wq = w_uq[:, hh * dq:(hh + 1) * dq] * s_mla
        q_p.append((hh * LANES, wq))
        qr_p.append((hh * LANES + MLA_NOPE, _rot_half_cols(wq[:, MLA_NOPE:])))
        wkv = w_ukv[:, hh * (MLA_NOPE + MLA_V):(hh + 1) * (MLA_NOPE + MLA_V)]
        k_p.append((hh * LANES, wkv[:, :MLA_NOPE]))
        v_p.append((hh * LANES, wkv[:, MLA_NOPE:]))
    width = H_MLA * LANES
    return tuple(_place(width, p).astype(MXU_DTYPE) for p in (q_p, qr_p, k_p, v_p))


def _rope_tables(positions):
    half = MLA_ROPE // 2
    freqs = ROPE_BASE ** (-jnp.arange(half, dtype=jnp.float32) / half)
    ang = positions.astype(jnp.float32)[..., None] * freqs
    cos, sin = jnp.cos(ang), jnp.sin(ang)
    lead = cos.shape[:-1]
    ones = jnp.ones(lead + (MLA_NOPE,), jnp.float32)
    zeros = jnp.zeros(lead + (LANES - MLA_NOPE - MLA_ROPE,), jnp.float32)
    cos_t = jnp.concatenate([ones, cos, cos, zeros], axis=-1)
    sin_t = jnp.concatenate([0.0 * ones, sin, sin, zeros], axis=-1)
    return cos_t.reshape(-1, LANES), sin_t.reshape(-1, LANES)


def _t5_bucket(rel):
    n = jnp.maximum(rel, 0)
    max_exact = N_BUCKETS // 2
    nf = jnp.maximum(n, 1).astype(jnp.float32)
    large = max_exact + (jnp.log(nf / max_exact) / math.log(MAX_DISTANCE / max_exact)
                         * (N_BUCKETS - max_exact)).astype(jnp.int32)
    large = jnp.minimum(large, N_BUCKETS - 1)
    return jnp.where(n < max_exact, n, large)


def _bias_tiles(rel_bias, tq):
    r = jnp.arange(tq, dtype=jnp.int32)
    rel = r[:, None] - r[None, :]
    far = rel_bias[N_BUCKETS - 1]
    tile = lambda d: jnp.transpose((rel_bias[_t5_bucket(d)] - far) * LOG2E, (2, 0, 1))
    return tile(rel), tile(rel + tq)


def kernel(x, positions, rel_bias, norm_attn, w_in, q_norm, w_uq, kv_norm, w_ukv, diff_lambda, diff_subln,
           w_out, norm_mlp, w_up, w_down, norm_final):
    b, s, d = x.shape
    depth = w_in.shape[0]
    cos_t, sin_t = _rope_tables(positions)
    bias_diag, bias_prev = _bias_tiles(rel_bias.astype(jnp.float32), TQ)
    row2 = lambda v: v.reshape(1, -1).astype(jnp.float32)
    x2d = x.reshape(b * s, d)
    for l in range(depth):
        w1 = _prep_in_proj(w_in[l])
        wuq, wuqr, wuk, wuv = _prep_mla_up(w_uq[l], w_ukv[l])
        qm, km, vm, qd, kd, vd, qi, ki, wi, qc, kc, vc = _proj_call(
            x2d, row2(norm_attn[l]), w1, row2(q_norm[l]), wuq, wuqr, row2(kv_norm[l]), wuk, wuv, cos_t, sin_t)
        r3 = lambda a: a.reshape(b, s, a.shape[-1])
        o_a = _mla_call(r3(qm), r3(km), r3(vm))
        o_b = _dsa_call(r3(qd), r3(qi), r3(wi), r3(ki), r3(kd), r3(vd), bias_diag[:H_DSA], bias_prev[:H_DSA])
        lam_init = 0.8 - 0.6 * math.exp(-0.3 * l)
        lp = diff_lambda[l].astype(jnp.float32)
        lam = (jnp.exp(jnp.sum(lp[0] * lp[1])) - jnp.exp(jnp.sum(lp[2] * lp[3])) + lam_init).reshape(1, 1)
        o_c = _diff_call(r3(qc), r3(kc), r3(vc), bias_diag[H_DSA:], bias_prev[H_DSA:], lam, row2(diff_subln[l]), lam_init)
        f2 = lambda a: a.reshape(b * s, a.shape[-1])
        x2d = _out_mlp_call(x2d, f2(o_a), f2(o_b), f2(o_c), w_out[l].astype(MXU_DTYPE), row2(norm_mlp[l]),
                            w_up[l].astype(MXU_DTYPE), w_down[l].astype(MXU_DTYPE), row2(norm_final), l == depth - 1)
    return x2d.reshape(b, s, d)
```

```python
import functools
import math

import jax
import jax.numpy as jnp
import numpy as np
from jax import lax
from jax.experimental import pallas as pl
from jax.experimental.pallas import tpu as pltpu

D_MODEL = 1024
H_MLA, MLA_NOPE, MLA_ROPE, MLA_V = 8, 64, 32, 64
Q_LORA, KV_LORA = 384, 256
H_DSA, DSA_DIM, H_IDX, D_IDX, TOPK_MAX = 4, 64, 8, 32, 256
H_DIFF, DIFF_QK = 4, 32
DIFF_V = 2 * DIFF_QK
D_FF = 4 * D_MODEL
N_BUCKETS, MAX_DISTANCE = 32, 128
ROPE_BASE = 10000.0
EPS = 1e-6
NEG = -1e30
LOG2E = math.log2(math.e)

_SPLITS = (Q_LORA, KV_LORA, MLA_ROPE, H_DSA * DSA_DIM, DSA_DIM, DSA_DIM, H_IDX * D_IDX, D_IDX, H_IDX,
           H_DIFF * 2 * DIFF_QK, H_DIFF * 2 * DIFF_QK, H_DIFF * DIFF_V)
_OFF = tuple(int(o) for o in np.concatenate([[0], np.cumsum(_SPLITS)]))

LANES = 128
MXU_DTYPE = jnp.bfloat16
VMEM_LIMIT = 56 * 1024 * 1024

C_CQ = 0
C_CKV = C_CQ + Q_LORA
C_KR = C_CKV + KV_LORA
C_KRR = C_KR + LANES
C_KB = C_KRR + LANES
C_KI = C_KB + LANES
C_KC = C_KI + H_IDX * D_IDX
N1 = C_KC + H_DIFF * 2 * DIFF_QK
R_QB = 0
R_QI = R_QB + H_DSA * LANES
R_QC = R_QI + H_IDX * D_IDX
R_VB = R_QC + H_DIFF * 2 * DIFF_QK
R_VC = R_VB + LANES
R_WI = R_VC + H_DIFF * LANES
WI_ROWS = 16
N1T = R_WI + LANES

ONES_ROW = 64

TQ = 256
MLA_HEADS_PER_STEP = 4
TM_MLP = 512
FF_CHUNK = 1024

_NT = (((1,), (1,)), ((), ()))


def _dot(a, b):
    return jnp.dot(a, b, preferred_element_type=jnp.float32)


def _dot_nt(a, b):
    return lax.dot_general(a, b, _NT, preferred_element_type=jnp.float32)


def _ones_rows(rows):
    r = lax.broadcasted_iota(jnp.int32, (rows, 1), 0)
    return jnp.where(r % LANES == ONES_ROW, 1.0, 0.0).astype(jnp.float32)


def _proj_kernel(x_ref, g_ref, w1_ref, w1t_ref, qn_ref, wuqt_ref, wuqrt_ref, kvn_ref, wuk_ref, wuvt_ref,
                 c_ref, s_ref, ct_ref, st_ref,
                 qmt_ref, km_ref, vmt_ref, qdt_ref, kd_ref, vdt_ref, qit_ref, ki_ref, wit_ref, qct_ref, kc_ref, vct_ref):
    x = x_ref[...]
    h = x * lax.rsqrt(jnp.mean(x * x, axis=-1, keepdims=True) + EPS)
    h = (h * g_ref[...]).astype(MXU_DTYPE)

    def seg(a, b):
        return _dot(h, w1_ref[:, a:b])

    def seg_t(a, b):
        return _dot_nt(w1t_ref[a:b, :], h)

    c_q = seg(C_CQ, C_CKV)
    c_q = c_q * lax.rsqrt(jnp.mean(c_q * c_q, axis=-1, keepdims=True) + EPS)
    c_q = (c_q * qn_ref[...]).astype(MXU_DTYPE)
    qa = _dot_nt(wuqt_ref[...], c_q)
    qb = _dot_nt(wuqrt_ref[...], c_q)
    cos_t = ct_ref[0]
    sin_t = st_ref[0]
    for hh in range(H_MLA):
        sl = slice(hh * LANES, (hh + 1) * LANES)
        qmt_ref[0, sl, :] = (qa[sl, :] * cos_t + qb[sl, :] * sin_t).astype(qmt_ref.dtype)

    c_kv = seg(C_CKV, C_KR)
    c_kv = c_kv * lax.rsqrt(jnp.mean(c_kv * c_kv, axis=-1, keepdims=True) + EPS)
    c_kv = (c_kv * kvn_ref[...]).astype(MXU_DTYPE)
    k_rope = seg(C_KR, C_KRR) * c_ref[...] + seg(C_KRR, C_KB) * s_ref[...]
    kk = _dot(c_kv, wuk_ref[...])
    for hh in range(H_MLA):
        sl = slice(hh * LANES, (hh + 1) * LANES)
        km_ref[:, sl] = (kk[:, sl] + k_rope).astype(km_ref.dtype)
    vmt_ref[0] = (_dot_nt(wuvt_ref[...], c_kv) + _ones_rows(H_MLA * LANES)).astype(vmt_ref.dtype)

    kd_ref[...] = seg(C_KB, C_KI).astype(kd_ref.dtype)
    ki_ref[...] = seg(C_KI, C_KC).astype(ki_ref.dtype)
    kc_ref[...] = seg(C_KC, N1).astype(kc_ref.dtype)
    qdt_ref[0] = seg_t(R_QB, R_QI).astype(qdt_ref.dtype)
    qit_ref[0] = seg_t(R_QI, R_QC).astype(qit_ref.dtype)
    qct_ref[0] = seg_t(R_QC, R_VB).astype(qct_ref.dtype)
    vdt_ref[0] = (seg_t(R_VB, R_VC) + _ones_rows(LANES)).astype(vdt_ref.dtype)
    vct_ref[0] = (seg_t(R_VC, R_WI) + _ones_rows(H_DIFF * LANES)).astype(vct_ref.dtype)
    wit_ref[0] = seg_t(R_WI, R_WI + WI_ROWS)[:H_IDX, :]


def _proj_call(x2d, g, w1, w1t, qn, wuqt, wuqrt, kvn, wuk, wuvt, cos_r, sin_r, cos_c, sin_c):
    t = x2d.shape[0]
    tm = TQ
    assert t % tm == 0
    n = t // tm
    row = lambda w: pl.BlockSpec((tm, w), lambda i: (i, 0))
    colmajor = lambda r: pl.BlockSpec((1, r, tm), lambda i: (i, 0, 0))
    full = lambda a: pl.BlockSpec(a.shape, lambda i: (0, 0))
    bf = MXU_DTYPE
    width_c = 2 * H_DIFF * DIFF_QK
    outs = [
        (colmajor(H_MLA * LANES), (n, H_MLA * LANES, tm), bf),
        (row(H_MLA * LANES), (t, H_MLA * LANES), bf),
        (colmajor(H_MLA * LANES), (n, H_MLA * LANES, tm), bf),
        (colmajor(H_DSA * LANES), (n, H_DSA * LANES, tm), bf),
        (row(LANES), (t, LANES), bf),
        (colmajor(LANES), (n, LANES, tm), bf),
        (colmajor(H_IDX * D_IDX), (n, H_IDX * D_IDX, tm), bf),
        (row(H_IDX * D_IDX), (t, H_IDX * D_IDX), bf),
        (colmajor(H_IDX), (n, H_IDX, tm), jnp.float32),
        (colmajor(width_c), (n, width_c, tm), bf),
        (row(width_c), (t, width_c), bf),
        (colmajor(H_DIFF * LANES), (n, H_DIFF * LANES, tm), bf),
    ]
    return pl.pallas_call(
        _proj_kernel,
        grid=(n,),
        in_specs=[row(D_MODEL), full(g), full(w1), full(w1t), full(qn), full(wuqt), full(wuqrt), full(kvn), full(wuk),
                  full(wuvt), row(LANES), row(LANES), colmajor(LANES), colmajor(LANES)],
        out_specs=[o[0] for o in outs],
        out_shape=[jax.ShapeDtypeStruct(o[1], o[2]) for o in outs],
        compiler_params=pltpu.CompilerParams(dimension_semantics=("arbitrary",), vmem_limit_bytes=VMEM_LIMIT),
        name="proj",
    )(x2d, g, w1, w1t, qn, wuqt, wuqrt, kvn, wuk, wuvt, cos_r, sin_r, cos_c, sin_c)


def _flash_update(s, m_ref, acc_ref, v_t):
    m_prev = m_ref[...]
    m_new = jnp.maximum(m_prev, jnp.max(s, axis=0, keepdims=True))
    alpha = jnp.exp2(m_prev - m_new)
    p = jnp.exp2(s - m_new).astype(MXU_DTYPE)
    acc_ref[...] = alpha * acc_ref[...] + _dot(v_t, p)
    m_ref[...] = m_new


def _normalized(acc, width):
    return acc[:width, :] / acc[ONES_ROW:ONES_ROW + 1, :]


def _causal_tile(tq):
    key = lax.broadcasted_iota(jnp.int32, (tq, tq), 0)
    qry = lax.broadcasted_iota(jnp.int32, (tq, tq), 1)
    return key <= qry


def _pipelined_key_loop(i, n_heads, produce, consume, prepare=None):
    prepare = prepare or (lambda c, kind: None)
    for hh in range(n_heads):
        produce(0, hh)

    def both(c, kind, nxt):
        ctx = prepare(c, kind)
        for hh in range(n_heads):
            consume(c, hh, kind, ctx)
            if nxt:
                produce(c + 1, hh)

    def body(j, carry):
        both(j, "far", True)
        return carry

    lax.fori_loop(0, i - 1, body, 0)

    @pl.when(i >= 1)
    def _():
        both(i - 1, "prev", True)

    both(i, "diag", False)


def _mla_kernel(qt_ref, k_ref, vt_ref, o_ref, s_ref, m_ref, acc_ref):
    i = pl.program_id(2)
    tq = qt_ref.shape[-1]
    nh = MLA_HEADS_PER_STEP
    m_ref[...] = jnp.full(m_ref.shape, NEG, jnp.float32)
    acc_ref[...] = jnp.zeros(acc_ref.shape, jnp.float32)
    causal = _causal_tile(tq)

    def produce(c, hh):
        sl = slice(hh * LANES, (hh + 1) * LANES)
        kb = k_ref[0, pl.ds(pl.multiple_of(c * tq, tq), tq), sl]
        s_ref[c % 2, hh] = _dot(kb, qt_ref[0, 0, sl, :])

    def consume(c, hh, kind, ctx):
        s = s_ref[c % 2, hh]
        if kind == "diag":
            s = jnp.where(causal, s, NEG)
        _flash_update(s, m_ref.at[hh], acc_ref.at[hh], vt_ref[0, c, hh * LANES:(hh + 1) * LANES, :])

    _pipelined_key_loop(i, nh, produce, consume)
    o_ref[0] = jnp.concatenate([_normalized(acc_ref[hh], MLA_V).T for hh in range(nh)], axis=-1).astype(o_ref.dtype)


def _mla_call(qmt, km, vmt):
    b, nk, _, tq = qmt.shape
    s = nk * tq
    nh = MLA_HEADS_PER_STEP
    return pl.pallas_call(
        _mla_kernel,
        grid=(b, H_MLA // nh, nk),
        in_specs=[pl.BlockSpec((1, 1, nh * LANES, tq), lambda bb, hp, i: (bb, i, hp, 0)),
                  pl.BlockSpec((1, s, nh * LANES), lambda bb, hp, i: (bb, 0, hp)),
                  pl.BlockSpec((1, nk, nh * LANES, tq), lambda bb, hp, i: (bb, 0, hp, 0))],
        out_specs=pl.BlockSpec((1, tq, nh * MLA_V), lambda bb, hp, i: (bb, i, hp)),
        out_shape=jax.ShapeDtypeStruct((b, s, H_MLA * MLA_V), MXU_DTYPE),
        scratch_shapes=[pltpu.VMEM((2, nh, tq, tq), jnp.float32),
                        pltpu.VMEM((nh, 1, tq), jnp.float32), pltpu.VMEM((nh, LANES, tq), jnp.float32)],
        compiler_params=pltpu.CompilerParams(dimension_semantics=("arbitrary",) * 3, vmem_limit_bytes=VMEM_LIMIT),
        name="mla",
    )(qmt, km, vmt)


def _ordered_to_float(k):
    bits = k ^ ((k >> 31) & jnp.int32(0x7FFFFFFF))
    return lax.bitcast_convert_type(bits, jnp.float32)


def _dsa_kernel(n_sel, qdt_ref, qit_ref, wit_ref, ki_ref, kd_ref, vdt_ref, bd_ref, bp_ref, o_ref,
                sc_ref, qim_ref, thr_ref, s_ref, m_ref, acc_ref):
    i = pl.program_id(1)
    tq = qdt_ref.shape[-1]
    n_chunks = i + 1
    causal = _causal_tile(tq)
    int_min = jnp.int32(-2 ** 31)

    qi = qit_ref[0, 0]
    row_head = lax.broadcasted_iota(jnp.int32, qi.shape, 0) // D_IDX
    for hh in range(H_IDX):
        qim_ref[hh] = jnp.where(row_head == hh, qi, jnp.zeros_like(qi))

    def score_chunk(jc, diag):
        kc = ki_ref[0, pl.ds(pl.multiple_of(jc * tq, tq), tq), :]
        w = wit_ref[0, 0]
        acc = jnp.zeros((tq, tq), jnp.float32)
        for hh in range(H_IDX):
            acc = acc + jnp.maximum(_dot(kc, qim_ref[hh]), 0.0) * w[hh:hh + 1, :]
        if diag:
            acc = jnp.where(causal, acc, NEG)
        sc_ref[jc] = acc

    def score_body(jc, carry):
        score_chunk(jc, False)
        return carry

    lax.fori_loop(0, i, score_body, 0)
    score_chunk(i, True)

    def count_where(hits):
        def body(jc, acc):
            return acc + jnp.sum(hits(sc_ref[jc], jc), axis=0, keepdims=True)
        return lax.fori_loop(0, n_chunks, body, jnp.zeros((1, tq), jnp.float32))

    def count_ge(thr):
        return count_where(lambda sc, jc: jnp.where(sc >= thr, 1.0, 0.0))

    k0 = jnp.where(count_ge(jnp.zeros((1, tq), jnp.float32)) >= n_sel, jnp.int32(0), int_min)

    def bit_body(t, k):
        cand = k + lax.shift_left(jnp.int32(1), jnp.int32(30) - t)
        return jnp.where(count_ge(_ordered_to_float(cand)) >= n_sel, cand, k)

    thr = _ordered_to_float(lax.fori_loop(0, 31, bit_body, k0))
    thr_ref[...] = thr

    @pl.when(jnp.max(count_ge(thr)) > n_sel)
    def _():
        thr_ = thr_ref[...]
        need = n_sel - count_where(lambda sc, jc: jnp.where(sc > thr_, 1.0, 0.0))

        def key_index(jc):
            return jc * tq + lax.broadcasted_iota(jnp.int32, (tq, tq), 0)

        def tie_before(bound):
            return count_where(lambda sc, jc: jnp.where(key_index(jc) < bound, jnp.where(sc == thr_, 1.0, 0.0), 0.0))

        def idx_body(t, jb):
            cand = jb + lax.shift_left(jnp.int32(1), jnp.int32(23) - t)
            return jnp.where(tie_before(cand) <= need, cand, jb)

        bound = lax.fori_loop(0, 24, idx_body, jnp.zeros((1, tq), jnp.int32))

        def drop_body(jc, carry):
            sc = sc_ref[jc]
            sc_ref[jc] = jnp.where(key_index(jc) >= bound, jnp.where(sc == thr_, -jnp.inf, sc), sc)
            return carry

        lax.fori_loop(0, n_chunks, drop_body, 0)

    thr = thr_ref[...]

    m_ref[...] = jnp.full(m_ref.shape, NEG, jnp.float32)
    acc_ref[...] = jnp.zeros(acc_ref.shape, jnp.float32)

    def produce(c, hh):
        kb = kd_ref[0, pl.ds(pl.multiple_of(c * tq, tq), tq), :]
        s_ref[c % 2, hh] = _dot(kb, qdt_ref[0, 0, hh * LANES:(hh + 1) * LANES, :])

    def prepare(c, kind):
        return sc_ref[c] >= thr

    def consume(c, hh, kind, sel):
        s = s_ref[c % 2, hh]
        if kind == "diag":
            s = jnp.where(causal, s + bd_ref[hh], NEG)
        elif kind == "prev":
            s = s + bp_ref[hh]
        s = jnp.where(sel, s, NEG)
        _flash_update(s, m_ref.at[hh], acc_ref.at[hh], vdt_ref[0, c])

    _pipelined_key_loop(i, H_DSA, produce, consume, prepare)
    o_ref[0] = jnp.concatenate([_normalized(acc_ref[hh], DSA_DIM).T for hh in range(H_DSA)], axis=-1).astype(o_ref.dtype)


def _dsa_call(qdt, qit, wit, ki, kd, vdt, bias_diag, bias_prev):
    b, nk, _, tq = qdt.shape
    s = nk * tq
    n_sel = min(TOPK_MAX, s // 4)
    assert tq >= n_sel and tq + 1 >= MAX_DISTANCE
    qblk = lambda r: pl.BlockSpec((1, 1, r, tq), lambda bb, i: (bb, i, 0, 0))
    kblk = lambda w: pl.BlockSpec((1, s, w), lambda bb, i: (bb, 0, 0))
    bblk = pl.BlockSpec((H_DSA, tq, tq), lambda bb, i: (0, 0, 0))
    return pl.pallas_call(
        functools.partial(_dsa_kernel, n_sel),
        grid=(b, nk),
        in_specs=[qblk(H_DSA * LANES), qblk(H_IDX * D_IDX), qblk(H_IDX),
                  kblk(H_IDX * D_IDX), kblk(LANES),
                  pl.BlockSpec((1, nk, LANES, tq), lambda bb, i: (bb, 0, 0, 0)), bblk, bblk],
        out_specs=pl.BlockSpec((1, tq, H_DSA * DSA_DIM), lambda bb, i: (bb, i, 0)),
        out_shape=jax.ShapeDtypeStruct((b, s, H_DSA * DSA_DIM), MXU_DTYPE),
        scratch_shapes=[pltpu.VMEM((nk, tq, tq), jnp.float32),
                        pltpu.VMEM((H_IDX, H_IDX * D_IDX, tq), MXU_DTYPE),
                        pltpu.VMEM((1, tq), jnp.float32),
                        pltpu.VMEM((2, H_DSA, tq, tq), jnp.float32),
                        pltpu.VMEM((H_DSA, 1, tq), jnp.float32),
                        pltpu.VMEM((H_DSA, LANES, tq), jnp.float32)],
        compiler_params=pltpu.CompilerParams(dimension_semantics=("arbitrary",) * 2, vmem_limit_bytes=VMEM_LIMIT),
        name="dsa",
    )(qdt, qit, wit, ki, kd, vdt, bias_diag, bias_prev)


def _diff_kernel(lam_scale, qct_ref, kc_ref, vct_ref, bd_ref, bp_ref, lam_ref, g_ref, o_ref,
                 qm_ref, s_ref, m_ref, acc_ref):
    i = pl.program_id(1)
    tq = qct_ref.shape[-1]
    n_maps = 2 * H_DIFF
    causal = _causal_tile(tq)

    q = qct_ref[0, 0]
    row_map = lax.broadcasted_iota(jnp.int32, q.shape, 0) // DIFF_QK
    for mm in range(n_maps):
        qm_ref[mm] = jnp.where(row_map == mm, q, jnp.zeros_like(q))
    m_ref[...] = jnp.full(m_ref.shape, NEG, jnp.float32)
    acc_ref[...] = jnp.zeros(acc_ref.shape, jnp.float32)

    def produce(c, mm):
        kb = kc_ref[0, pl.ds(pl.multiple_of(c * tq, tq), tq), :]
        s_ref[c % 2, mm] = _dot(kb, qm_ref[mm])

    def consume(c, mm, kind, ctx):
        hh = mm // 2
        s = s_ref[c % 2, mm]
        if kind == "diag":
            s = jnp.where(causal, s + bd_ref[hh], NEG)
        elif kind == "prev":
            s = s + bp_ref[hh]
        _flash_update(s, m_ref.at[mm], acc_ref.at[mm], vct_ref[0, c, hh * LANES:(hh + 1) * LANES, :])

    _pipelined_key_loop(i, n_maps, produce, consume)

    lam = lam_ref[...]
    outs = []
    for hh in range(H_DIFF):
        o = _normalized(acc_ref[2 * hh], DIFF_V) - lam * _normalized(acc_ref[2 * hh + 1], DIFF_V)
        o = o * lax.rsqrt(jnp.mean(o * o, axis=0, keepdims=True) + EPS)
        outs.append((o * g_ref[...] * lam_scale).T)
    o_ref[0] = jnp.concatenate(outs, axis=-1).astype(o_ref.dtype)


def _diff_call(qct, kc, vct, bias_diag, bias_prev, lam, subln, lam_init):
    b, nk, width, tq = qct.shape
    s = nk * tq
    assert tq + 1 >= MAX_DISTANCE
    bblk = pl.BlockSpec((H_DIFF, tq, tq), lambda bb, i: (0, 0, 0))
    return pl.pallas_call(
        functools.partial(_diff_kernel, 1.0 - lam_init),
        grid=(b, nk),
        in_specs=[pl.BlockSpec((1, 1, width, tq), lambda bb, i: (bb, i, 0, 0)),
                  pl.BlockSpec((1, s, width), lambda bb, i: (bb, 0, 0)),
                  pl.BlockSpec((1, nk, H_DIFF * LANES, tq), lambda bb, i: (bb, 0, 0, 0)), bblk, bblk,
                  pl.BlockSpec((1, 1), lambda bb, i: (0, 0)), pl.BlockSpec((DIFF_V, 1), lambda bb, i: (0, 0))],
        out_specs=pl.BlockSpec((1, tq, H_DIFF * DIFF_V), lambda bb, i: (bb, i, 0)),
        out_shape=jax.ShapeDtypeStruct((b, s, H_DIFF * DIFF_V), MXU_DTYPE),
        scratch_shapes=[pltpu.VMEM((2 * H_DIFF, width, tq), MXU_DTYPE),
                        pltpu.VMEM((2, 2 * H_DIFF, tq, tq), jnp.float32),
                        pltpu.VMEM((2 * H_DIFF, 1, tq), jnp.float32),
                        pltpu.VMEM((2 * H_DIFF, LANES, tq), jnp.float32)],
        compiler_params=pltpu.CompilerParams(dimension_semantics=("arbitrary",) * 2, vmem_limit_bytes=VMEM_LIMIT),
        name="diff",
    )(qct, kc, vct, bias_diag, bias_prev, lam, subln)


def _weight_copies(wup_hbm, wdn_hbm, wup_ref, wdn_ref, sem):
    return (pltpu.make_async_copy(wup_hbm, wup_ref, sem.at[0]), pltpu.make_async_copy(wdn_hbm, wdn_ref, sem.at[1]))


def _out_mlp_kernel(final, x_ref, oa_ref, ob_ref, oc_ref, wo_ref, g_ref, wup_hbm, wdn_hbm, gf_ref, y_ref,
                    wup_ref, wdn_ref, sem):
    @pl.when(pl.program_id(0) == 0)
    def _():
        copies = _weight_copies(wup_hbm, wdn_hbm, wup_ref, wdn_ref, sem)
        for c in copies:
            c.start()
        for c in copies:
            c.wait()

    mix = jnp.concatenate([oa_ref[...], ob_ref[...], oc_ref[...]], axis=-1)
    x = x_ref[...] + _dot(mix, wo_ref[...])
    h = x * lax.rsqrt(jnp.mean(x * x, axis=-1, keepdims=True) + EPS)
    h = (h * g_ref[...]).astype(MXU_DTYPE)
    y = x
    for c in range(D_FF // FF_CHUNK):
        cs = slice(c * FF_CHUNK, (c + 1) * FF_CHUNK)
        u = jnp.maximum(_dot(h, wup_ref[:, cs]), 0.0)
        y = y + _dot((u * u).astype(MXU_DTYPE), wdn_ref[cs, :])
    if final:
        y = y * lax.rsqrt(jnp.mean(y * y, axis=-1, keepdims=True) + EPS) * gf_ref[...]
    y_ref[...] = y


def _out_mlp_call(x2d, oa, ob, oc, wo, g, wup, wdn, gf, final):
    t = x2d.shape[0]
    tm = TM_MLP
    assert t % tm == 0
    row = lambda a: pl.BlockSpec((tm, a.shape[1]), lambda i: (i, 0))
    full = lambda a: pl.BlockSpec(a.shape, lambda i: (0, 0))
    hbm = pl.BlockSpec(memory_space=pl.ANY)
    return pl.pallas_call(
        functools.partial(_out_mlp_kernel, final),
        grid=(t // tm,),
        in_specs=[row(x2d), row(oa), row(ob), row(oc), full(wo), full(g), hbm, hbm, full(gf)],
        out_specs=row(x2d),
        out_shape=jax.ShapeDtypeStruct(x2d.shape, jnp.float32),
        scratch_shapes=[pltpu.VMEM(wup.shape, wup.dtype), pltpu.VMEM(wdn.shape, wdn.dtype),
                        pltpu.SemaphoreType.DMA((2,))],
        compiler_params=pltpu.CompilerParams(dimension_semantics=("arbitrary",), vmem_limit_bytes=VMEM_LIMIT),
        name="out_mlp",
    )(x2d, oa, ob, oc, wo, g, wup, wdn, gf)


def _place(dst_cols, pieces):
    rows = pieces[0][1].shape[0]
    out = jnp.zeros((rows, dst_cols), jnp.float32)
    for off, a in pieces:
        out = lax.dynamic_update_slice(out, a.astype(jnp.float32), (0, off))
    return out


def _rot_half_cols(w):
    half = w.shape[1] // 2
    return jnp.concatenate([-w[:, half:], w[:, :half]], axis=1)


def _prep_in_proj(w_in):
    o = _OFF
    col = lambda k: w_in[:, o[k]:o[k + 1]]
    kr = col(2)
    s_dsa = (DSA_DIM ** -0.5) * LOG2E
    s_diff = (DIFF_QK ** -0.5) * LOG2E
    s_idx = (D_IDX ** -0.5) * (H_IDX ** -0.5)
    pieces = [(C_CQ, col(0)), (C_CKV, col(1)),
              (C_KR + MLA_NOPE, kr), (C_KRR + MLA_NOPE, _rot_half_cols(kr)),
              (C_KB, col(4)), (C_KC, col(10))]
    for hh in range(H_IDX):
        pieces.append((C_KI + hh * D_IDX, col(7)))
    t_pieces = [(R_QI, col(6)), (R_QC, col(9) * s_diff), (R_VB, col(5)), (R_WI, col(8) * s_idx)]
    qb, vc = col(3), col(11)
    for hh in range(H_DSA):
        t_pieces.append((R_QB + hh * LANES, qb[:, hh * DSA_DIM:(hh + 1) * DSA_DIM] * s_dsa))
    for hh in range(H_DIFF):
        t_pieces.append((R_VC + hh * LANES, vc[:, hh * DIFF_V:(hh + 1) * DIFF_V]))
    return _place(N1, pieces).astype(MXU_DTYPE), _place(N1T, t_pieces).T.astype(MXU_DTYPE)


def _prep_mla_up(w_uq, w_ukv):
    dq = MLA_NOPE + MLA_ROPE
    s_mla = (dq ** -0.5) * LOG2E
    q_p, qr_p, k_p, v_p = [], [], [], []
    for hh in range(H_MLA):
        wq = w_uq[:, hh * dq:(hh + 1) * dq] * s_mla
        q_p.append((hh * LANES, wq))
        qr_p.append((hh * LANES + MLA_NOPE, _rot_half_cols(wq[:, MLA_NOPE:])))
        wkv = w_ukv[:, hh * (MLA_NOPE + MLA_V):(hh + 1) * (MLA_NOPE + MLA_V)]
        k_p.append((hh * LANES, wkv[:, :MLA_NOPE]))
        v_p.append((hh * LANES, wkv[:, MLA_NOPE:]))
    width = H_MLA * LANES
    wuq, wuqr, wuk, wuv = (_place(width, p) for p in (q_p, qr_p, k_p, v_p))
    return wuq.T.astype(MXU_DTYPE), wuqr.T.astype(MXU_DTYPE), wuk.astype(MXU_DTYPE), wuv.T.astype(MXU_DTYPE)


def _rope_tables(positions, tm):
    half = MLA_ROPE // 2
    freqs = ROPE_BASE ** (-jnp.arange(half, dtype=jnp.float32) / half)
    ang = positions.astype(jnp.float32)[..., None] * freqs
    cos, sin = jnp.cos(ang), jnp.sin(ang)
    lead = cos.shape[:-1]
    ones = jnp.ones(lead + (MLA_NOPE,), jnp.float32)
    zeros = jnp.zeros(lead + (LANES - MLA_NOPE - MLA_ROPE,), jnp.float32)
    cos_r = jnp.concatenate([ones, cos, cos, zeros], axis=-1).reshape(-1, LANES)
    sin_r = jnp.concatenate([0.0 * ones, sin, sin, zeros], axis=-1).reshape(-1, LANES)
    to_cols = lambda a: jnp.transpose(a.reshape(-1, tm, LANES), (0, 2, 1))
    return cos_r, sin_r, to_cols(cos_r), to_cols(sin_r)


def _t5_bucket(rel):
    n = jnp.maximum(rel, 0)
    max_exact = N_BUCKETS // 2
    nf = jnp.maximum(n, 1).astype(jnp.float32)
    large = max_exact + (jnp.log(nf / max_exact) / math.log(MAX_DISTANCE / max_exact)
                         * (N_BUCKETS - max_exact)).astype(jnp.int32)
    large = jnp.minimum(large, N_BUCKETS - 1)
    return jnp.where(n < max_exact, n, large)


def _bias_tiles(rel_bias, tq):
    n_heads = rel_bias.shape[1]
    dist = jnp.arange(2 * tq, dtype=jnp.int32)
    b = ((rel_bias[_t5_bucket(dist)] - rel_bias[N_BUCKETS - 1]) * LOG2E).T

    def toeplitz(u):
        flat = jnp.tile(u, (1, tq))[:, :tq * (2 * tq - 1)]
        return flat.reshape(n_heads, tq, 2 * tq - 1)[:, :, :tq]

    diag = toeplitz(jnp.concatenate([b[:, :tq], jnp.zeros_like(b[:, :tq])], axis=1))
    prev = toeplitz(jnp.concatenate([b[:, tq:], b[:, :tq]], axis=1))
    return diag, prev


def kernel(x, positions, rel_bias, norm_attn, w_in, q_norm, w_uq, kv_norm, w_ukv, diff_lambda, diff_subln,
           w_out, norm_mlp, w_up, w_down, norm_final):
    b, s, d = x.shape
    depth = w_in.shape[0]
    assert s % TQ == 0
    nk = s // TQ
    cos_r, sin_r, cos_c, sin_c = _rope_tables(positions, TQ)
    bias_diag, bias_prev = _bias_tiles(rel_bias.astype(jnp.float32), TQ)
    row2 = lambda v: v.reshape(1, -1).astype(jnp.float32)
    x2d = x.reshape(b * s, d)
    for l in range(depth):
        w1, w1t = _prep_in_proj(w_in[l])
        wuqt, wuqrt, wuk, wuvt = _prep_mla_up(w_uq[l], w_ukv[l])
        qmt, km, vmt, qdt, kd, vdt, qit, ki, wit, qct, kc, vct = _proj_call(
            x2d, row2(norm_attn[l]), w1, w1t, row2(q_norm[l]), wuqt, wuqrt, row2(kv_norm[l]), wuk, wuvt,
            cos_r, sin_r, cos_c, sin_c)
        r3 = lambda a: a.reshape(b, s, a.shape[-1])
        r4 = lambda a: a.reshape(b, nk, a.shape[-2], a.shape[-1])
        o_a = _mla_call(r4(qmt), r3(km), r4(vmt))
        o_b = _dsa_call(r4(qdt), r4(qit), r4(wit), r3(ki), r3(kd), r4(vdt), bias_diag[:H_DSA], bias_prev[:H_DSA])
        lam_init = 0.8 - 0.6 * math.exp(-0.3 * l)
        lp = diff_lambda[l].astype(jnp.float32)
        lam = (jnp.exp(jnp.sum(lp[0] * lp[1])) - jnp.exp(jnp.sum(lp[2] * lp[3])) + lam_init).reshape(1, 1)
        o_c = _diff_call(r4(qct), r3(kc), r4(vct), bias_diag[H_DSA:], bias_prev[H_DSA:], lam,
                         diff_subln[l].reshape(-1, 1).astype(jnp.float32), lam_init)
        f2 = lambda a: a.reshape(b * s, a.shape[-1])
        x2d = _out_mlp_call(x2d, f2(o_a), f2(o_b), f2(o_c), w_out[l].astype(MXU_DTYPE), row2(norm_mlp[l]),
                            w_up[l].astype(MXU_DTYPE), w_down[l].astype(MXU_DTYPE), row2(norm_final), l == depth - 1)
    return x2d.reshape(b, s, d)
```

```python
import functools
import math

import jax
import jax.numpy as jnp
import numpy as np
from jax import lax
from jax.experimental import pallas as pl
from jax.experimental.pallas import tpu as pltpu

D_MODEL = 1024
H_MLA, MLA_NOPE, MLA_ROPE, MLA_V = 8, 64, 32, 64
Q_LORA, KV_LORA = 384, 256
H_DSA, DSA_DIM, H_IDX, D_IDX, TOPK_MAX = 4, 64, 8, 32, 256
H_DIFF, DIFF_QK = 4, 32
DIFF_V = 2 * DIFF_QK
D_FF = 4 * D_MODEL
N_BUCKETS, MAX_DISTANCE = 32, 128
ROPE_BASE = 10000.0
EPS = 1e-6
NEG = -1e30
LOG2E = math.log2(math.e)

_SPLITS = (Q_LORA, KV_LORA, MLA_ROPE, H_DSA * DSA_DIM, DSA_DIM, DSA_DIM, H_IDX * D_IDX, D_IDX, H_IDX,
           H_DIFF * 2 * DIFF_QK, H_DIFF * 2 * DIFF_QK, H_DIFF * DIFF_V)
_OFF = tuple(int(o) for o in np.concatenate([[0], np.cumsum(_SPLITS)]))

LANES = 128
SUBLANES = 8
MXU_DTYPE = jnp.bfloat16
VMEM_LIMIT = 56 * 1024 * 1024

C_CQ = 0
C_CKV = C_CQ + Q_LORA
C_KR = C_CKV + KV_LORA
C_KRR = C_KR + LANES
C_KB = C_KRR + LANES
C_KI = C_KB + LANES
C_KC = C_KI + H_IDX * D_IDX
N1 = C_KC + H_DIFF * 2 * DIFF_QK
R_QB = 0
R_QI = R_QB + H_DSA * LANES
R_QC = R_QI + H_IDX * D_IDX
R_VB = R_QC + H_DIFF * 2 * DIFF_QK
R_VC = R_VB + LANES
R_WI = R_VC + H_DIFF * LANES
WI_ROWS = 16
N1T = R_WI + LANES

ONES_ROW = 64

TQ = 256
MLA_HEADS_PER_STEP = 4
TM_MLP = 512
FF_CHUNK = 1024

_NT = (((1,), (1,)), ((), ()))


def _dot(a, b):
    return jnp.dot(a, b, preferred_element_type=jnp.float32)


def _dot_nt(a, b):
    return lax.dot_general(a, b, _NT, preferred_element_type=jnp.float32)


def _ones_rows(rows):
    r = lax.broadcasted_iota(jnp.int32, (rows, 1), 0)
    return jnp.where(r % LANES == ONES_ROW, 1.0, 0.0).astype(jnp.float32)


def _proj_kernel(x_ref, g_ref, w1_ref, w1t_ref, qn_ref, wuqt_ref, wuqrt_ref, kvn_ref, wuk_ref, wuvt_ref,
                 c_ref, s_ref, ct_ref, st_ref,
                 qmt_ref, km_ref, vmt_ref, qdt_ref, kd_ref, vdt_ref, qit_ref, ki_ref, wit_ref, qct_ref, kc_ref, vct_ref):
    x = x_ref[...]
    h = x * lax.rsqrt(jnp.mean(x * x, axis=-1, keepdims=True) + EPS)
    h = (h * g_ref[...]).astype(MXU_DTYPE)

    def seg(a, b):
        return _dot(h, w1_ref[:, a:b])

    def seg_t(a, b):
        return _dot_nt(w1t_ref[a:b, :], h)

    c_q = seg(C_CQ, C_CKV)
    c_q = c_q * lax.rsqrt(jnp.mean(c_q * c_q, axis=-1, keepdims=True) + EPS)
    c_q = (c_q * qn_ref[...]).astype(MXU_DTYPE)
    qa = _dot_nt(wuqt_ref[...], c_q)
    qb = _dot_nt(wuqrt_ref[...], c_q)
    cos_t = ct_ref[0]
    sin_t = st_ref[0]
    for hh in range(H_MLA):
        sl = slice(hh * LANES, (hh + 1) * LANES)
        qmt_ref[0, sl, :] = (qa[sl, :] * cos_t + qb[sl, :] * sin_t).astype(qmt_ref.dtype)

    c_kv = seg(C_CKV, C_KR)
    c_kv = c_kv * lax.rsqrt(jnp.mean(c_kv * c_kv, axis=-1, keepdims=True) + EPS)
    c_kv = (c_kv * kvn_ref[...]).astype(MXU_DTYPE)
    k_rope = seg(C_KR, C_KRR) * c_ref[...] + seg(C_KRR, C_KB) * s_ref[...]
    kk = _dot(c_kv, wuk_ref[...])
    for hh in range(H_MLA):
        sl = slice(hh * LANES, (hh + 1) * LANES)
        km_ref[:, sl] = (kk[:, sl] + k_rope).astype(km_ref.dtype)
    vmt_ref[0] = (_dot_nt(wuvt_ref[...], c_kv) + _ones_rows(H_MLA * LANES)).astype(vmt_ref.dtype)

    kd_ref[...] = seg(C_KB, C_KI).astype(kd_ref.dtype)
    ki_ref[...] = seg(C_KI, C_KC).astype(ki_ref.dtype)
    kc_ref[...] = seg(C_KC, N1).astype(kc_ref.dtype)
    qdt_ref[0] = seg_t(R_QB, R_QI).astype(qdt_ref.dtype)
    qit_ref[0] = seg_t(R_QI, R_QC).astype(qit_ref.dtype)
    qct_ref[0] = seg_t(R_QC, R_VB).astype(qct_ref.dtype)
    vdt_ref[0] = (seg_t(R_VB, R_VC) + _ones_rows(LANES)).astype(vdt_ref.dtype)
    vct_ref[0] = (seg_t(R_VC, R_WI) + _ones_rows(H_DIFF * LANES)).astype(vct_ref.dtype)
    wit_ref[0] = seg_t(R_WI, R_WI + WI_ROWS)[:H_IDX, :]


def _proj_call(x2d, g, w1, w1t, qn, wuqt, wuqrt, kvn, wuk, wuvt, cos_r, sin_r, cos_c, sin_c):
    t = x2d.shape[0]
    tm = TQ
    assert t % tm == 0
    n = t // tm
    row = lambda w: pl.BlockSpec((tm, w), lambda i: (i, 0))
    colmajor = lambda r: pl.BlockSpec((1, r, tm), lambda i: (i, 0, 0))
    full = lambda a: pl.BlockSpec(a.shape, lambda i: (0, 0))
    bf = MXU_DTYPE
    width_c = 2 * H_DIFF * DIFF_QK
    outs = [
        (colmajor(H_MLA * LANES), (n, H_MLA * LANES, tm), bf),
        (row(H_MLA * LANES), (t, H_MLA * LANES), bf),
        (colmajor(H_MLA * LANES), (n, H_MLA * LANES, tm), bf),
        (colmajor(H_DSA * LANES), (n, H_DSA * LANES, tm), bf),
        (row(LANES), (t, LANES), bf),
        (colmajor(LANES), (n, LANES, tm), bf),
        (colmajor(H_IDX * D_IDX), (n, H_IDX * D_IDX, tm), bf),
        (row(H_IDX * D_IDX), (t, H_IDX * D_IDX), bf),
        (colmajor(H_IDX), (n, H_IDX, tm), jnp.float32),
        (colmajor(width_c), (n, width_c, tm), bf),
        (row(width_c), (t, width_c), bf),
        (colmajor(H_DIFF * LANES), (n, H_DIFF * LANES, tm), bf),
    ]
    return pl.pallas_call(
        _proj_kernel,
        grid=(n,),
        in_specs=[row(D_MODEL), full(g), full(w1), full(w1t), full(qn), full(wuqt), full(wuqrt), full(kvn), full(wuk),
                  full(wuvt), row(LANES), row(LANES), colmajor(LANES), colmajor(LANES)],
        out_specs=[o[0] for o in outs],
        out_shape=[jax.ShapeDtypeStruct(o[1], o[2]) for o in outs],
        compiler_params=pltpu.CompilerParams(dimension_semantics=("arbitrary",), vmem_limit_bytes=VMEM_LIMIT),
        name="proj",
    )(x2d, g, w1, w1t, qn, wuqt, wuqrt, kvn, wuk, wuvt, cos_r, sin_r, cos_c, sin_c)


def _flash_update(s, m_ref, acc_ref, v_t):
    m_prev = m_ref[...]
    m_new = jnp.maximum(m_prev, jnp.max(s, axis=0, keepdims=True))
    alpha = jnp.exp2(m_prev - m_new)
    p = jnp.exp2(s - m_new).astype(MXU_DTYPE)
    acc_ref[...] = alpha * acc_ref[...] + _dot(v_t, p)
    m_ref[...] = m_new


def _normalized(acc, width):
    return acc[:width, :] / acc[ONES_ROW:ONES_ROW + 1, :]


def _causal_tile(tq):
    key = lax.broadcasted_iota(jnp.int32, (tq, tq), 0)
    qry = lax.broadcasted_iota(jnp.int32, (tq, tq), 1)
    return key <= qry


def _pipelined_key_loop(i, n_heads, produce, consume, prepare=None):
    prepare = prepare or (lambda c, kind: None)
    for hh in range(n_heads):
        produce(0, hh)

    def both(c, kind, nxt):
        ctx = prepare(c, kind)
        for hh in range(n_heads):
            consume(c, hh, kind, ctx)
            if nxt:
                produce(c + 1, hh)

    def body(j, carry):
        both(j, "far", True)
        return carry

    lax.fori_loop(0, i - 1, body, 0)

    @pl.when(i >= 1)
    def _():
        both(i - 1, "prev", True)

    both(i, "diag", False)


def _mla_kernel(qt_ref, k_ref, vt_ref, o_ref, s_ref, m_ref, acc_ref):
    i = pl.program_id(2)
    tq = qt_ref.shape[-1]
    nh = MLA_HEADS_PER_STEP
    m_ref[...] = jnp.full(m_ref.shape, NEG, jnp.float32)
    acc_ref[...] = jnp.zeros(acc_ref.shape, jnp.float32)
    causal = _causal_tile(tq)

    def produce(c, hh):
        sl = slice(hh * LANES, (hh + 1) * LANES)
        kb = k_ref[0, pl.ds(pl.multiple_of(c * tq, tq), tq), sl]
        s_ref[c % 2, hh] = _dot(kb, qt_ref[0, 0, sl, :])

    def consume(c, hh, kind, ctx):
        s = s_ref[c % 2, hh]
        if kind == "diag":
            s = jnp.where(causal, s, NEG)
        _flash_update(s, m_ref.at[hh], acc_ref.at[hh], vt_ref[0, c, hh * LANES:(hh + 1) * LANES, :])

    _pipelined_key_loop(i, nh, produce, consume)
    o_ref[0] = jnp.concatenate([_normalized(acc_ref[hh], MLA_V).T for hh in range(nh)], axis=-1).astype(o_ref.dtype)


def _mla_call(qmt, km, vmt):
    b, nk, _, tq = qmt.shape
    s = nk * tq
    nh = MLA_HEADS_PER_STEP
    return pl.pallas_call(
        _mla_kernel,
        grid=(b, H_MLA // nh, nk),
        in_specs=[pl.BlockSpec((1, 1, nh * LANES, tq), lambda bb, hp, i: (bb, i, hp, 0)),
                  pl.BlockSpec((1, s, nh * LANES), lambda bb, hp, i: (bb, 0, hp)),
                  pl.BlockSpec((1, nk, nh * LANES, tq), lambda bb, hp, i: (bb, 0, hp, 0))],
        out_specs=pl.BlockSpec((1, tq, nh * MLA_V), lambda bb, hp, i: (bb, i, hp)),
        out_shape=jax.ShapeDtypeStruct((b, s, H_MLA * MLA_V), MXU_DTYPE),
        scratch_shapes=[pltpu.VMEM((2, nh, tq, tq), jnp.float32),
                        pltpu.VMEM((nh, 1, tq), jnp.float32), pltpu.VMEM((nh, LANES, tq), jnp.float32)],
        compiler_params=pltpu.CompilerParams(dimension_semantics=("arbitrary",) * 3, vmem_limit_bytes=VMEM_LIMIT),
        name="mla",
    )(qmt, km, vmt)


def _ordered_to_float(k):
    bits = k ^ ((k >> 31) & jnp.int32(0x7FFFFFFF))
    return lax.bitcast_convert_type(bits, jnp.float32)


def _dsa_kernel(n_sel, qdt_ref, qit_ref, wit_ref, ki_ref, kd_ref, vdt_ref, bd_ref, bp_ref, o_ref,
                sc_ref, qim_ref, thr_ref, s_ref, m_ref, acc_ref):
    i = pl.program_id(1)
    tq = qdt_ref.shape[-1]
    n_chunks = i + 1
    causal = _causal_tile(tq)
    int_min = jnp.int32(-2 ** 31)

    qi = qit_ref[0, 0]
    row_head = lax.broadcasted_iota(jnp.int32, qi.shape, 0) // D_IDX
    for hh in range(H_IDX):
        qim_ref[hh] = jnp.where(row_head == hh, qi, jnp.zeros_like(qi))

    def score_chunk(jc, diag):
        kc = ki_ref[0, pl.ds(pl.multiple_of(jc * tq, tq), tq), :]
        w = wit_ref[0, 0]
        acc = jnp.zeros((tq, tq), jnp.float32)
        for hh in range(H_IDX):
            acc = acc + jnp.maximum(_dot(kc, qim_ref[hh]), 0.0) * w[hh:hh + 1, :]
        if diag:
            acc = jnp.where(causal, acc, NEG)
        sc_ref[jc] = acc

    def score_body(jc, carry):
        score_chunk(jc, False)
        return carry

    lax.fori_loop(0, i, score_body, 0)
    score_chunk(i, True)

    @pl.when(n_chunks % 2 == 1)
    def _():
        sc_ref[n_chunks] = jnp.full((tq, tq), -jnp.inf, jnp.float32)

    n_pairs = (n_chunks + 1) // 2

    def count_where(hits):
        def fold(jc):
            return jnp.sum(hits(sc_ref[jc], jc).reshape(tq // SUBLANES, SUBLANES, tq), axis=0)

        def body(jp, acc):
            return acc + fold(2 * jp) + fold(2 * jp + 1)
        acc = lax.fori_loop(0, n_pairs, body, jnp.zeros((SUBLANES, tq), jnp.float32))
        return jnp.sum(acc, axis=0, keepdims=True)

    def count_ge(thr):
        return count_where(lambda sc, jc: jnp.where(sc >= thr, 1.0, 0.0))

    c0 = count_ge(jnp.zeros((1, tq), jnp.float32))
    start = (jnp.where(c0 >= n_sel, jnp.int32(0), int_min), jnp.where(c0 >= n_sel, c0, jnp.float32(2 * n_sel)))

    def bit_body(t, carry):
        k, c_k = carry
        cand = k + lax.shift_left(jnp.int32(1), jnp.int32(30) - t)
        c = count_ge(_ordered_to_float(cand))
        take = c >= n_sel
        return jnp.where(take, cand, k), jnp.where(take, c, c_k)

    kth, c_kth = lax.fori_loop(0, 31, bit_body, start)
    thr_ref[...] = _ordered_to_float(kth)

    @pl.when(jnp.max(c_kth) > n_sel)
    def _():
        thr_ = thr_ref[...]
        need = n_sel - count_where(lambda sc, jc: jnp.where(sc > thr_, 1.0, 0.0))

        def key_index(jc):
            return jc * tq + lax.broadcasted_iota(jnp.int32, (tq, tq), 0)

        def tie_before(bound):
            return count_where(lambda sc, jc: jnp.where(key_index(jc) < bound, jnp.where(sc == thr_, 1.0, 0.0), 0.0))

        def idx_body(t, jb):
            cand = jb + lax.shift_left(jnp.int32(1), jnp.int32(23) - t)
            return jnp.where(tie_before(cand) <= need, cand, jb)

        bound = lax.fori_loop(0, 24, idx_body, jnp.zeros((1, tq), jnp.int32))

        def drop_body(jc, carry):
            sc = sc_ref[jc]
            sc_ref[jc] = jnp.where(key_index(jc) >= bound, jnp.where(sc == thr_, -jnp.inf, sc), sc)
            return carry

        lax.fori_loop(0, n_chunks, drop_body, 0)

    thr = thr_ref[...]

    m_ref[...] = jnp.full(m_ref.shape, NEG, jnp.float32)
    acc_ref[...] = jnp.zeros(acc_ref.shape, jnp.float32)

    def produce(c, hh):
        kb = kd_ref[0, pl.ds(pl.multiple_of(c * tq, tq), tq), :]
        s_ref[c % 2, hh] = _dot(kb, qdt_ref[0, 0, hh * LANES:(hh + 1) * LANES, :])

    def prepare(c, kind):
        return sc_ref[c] >= thr

    def consume(c, hh, kind, sel):
        s = s_ref[c % 2, hh]
        if kind == "diag":
            s = jnp.where(causal, s + bd_ref[hh], NEG)
        elif kind == "prev":
            s = s + bp_ref[hh]
        s = jnp.where(sel, s, NEG)
        _flash_update(s, m_ref.at[hh], acc_ref.at[hh], vdt_ref[0, c])

    _pipelined_key_loop(i, H_DSA, produce, consume, prepare)
    o_ref[0] = jnp.concatenate([_normalized(acc_ref[hh], DSA_DIM).T for hh in range(H_DSA)], axis=-1).astype(o_ref.dtype)


def _dsa_call(qdt, qit, wit, ki, kd, vdt, bias_diag, bias_prev):
    b, nk, _, tq = qdt.shape
    s = nk * tq
    n_sel = min(TOPK_MAX, s // 4)
    assert tq >= n_sel and tq + 1 >= MAX_DISTANCE
    qblk = lambda r: pl.BlockSpec((1, 1, r, tq), lambda bb, i: (bb, i, 0, 0))
    kblk = lambda w: pl.BlockSpec((1, s, w), lambda bb, i: (bb, 0, 0))
    bblk = pl.BlockSpec((H_DSA, tq, tq), lambda bb, i: (0, 0, 0))
    return pl.pallas_call(
        functools.partial(_dsa_kernel, n_sel),
        grid=(b, nk),
        in_specs=[qblk(H_DSA * LANES), qblk(H_IDX * D_IDX), qblk(H_IDX),
                  kblk(H_IDX * D_IDX), kblk(LANES),
                  pl.BlockSpec((1, nk, LANES, tq), lambda bb, i: (bb, 0, 0, 0)), bblk, bblk],
        out_specs=pl.BlockSpec((1, tq, H_DSA * DSA_DIM), lambda bb, i: (bb, i, 0)),
        out_shape=jax.ShapeDtypeStruct((b, s, H_DSA * DSA_DIM), MXU_DTYPE),
        scratch_shapes=[pltpu.VMEM((nk + 1, tq, tq), jnp.float32),
                        pltpu.VMEM((H_IDX, H_IDX * D_IDX, tq), MXU_DTYPE),
                        pltpu.VMEM((1, tq), jnp.float32),
                        pltpu.VMEM((2, H_DSA, tq, tq), jnp.float32),
                        pltpu.VMEM((H_DSA, 1, tq), jnp.float32),
                        pltpu.VMEM((H_DSA, LANES, tq), jnp.float32)],
        compiler_params=pltpu.CompilerParams(dimension_semantics=("arbitrary",) * 2, vmem_limit_bytes=VMEM_LIMIT),
        name="dsa",
    )(qdt, qit, wit, ki, kd, vdt, bias_diag, bias_prev)


def _diff_kernel(lam_scale, qct_ref, kc_ref, vct_ref, bd_ref, bp_ref, lam_ref, g_ref, o_ref,
                 qm_ref, s_ref, m_ref, acc_ref):
    i = pl.program_id(1)
    tq = qct_ref.shape[-1]
    n_maps = 2 * H_DIFF
    causal = _causal_tile(tq)

    q = qct_ref[0, 0]
    row_map = lax.broadcasted_iota(jnp.int32, q.shape, 0) // DIFF_QK
    for mm in range(n_maps):
        qm_ref[mm] = jnp.where(row_map == mm, q, jnp.zeros_like(q))
    m_ref[...] = jnp.full(m_ref.shape, NEG, jnp.float32)
    acc_ref[...] = jnp.zeros(acc_ref.shape, jnp.float32)

    def produce(c, mm):
        kb = kc_ref[0, pl.ds(pl.multiple_of(c * tq, tq), tq), :]
        s_ref[c % 2, mm] = _dot(kb, qm_ref[mm])

    def consume(c, mm, kind, ctx):
        hh = mm // 2
        s = s_ref[c % 2, mm]
        if kind == "diag":
            s = jnp.where(causal, s + bd_ref[hh], NEG)
        elif kind == "prev":
            s = s + bp_ref[hh]
        _flash_update(s, m_ref.at[mm], acc_ref.at[mm], vct_ref[0, c, hh * LANES:(hh + 1) * LANES, :])

    _pipelined_key_loop(i, n_maps, produce, consume)

    lam = lam_ref[...]
    outs = []
    for hh in range(H_DIFF):
        o = _normalized(acc_ref[2 * hh], DIFF_V) - lam * _normalized(acc_ref[2 * hh + 1], DIFF_V)
        o = o * lax.rsqrt(jnp.mean(o * o, axis=0, keepdims=True) + EPS)
        outs.append((o * g_ref[...] * lam_scale).T)
    o_ref[0] = jnp.concatenate(outs, axis=-1).astype(o_ref.dtype)


def _diff_call(qct, kc, vct, bias_diag, bias_prev, lam, subln, lam_init):
    b, nk, width, tq = qct.shape
    s = nk * tq
    assert tq + 1 >= MAX_DISTANCE
    bblk = pl.BlockSpec((H_DIFF, tq, tq), lambda bb, i: (0, 0, 0))
    return pl.pallas_call(
        functools.partial(_diff_kernel, 1.0 - lam_init),
        grid=(b, nk),
        in_specs=[pl.BlockSpec((1, 1, width, tq), lambda bb, i: (bb, i, 0, 0)),
                  pl.BlockSpec((1, s, width), lambda bb, i: (bb, 0, 0)),
                  pl.BlockSpec((1, nk, H_DIFF * LANES, tq), lambda bb, i: (bb, 0, 0, 0)), bblk, bblk,
                  pl.BlockSpec((1, 1), lambda bb, i: (0, 0)), pl.BlockSpec((DIFF_V, 1), lambda bb, i: (0, 0))],
        out_specs=pl.BlockSpec((1, tq, H_DIFF * DIFF_V), lambda bb, i: (bb, i, 0)),
        out_shape=jax.ShapeDtypeStruct((b, s, H_DIFF * DIFF_V), MXU_DTYPE),
        scratch_shapes=[pltpu.VMEM((2 * H_DIFF, width, tq), MXU_DTYPE),
                        pltpu.VMEM((2, 2 * H_DIFF, tq, tq), jnp.float32),
                        pltpu.VMEM((2 * H_DIFF, 1, tq), jnp.float32),
                        pltpu.VMEM((2 * H_DIFF, LANES, tq), jnp.float32)],
        compiler_params=pltpu.CompilerParams(dimension_semantics=("arbitrary",) * 2, vmem_limit_bytes=VMEM_LIMIT),
        name="diff",
    )(qct, kc, vct, bias_diag, bias_prev, lam, subln)


def _weight_copies(wup_hbm, wdn_hbm, wup_ref, wdn_ref, sem):
    return (pltpu.make_async_copy(wup_hbm, wup_ref, sem.at[0]), pltpu.make_async_copy(wdn_hbm, wdn_ref, sem.at[1]))


def _out_mlp_kernel(final, x_ref, oa_ref, ob_ref, oc_ref, wo_ref, g_ref, wup_hbm, wdn_hbm, gf_ref, y_ref,
                    wup_ref, wdn_ref, sem):
    @pl.when(pl.program_id(0) == 0)
    def _():
        copies = _weight_copies(wup_hbm, wdn_hbm, wup_ref, wdn_ref, sem)
        for c in copies:
            c.start()
        for c in copies:
            c.wait()

    mix = jnp.concatenate([oa_ref[...], ob_ref[...], oc_ref[...]], axis=-1)
    x = x_ref[...] + _dot(mix, wo_ref[...])
    h = x * lax.rsqrt(jnp.mean(x * x, axis=-1, keepdims=True) + EPS)
    h = (h * g_ref[...]).astype(MXU_DTYPE)
    y = x
    for c in range(D_FF // FF_CHUNK):
        cs = slice(c * FF_CHUNK, (c + 1) * FF_CHUNK)
        u = jnp.maximum(_dot(h, wup_ref[:, cs]), 0.0)
        y = y + _dot((u * u).astype(MXU_DTYPE), wdn_ref[cs, :])
    if final:
        y = y * lax.rsqrt(jnp.mean(y * y, axis=-1, keepdims=True) + EPS) * gf_ref[...]
    y_ref[...] = y


def _out_mlp_call(x2d, oa, ob, oc, wo, g, wup, wdn, gf, final):
    t = x2d.shape[0]
    tm = TM_MLP
    assert t % tm == 0
    row = lambda a: pl.BlockSpec((tm, a.shape[1]), lambda i: (i, 0))
    full = lambda a: pl.BlockSpec(a.shape, lambda i: (0, 0))
    hbm = pl.BlockSpec(memory_space=pl.ANY)
    return pl.pallas_call(
        functools.partial(_out_mlp_kernel, final),
        grid=(t // tm,),
        in_specs=[row(x2d), row(oa), row(ob), row(oc), full(wo), full(g), hbm, hbm, full(gf)],
        out_specs=row(x2d),
        out_shape=jax.ShapeDtypeStruct(x2d.shape, jnp.float32),
        scratch_shapes=[pltpu.VMEM(wup.shape, wup.dtype), pltpu.VMEM(wdn.shape, wdn.dtype),
                        pltpu.SemaphoreType.DMA((2,))],
        compiler_params=pltpu.CompilerParams(dimension_semantics=("arbitrary",), vmem_limit_bytes=VMEM_LIMIT),
        name="out_mlp",
    )(x2d, oa, ob, oc, wo, g, wup, wdn, gf)


def _place(dst_cols, pieces):
    rows = pieces[0][1].shape[0]
    out = jnp.zeros((rows, dst_cols), jnp.float32)
    for off, a in pieces:
        out = lax.dynamic_update_slice(out, a.astype(jnp.float32), (0, off))
    return out


def _rot_half_cols(w):
    half = w.shape[1] // 2
    return jnp.concatenate([-w[:, half:], w[:, :half]], axis=1)


def _prep_in_proj(w_in):
    o = _OFF
    col = lambda k: w_in[:, o[k]:o[k + 1]]
    kr = col(2)
    s_dsa = (DSA_DIM ** -0.5) * LOG2E
    s_diff = (DIFF_QK ** -0.5) * LOG2E
    s_idx = (D_IDX ** -0.5) * (H_IDX ** -0.5)
    pieces = [(C_CQ, col(0)), (C_CKV, col(1)),
              (C_KR + MLA_NOPE, kr), (C_KRR + MLA_NOPE, _rot_half_cols(kr)),
              (C_KB, col(4)), (C_KC, col(10))]
    for hh in range(H_IDX):
        pieces.append((C_KI + hh * D_IDX, col(7)))
    t_pieces = [(R_QI, col(6)), (R_QC, col(9) * s_diff), (R_VB, col(5)), (R_WI, col(8) * s_idx)]
    qb, vc = col(3), col(11)
    for hh in range(H_DSA):
        t_pieces.append((R_QB + hh * LANES, qb[:, hh * DSA_DIM:(hh + 1) * DSA_DIM] * s_dsa))
    for hh in range(H_DIFF):
        t_pieces.append((R_VC + hh * LANES, vc[:, hh * DIFF_V:(hh + 1) * DIFF_V]))
    return _place(N1, pieces).astype(MXU_DTYPE), _place(N1T, t_pieces).T.astype(MXU_DTYPE)


def _prep_mla_up(w_uq, w_ukv):
    dq = MLA_NOPE + MLA_ROPE
    s_mla = (dq ** -0.5) * LOG2E
    q_p, qr_p, k_p, v_p = [], [], [], []
    for hh in range(H_MLA):
        wq = w_uq[:, hh * dq:(hh + 1) * dq] * s_mla
        q_p.append((hh * LANES, wq))
        qr_p.append((hh * LANES + MLA_NOPE, _rot_half_cols(wq[:, MLA_NOPE:])))
        wkv = w_ukv[:, hh * (MLA_NOPE + MLA_V):(hh + 1) * (MLA_NOPE + MLA_V)]
        k_p.append((hh * LANES, wkv[:, :MLA_NOPE]))
        v_p.append((hh * LANES, wkv[:, MLA_NOPE:]))
    width = H_MLA * LANES
    wuq, wuqr, wuk, wuv = (_place(width, p) for p in (q_p, qr_p, k_p, v_p))
    return wuq.T.astype(MXU_DTYPE), wuqr.T.astype(MXU_DTYPE), wuk.astype(MXU_DTYPE), wuv.T.astype(MXU_DTYPE)


def _rope_tables(positions, tm):
    half = MLA_ROPE // 2
    freqs = ROPE_BASE ** (-jnp.arange(half, dtype=jnp.float32) / half)
    ang = positions.astype(jnp.float32)[..., None] * freqs
    cos, sin = jnp.cos(ang), jnp.sin(ang)
    lead = cos.shape[:-1]
    ones = jnp.ones(lead + (MLA_NOPE,), jnp.float32)
    zeros = jnp.zeros(lead + (LANES - MLA_NOPE - MLA_ROPE,), jnp.float32)
    cos_r = jnp.concatenate([ones, cos, cos, zeros], axis=-1).reshape(-1, LANES)
    sin_r = jnp.concatenate([0.0 * ones, sin, sin, zeros], axis=-1).reshape(-1, LANES)
    to_cols = lambda a: jnp.transpose(a.reshape(-1, tm, LANES), (0, 2, 1))
    return cos_r, sin_r, to_cols(cos_r), to_cols(sin_r)


def _t5_bucket(rel):
    n = jnp.maximum(rel, 0)
    max_exact = N_BUCKETS // 2
    nf = jnp.maximum(n, 1).astype(jnp.float32)
    large = max_exact + (jnp.log(nf / max_exact) / math.log(MAX_DISTANCE / max_exact)
                         * (N_BUCKETS - max_exact)).astype(jnp.int32)
    large = jnp.minimum(large, N_BUCKETS - 1)
    return jnp.where(n < max_exact, n, large)


def _bias_tiles(rel_bias, tq):
    n_heads = rel_bias.shape[1]
    dist = jnp.arange(2 * tq, dtype=jnp.int32)
    b = ((rel_bias[_t5_bucket(dist)] - rel_bias[N_BUCKETS - 1]) * LOG2E).T

    def toeplitz(u):
        flat = jnp.tile(u, (1, tq))[:, :tq * (2 * tq - 1)]
        return flat.reshape(n_heads, tq, 2 * tq - 1)[:, :, :tq]

    diag = toeplitz(jnp.concatenate([b[:, :tq], jnp.zeros_like(b[:, :tq])], axis=1))
    prev = toeplitz(jnp.concatenate([b[:, tq:], b[:, :tq]], axis=1))
    return diag, prev


def kernel(x, positions, rel_bias, norm_attn, w_in, q_norm, w_uq, kv_norm, w_ukv, diff_lambda, diff_subln,
           w_out, norm_mlp, w_up, w_down, norm_final):
    b, s, d = x.shape
    depth = w_in.shape[0]
    assert s % TQ == 0
    nk = s // TQ
    cos_r, sin_r, cos_c, sin_c = _rope_tables(positions, TQ)
    bias_diag, bias_prev = _bias_tiles(rel_bias.astype(jnp.float32), TQ)
    row2 = lambda v: v.reshape(1, -1).astype(jnp.float32)
    x2d = x.reshape(b * s, d)
    for l in range(depth):
        w1, w1t = _prep_in_proj(w_in[l])
        wuqt, wuqrt, wuk, wuvt = _prep_mla_up(w_uq[l], w_ukv[l])
        qmt, km, vmt, qdt, kd, vdt, qit, ki, wit, qct, kc, vct = _proj_call(
            x2d, row2(norm_attn[l]), w1, w1t, row2(q_norm[l]), wuqt, wuqrt, row2(kv_norm[l]), wuk, wuvt,
            cos_r, sin_r, cos_c, sin_c)
        r3 = lambda a: a.reshape(b, s, a.shape[-1])
        r4 = lambda a: a.reshape(b, nk, a.shape[-2], a.shape[-1])
        o_a = _mla_call(r4(qmt), r3(km), r4(vmt))
        o_b = _dsa_call(r4(qdt), r4(qit), r4(wit), r3(ki), r3(kd), r4(vdt), bias_diag[:H_DSA], bias_prev[:H_DSA])
        lam_init = 0.8 - 0.6 * math.exp(-0.3 * l)
        lp = diff_lambda[l].astype(jnp.float32)
        lam = (jnp.exp(jnp.sum(lp[0] * lp[1])) - jnp.exp(jnp.sum(lp[2] * lp[3])) + lam_init).reshape(1, 1)
        o_c = _diff_call(r4(qct), r3(kc), r4(vct), bias_diag[H_DSA:], bias_prev[H_DSA:], lam,
                         diff_subln[l].reshape(-1, 1).astype(jnp.float32), lam_init)
        f2 = lambda a: a.reshape(b * s, a.shape[-1])
        x2d = _out_mlp_call(x2d, f2(o_a), f2(o_b), f2(o_c), w_out[l].astype(MXU_DTYPE), row2(norm_mlp[l]),
                            w_up[l].astype(MXU_DTYPE), w_down[l].astype(MXU_DTYPE), row2(norm_final), l == depth - 1)
    return x2d.reshape(b, s, d)
```

```python
import functools
import math

import jax
import jax.numpy as jnp
import numpy as np
from jax import lax
from jax.experimental import pallas as pl
from jax.experimental.pallas import tpu as pltpu

D_MODEL = 1024
H_MLA, MLA_NOPE, MLA_ROPE, MLA_V = 8, 64, 32, 64
Q_LORA, KV_LORA = 384, 256
H_DSA, DSA_DIM, H_IDX, D_IDX, TOPK_MAX = 4, 64, 8, 32, 256
H_DIFF, DIFF_QK = 4, 32
DIFF_V = 2 * DIFF_QK
D_FF = 4 * D_MODEL
N_BUCKETS, MAX_DISTANCE = 32, 128
ROPE_BASE = 10000.0
EPS = 1e-6
NEG = -1e30
LOG2E = math.log2(math.e)

_SPLITS = (Q_LORA, KV_LORA, MLA_ROPE, H_DSA * DSA_DIM, DSA_DIM, DSA_DIM, H_IDX * D_IDX, D_IDX, H_IDX,
           H_DIFF * 2 * DIFF_QK, H_DIFF * 2 * DIFF_QK, H_DIFF * DIFF_V)
_OFF = tuple(int(o) for o in np.concatenate([[0], np.cumsum(_SPLITS)]))

LANES = 128
SUBLANES = 8
MXU_DTYPE = jnp.bfloat16
VMEM_LIMIT = 56 * 1024 * 1024

C_CQ = 0
C_CKV = C_CQ + Q_LORA
C_KR = C_CKV + KV_LORA
C_KRR = C_KR + LANES
C_KB = C_KRR + LANES
C_KI = C_KB + LANES
C_KC = C_KI + H_IDX * D_IDX
N1 = C_KC + H_DIFF * 2 * DIFF_QK
R_QB = 0
R_QI = R_QB + H_DSA * LANES
R_QC = R_QI + H_IDX * D_IDX
R_VB = R_QC + H_DIFF * 2 * DIFF_QK
R_VC = R_VB + LANES
R_WI = R_VC + H_DIFF * LANES
WI_ROWS = 16
N1T = R_WI + LANES

ONES_ROW = 64

TQ = 256
MLA_HEADS_PER_STEP = 4
TM_MLP = 512
FF_CHUNK = 1024

_NT = (((1,), (1,)), ((), ()))


def _dot(a, b):
    return jnp.dot(a, b, preferred_element_type=jnp.float32)


def _dot_nt(a, b):
    return lax.dot_general(a, b, _NT, preferred_element_type=jnp.float32)


def _ones_rows(rows):
    r = lax.broadcasted_iota(jnp.int32, (rows, 1), 0)
    return jnp.where(r % LANES == ONES_ROW, 1.0, 0.0).astype(jnp.float32)


def _proj_kernel(x_ref, g_ref, w1_ref, w1t_ref, qn_ref, wuqt_ref, wuqrt_ref, kvn_ref, wuk_ref, wuvt_ref,
                 c_ref, s_ref, ct_ref, st_ref,
                 qmt_ref, km_ref, vmt_ref, qdt_ref, kd_ref, vdt_ref, qit_ref, ki_ref, wit_ref, qct_ref, kc_ref, vct_ref):
    x = x_ref[...]
    h = x * lax.rsqrt(jnp.mean(x * x, axis=-1, keepdims=True) + EPS)
    h = (h * g_ref[...]).astype(MXU_DTYPE)

    def seg(a, b):
        return _dot(h, w1_ref[:, a:b])

    def seg_t(a, b):
        return _dot_nt(w1t_ref[a:b, :], h)

    c_q = seg(C_CQ, C_CKV)
    c_q = c_q * lax.rsqrt(jnp.mean(c_q * c_q, axis=-1, keepdims=True) + EPS)
    c_q = (c_q * qn_ref[...]).astype(MXU_DTYPE)
    qa = _dot_nt(wuqt_ref[...], c_q)
    qb = _dot_nt(wuqrt_ref[...], c_q)
    cos_t = ct_ref[0]
    sin_t = st_ref[0]
    for hh in range(H_MLA):
        sl = slice(hh * LANES, (hh + 1) * LANES)
        qmt_ref[0, sl, :] = (qa[sl, :] * cos_t + qb[sl, :] * sin_t).astype(qmt_ref.dtype)

    c_kv = seg(C_CKV, C_KR)
    c_kv = c_kv * lax.rsqrt(jnp.mean(c_kv * c_kv, axis=-1, keepdims=True) + EPS)
    c_kv = (c_kv * kvn_ref[...]).astype(MXU_DTYPE)
    k_rope = seg(C_KR, C_KRR) * c_ref[...] + seg(C_KRR, C_KB) * s_ref[...]
    kk = _dot(c_kv, wuk_ref[...])
    for hh in range(H_MLA):
        sl = slice(hh * LANES, (hh + 1) * LANES)
        km_ref[:, sl] = (kk[:, sl] + k_rope).astype(km_ref.dtype)
    vmt_ref[0] = (_dot_nt(wuvt_ref[...], c_kv) + _ones_rows(H_MLA * LANES)).astype(vmt_ref.dtype)

    kd_ref[...] = seg(C_KB, C_KI).astype(kd_ref.dtype)
    ki_ref[...] = seg(C_KI, C_KC).astype(ki_ref.dtype)
    kc_ref[...] = seg(C_KC, N1).astype(kc_ref.dtype)
    qdt_ref[0] = seg_t(R_QB, R_QI).astype(qdt_ref.dtype)
    qit_ref[0] = seg_t(R_QI, R_QC).astype(qit_ref.dtype)
    qct_ref[0] = seg_t(R_QC, R_VB).astype(qct_ref.dtype)
    vdt_ref[0] = (seg_t(R_VB, R_VC) + _ones_rows(LANES)).astype(vdt_ref.dtype)
    vct_ref[0] = (seg_t(R_VC, R_WI) + _ones_rows(H_DIFF * LANES)).astype(vct_ref.dtype)
    wit_ref[0] = seg_t(R_WI, R_WI + WI_ROWS)[:H_IDX, :]


def _proj_call(x2d, g, w1, w1t, qn, wuqt, wuqrt, kvn, wuk, wuvt, cos_r, sin_r, cos_c, sin_c):
    t = x2d.shape[0]
    tm = TQ
    assert t % tm == 0
    n = t // tm
    row = lambda w: pl.BlockSpec((tm, w), lambda i: (i, 0))
    colmajor = lambda r: pl.BlockSpec((1, r, tm), lambda i: (i, 0, 0))
    full = lambda a: pl.BlockSpec(a.shape, lambda i: (0, 0))
    bf = MXU_DTYPE
    width_c = 2 * H_DIFF * DIFF_QK
    outs = [
        (colmajor(H_MLA * LANES), (n, H_MLA * LANES, tm), bf),
        (row(H_MLA * LANES), (t, H_MLA * LANES), bf),
        (colmajor(H_MLA * LANES), (n, H_MLA * LANES, tm), bf),
        (colmajor(H_DSA * LANES), (n, H_DSA * LANES, tm), bf),
        (row(LANES), (t, LANES), bf),
        (colmajor(LANES), (n, LANES, tm), bf),
        (colmajor(H_IDX * D_IDX), (n, H_IDX * D_IDX, tm), bf),
        (row(H_IDX * D_IDX), (t, H_IDX * D_IDX), bf),
        (colmajor(H_IDX), (n, H_IDX, tm), jnp.float32),
        (colmajor(width_c), (n, width_c, tm), bf),
        (row(width_c), (t, width_c), bf),
        (colmajor(H_DIFF * LANES), (n, H_DIFF * LANES, tm), bf),
    ]
    return pl.pallas_call(
        _proj_kernel,
        grid=(n,),
        in_specs=[row(D_MODEL), full(g), full(w1), full(w1t), full(qn), full(wuqt), full(wuqrt), full(kvn), full(wuk),
                  full(wuvt), row(LANES), row(LANES), colmajor(LANES), colmajor(LANES)],
        out_specs=[o[0] for o in outs],
        out_shape=[jax.ShapeDtypeStruct(o[1], o[2]) for o in outs],
        compiler_params=pltpu.CompilerParams(dimension_semantics=("arbitrary",), vmem_limit_bytes=VMEM_LIMIT),
        name="proj",
    )(x2d, g, w1, w1t, qn, wuqt, wuqrt, kvn, wuk, wuvt, cos_r, sin_r, cos_c, sin_c)


def _flash_update(s, m_ref, acc_ref, v_t):
    m_prev = m_ref[...]
    m_new = jnp.maximum(m_prev, jnp.max(s, axis=0, keepdims=True))
    alpha = jnp.exp2(m_prev - m_new)
    p = jnp.exp2(s - m_new).astype(MXU_DTYPE)
    acc_ref[...] = alpha * acc_ref[...] + _dot(v_t, p)
    m_ref[...] = m_new


def _normalized(acc, width):
    return acc[:width, :] / acc[ONES_ROW:ONES_ROW + 1, :]


def _causal_tile(tq):
    key = lax.broadcasted_iota(jnp.int32, (tq, tq), 0)
    qry = lax.broadcasted_iota(jnp.int32, (tq, tq), 1)
    return key <= qry


def _pipelined_key_loop(i, n_heads, produce, consume, prepare=None):
    prepare = prepare or (lambda c, kind: None)

    def step(c, slot, kind, nxt):
        ctx = prepare(c, kind)
        for hh in range(n_heads):
            consume(c, slot, hh, kind, ctx)
            if nxt:
                produce(c + 1, 1 - slot, hh)

    for hh in range(n_heads):
        produce(0, 0, hh)
    n_far_pairs = jnp.maximum(i - 1, 0) // 2

    def body(p, carry):
        step(2 * p, 0, "far", True)
        step(2 * p + 1, 1, "far", True)
        return carry

    lax.fori_loop(0, n_far_pairs, body, 0)
    c0 = 2 * n_far_pairs
    tail = i + 1 - c0

    @pl.when(tail == 3)
    def _():
        step(c0, 0, "far", True)
        step(c0 + 1, 1, "prev", True)
        step(c0 + 2, 0, "diag", False)

    @pl.when(tail == 2)
    def _():
        step(c0, 0, "prev", True)
        step(c0 + 1, 1, "diag", False)

    @pl.when(tail == 1)
    def _():
        step(c0, 0, "diag", False)


def _mla_kernel(qt_ref, k_ref, vt_ref, o_ref, s_ref, m_ref, acc_ref):
    i = pl.program_id(2)
    tq = qt_ref.shape[-1]
    nh = MLA_HEADS_PER_STEP
    m_ref[...] = jnp.full(m_ref.shape, NEG, jnp.float32)
    acc_ref[...] = jnp.zeros(acc_ref.shape, jnp.float32)
    causal = _causal_tile(tq)

    def produce(c, slot, hh):
        sl = slice(hh * LANES, (hh + 1) * LANES)
        kb = k_ref[0, pl.ds(pl.multiple_of(c * tq, tq), tq), sl]
        s_ref[slot, hh] = _dot(kb, qt_ref[0, 0, sl, :])

    def consume(c, slot, hh, kind, ctx):
        s = s_ref[slot, hh]
        if kind == "diag":
            s = jnp.where(causal, s, NEG)
        _flash_update(s, m_ref.at[hh], acc_ref.at[hh], vt_ref[0, c, hh * LANES:(hh + 1) * LANES, :])

    _pipelined_key_loop(i, nh, produce, consume)
    o_ref[0] = jnp.concatenate([_normalized(acc_ref[hh], MLA_V).T for hh in range(nh)], axis=-1).astype(o_ref.dtype)


def _mla_call(qmt, km, vmt):
    b, nk, _, tq = qmt.shape
    s = nk * tq
    nh = MLA_HEADS_PER_STEP
    return pl.pallas_call(
        _mla_kernel,
        grid=(b, H_MLA // nh, nk),
        in_specs=[pl.BlockSpec((1, 1, nh * LANES, tq), lambda bb, hp, i: (bb, i, hp, 0)),
                  pl.BlockSpec((1, s, nh * LANES), lambda bb, hp, i: (bb, 0, hp)),
                  pl.BlockSpec((1, nk, nh * LANES, tq), lambda bb, hp, i: (bb, 0, hp, 0))],
        out_specs=pl.BlockSpec((1, tq, nh * MLA_V), lambda bb, hp, i: (bb, i, hp)),
        out_shape=jax.ShapeDtypeStruct((b, s, H_MLA * MLA_V), MXU_DTYPE),
        scratch_shapes=[pltpu.VMEM((2, nh, tq, tq), jnp.float32),
                        pltpu.VMEM((nh, 1, tq), jnp.float32), pltpu.VMEM((nh, LANES, tq), jnp.float32)],
        compiler_params=pltpu.CompilerParams(dimension_semantics=("arbitrary",) * 3, vmem_limit_bytes=VMEM_LIMIT),
        name="mla",
    )(qmt, km, vmt)


def _ordered_to_float(k):
    bits = k ^ ((k >> 31) & jnp.int32(0x7FFFFFFF))
    return lax.bitcast_convert_type(bits, jnp.float32)


def _dsa_kernel(n_sel, qdt_ref, qit_ref, wit_ref, ki_ref, kd_ref, vdt_ref, bd_ref, bp_ref, o_ref,
                sc_ref, qim_ref, thr_ref, s_ref, m_ref, acc_ref):
    i = pl.program_id(1)
    tq = qdt_ref.shape[-1]
    n_chunks = i + 1
    causal = _causal_tile(tq)
    int_min = jnp.int32(-2 ** 31)

    qi = qit_ref[0, 0]
    row_head = lax.broadcasted_iota(jnp.int32, qi.shape, 0) // D_IDX
    for hh in range(H_IDX):
        qim_ref[hh] = jnp.where(row_head == hh, qi, jnp.zeros_like(qi))

    def score_chunk(jc, diag):
        kc = ki_ref[0, pl.ds(pl.multiple_of(jc * tq, tq), tq), :]
        w = wit_ref[0, 0]
        acc = jnp.zeros((tq, tq), jnp.float32)
        for hh in range(H_IDX):
            acc = acc + jnp.maximum(_dot(kc, qim_ref[hh]), 0.0) * w[hh:hh + 1, :]
        if diag:
            acc = jnp.where(causal, acc, NEG)
        sc_ref[jc] = acc

    def score_body(jc, carry):
        score_chunk(jc, False)
        return carry

    lax.fori_loop(0, i, score_body, 0)
    score_chunk(i, True)

    @pl.when(n_chunks % 2 == 1)
    def _():
        sc_ref[n_chunks] = jnp.full((tq, tq), -jnp.inf, jnp.float32)

    n_pairs = (n_chunks + 1) // 2

    def count_where(hits):
        def fold(jc):
            return jnp.sum(hits(sc_ref[jc], jc).reshape(tq // SUBLANES, SUBLANES, tq), axis=0)

        def body(jp, acc):
            return acc + fold(2 * jp) + fold(2 * jp + 1)
        acc = lax.fori_loop(0, n_pairs, body, jnp.zeros((SUBLANES, tq), jnp.float32))
        return jnp.sum(acc, axis=0, keepdims=True)

    def count_ge(thr):
        return count_where(lambda sc, jc: jnp.where(sc >= thr, 1.0, 0.0))

    c0 = count_ge(jnp.zeros((1, tq), jnp.float32))
    start = (jnp.where(c0 >= n_sel, jnp.int32(0), int_min), jnp.where(c0 >= n_sel, c0, jnp.float32(2 * n_sel)))

    def bit_body(t, carry):
        k, c_k = carry
        cand = k + lax.shift_left(jnp.int32(1), jnp.int32(30) - t)
        c = count_ge(_ordered_to_float(cand))
        take = c >= n_sel
        return jnp.where(take, cand, k), jnp.where(take, c, c_k)

    kth, c_kth = lax.fori_loop(0, 31, bit_body, start)
    thr_ref[...] = _ordered_to_float(kth)

    @pl.when(jnp.max(c_kth) > n_sel)
    def _():
        thr_ = thr_ref[...]
        need = n_sel - count_where(lambda sc, jc: jnp.where(sc > thr_, 1.0, 0.0))

        def key_index(jc):
            return jc * tq + lax.broadcasted_iota(jnp.int32, (tq, tq), 0)

        def tie_before(bound):
            return count_where(lambda sc, jc: jnp.where(key_index(jc) < bound, jnp.where(sc == thr_, 1.0, 0.0), 0.0))

        def idx_body(t, jb):
            cand = jb + lax.shift_left(jnp.int32(1), jnp.int32(23) - t)
            return jnp.where(tie_before(cand) <= need, cand, jb)

        bound = lax.fori_loop(0, 24, idx_body, jnp.zeros((1, tq), jnp.int32))

        def drop_body(jc, carry):
            sc = sc_ref[jc]
            sc_ref[jc] = jnp.where(key_index(jc) >= bound, jnp.where(sc == thr_, -jnp.inf, sc), sc)
            return carry

        lax.fori_loop(0, n_chunks, drop_body, 0)

    thr = thr_ref[...]

    m_ref[...] = jnp.full(m_ref.shape, NEG, jnp.float32)
    acc_ref[...] = jnp.zeros(acc_ref.shape, jnp.float32)

    def produce(c, slot, hh):
        kb = kd_ref[0, pl.ds(pl.multiple_of(c * tq, tq), tq), :]
        s_ref[slot, hh] = _dot(kb, qdt_ref[0, 0, hh * LANES:(hh + 1) * LANES, :])

    def prepare(c, kind):
        return sc_ref[c] >= thr

    def consume(c, slot, hh, kind, sel):
        s = s_ref[slot, hh]
        if kind == "diag":
            s = jnp.where(causal, s + bd_ref[hh], NEG)
        elif kind == "prev":
            s = s + bp_ref[hh]
        s = jnp.where(sel, s, NEG)
        _flash_update(s, m_ref.at[hh], acc_ref.at[hh], vdt_ref[0, c])

    _pipelined_key_loop(i, H_DSA, produce, consume, prepare)
    o_ref[0] = jnp.concatenate([_normalized(acc_ref[hh], DSA_DIM).T for hh in range(H_DSA)], axis=-1).astype(o_ref.dtype)


def _dsa_call(qdt, qit, wit, ki, kd, vdt, bias_diag, bias_prev):
    b, nk, _, tq = qdt.shape
    s = nk * tq
    n_sel = min(TOPK_MAX, s // 4)
    assert tq >= n_sel and tq + 1 >= MAX_DISTANCE
    qblk = lambda r: pl.BlockSpec((1, 1, r, tq), lambda bb, i: (bb, i, 0, 0))
    kblk = lambda w: pl.BlockSpec((1, s, w), lambda bb, i: (bb, 0, 0))
    bblk = pl.BlockSpec((H_DSA, tq, tq), lambda bb, i: (0, 0, 0))
    return pl.pallas_call(
        functools.partial(_dsa_kernel, n_sel),
        grid=(b, nk),
        in_specs=[qblk(H_DSA * LANES), qblk(H_IDX * D_IDX), qblk(H_IDX),
                  kblk(H_IDX * D_IDX), kblk(LANES),
                  pl.BlockSpec((1, nk, LANES, tq), lambda bb, i: (bb, 0, 0, 0)), bblk, bblk],
        out_specs=pl.BlockSpec((1, tq, H_DSA * DSA_DIM), lambda bb, i: (bb, i, 0)),
        out_shape=jax.ShapeDtypeStruct((b, s, H_DSA * DSA_DIM), MXU_DTYPE),
        scratch_shapes=[pltpu.VMEM((nk + 1, tq, tq), jnp.float32),
                        pltpu.VMEM((H_IDX, H_IDX * D_IDX, tq), MXU_DTYPE),
                        pltpu.VMEM((1, tq), jnp.float32),
                        pltpu.VMEM((2, H_DSA, tq, tq), jnp.float32),
                        pltpu.VMEM((H_DSA, 1, tq), jnp.float32),
                        pltpu.VMEM((H_DSA, LANES, tq), jnp.float32)],
        compiler_params=pltpu.CompilerParams(dimension_semantics=("arbitrary",) * 2, vmem_limit_bytes=VMEM_LIMIT),
        name="dsa",
    )(qdt, qit, wit, ki, kd, vdt, bias_diag, bias_prev)


def _diff_kernel(lam_scale, qct_ref, kc_ref, vct_ref, bd_ref, bp_ref, lam_ref, g_ref, o_ref,
                 qm_ref, s_ref, m_ref, acc_ref):
    i = pl.program_id(1)
    tq = qct_ref.shape[-1]
    n_maps = 2 * H_DIFF
    causal = _causal_tile(tq)

    q = qct_ref[0, 0]
    row_map = lax.broadcasted_iota(jnp.int32, q.shape, 0) // DIFF_QK
    for mm in range(n_maps):
        qm_ref[mm] = jnp.where(row_map == mm, q, jnp.zeros_like(q))
    m_ref[...] = jnp.full(m_ref.shape, NEG, jnp.float32)
    acc_ref[...] = jnp.zeros(acc_ref.shape, jnp.float32)

    def produce(c, slot, mm):
        kb = kc_ref[0, pl.ds(pl.multiple_of(c * tq, tq), tq), :]
        s_ref[slot, mm] = _dot(kb, qm_ref[mm])

    def consume(c, slot, mm, kind, ctx):
        hh = mm // 2
        s = s_ref[slot, mm]
        if kind == "diag":
            s = jnp.where(causal, s + bd_ref[hh], NEG)
        elif kind == "prev":
            s = s + bp_ref[hh]
        _flash_update(s, m_ref.at[mm], acc_ref.at[mm], vct_ref[0, c, hh * LANES:(hh + 1) * LANES, :])

    _pipelined_key_loop(i, n_maps, produce, consume)

    lam = lam_ref[...]
    outs = []
    for hh in range(H_DIFF):
        o = _normalized(acc_ref[2 * hh], DIFF_V) - lam * _normalized(acc_ref[2 * hh + 1], DIFF_V)
        o = o * lax.rsqrt(jnp.mean(o * o, axis=0, keepdims=True) + EPS)
        outs.append((o * g_ref[...] * lam_scale).T)
    o_ref[0] = jnp.concatenate(outs, axis=-1).astype(o_ref.dtype)


def _diff_call(qct, kc, vct, bias_diag, bias_prev, lam, subln, lam_init):
    b, nk, width, tq = qct.shape
    s = nk * tq
    assert tq + 1 >= MAX_DISTANCE
    bblk = pl.BlockSpec((H_DIFF, tq, tq), lambda bb, i: (0, 0, 0))
    return pl.pallas_call(
        functools.partial(_diff_kernel, 1.0 - lam_init),
        grid=(b, nk),
        in_specs=[pl.BlockSpec((1, 1, width, tq), lambda bb, i: (bb, i, 0, 0)),
                  pl.BlockSpec((1, s, width), lambda bb, i: (bb, 0, 0)),
                  pl.BlockSpec((1, nk, H_DIFF * LANES, tq), lambda bb, i: (bb, 0, 0, 0)), bblk, bblk,
                  pl.BlockSpec((1, 1), lambda bb, i: (0, 0)), pl.BlockSpec((DIFF_V, 1), lambda bb, i: (0, 0))],
        out_specs=pl.BlockSpec((1, tq, H_DIFF * DIFF_V), lambda bb, i: (bb, i, 0)),
        out_shape=jax.ShapeDtypeStruct((b, s, H_DIFF * DIFF_V), MXU_DTYPE),
        scratch_shapes=[pltpu.VMEM((2 * H_DIFF, width, tq), MXU_DTYPE),
                        pltpu.VMEM((2, 2 * H_DIFF, tq, tq), jnp.float32),
                        pltpu.VMEM((2 * H_DIFF, 1, tq), jnp.float32),
                        pltpu.VMEM((2 * H_DIFF, LANES, tq), jnp.float32)],
        compiler_params=pltpu.CompilerParams(dimension_semantics=("arbitrary",) * 2, vmem_limit_bytes=VMEM_LIMIT),
        name="diff",
    )(qct, kc, vct, bias_diag, bias_prev, lam, subln)


def _weight_copies(wup_hbm, wdn_hbm, wup_ref, wdn_ref, sem):
    return (pltpu.make_async_copy(wup_hbm, wup_ref, sem.at[0]), pltpu.make_async_copy(wdn_hbm, wdn_ref, sem.at[1]))


def _out_mlp_kernel(final, x_ref, oa_ref, ob_ref, oc_ref, wo_ref, g_ref, wup_hbm, wdn_hbm, gf_ref, y_ref,
                    wup_ref, wdn_ref, sem):
    @pl.when(pl.program_id(0) == 0)
    def _():
        copies = _weight_copies(wup_hbm, wdn_hbm, wup_ref, wdn_ref, sem)
        for c in copies:
            c.start()
        for c in copies:
            c.wait()

    mix = jnp.concatenate([oa_ref[...], ob_ref[...], oc_ref[...]], axis=-1)
    x = x_ref[...] + _dot(mix, wo_ref[...])
    h = x * lax.rsqrt(jnp.mean(x * x, axis=-1, keepdims=True) + EPS)
    h = (h * g_ref[...]).astype(MXU_DTYPE)
    y = x
    for c in range(D_FF // FF_CHUNK):
        cs = slice(c * FF_CHUNK, (c + 1) * FF_CHUNK)
        u = jnp.maximum(_dot(h, wup_ref[:, cs]), 0.0)
        y = y + _dot((u * u).astype(MXU_DTYPE), wdn_ref[cs, :])
    if final:
        y = y * lax.rsqrt(jnp.mean(y * y, axis=-1, keepdims=True) + EPS) * gf_ref[...]
    y_ref[...] = y


def _out_mlp_call(x2d, oa, ob, oc, wo, g, wup, wdn, gf, final):
    t = x2d.shape[0]
    tm = TM_MLP
    assert t % tm == 0
    row = lambda a: pl.BlockSpec((tm, a.shape[1]), lambda i: (i, 0))
    full = lambda a: pl.BlockSpec(a.shape, lambda i: (0, 0))
    hbm = pl.BlockSpec(memory_space=pl.ANY)
    return pl.pallas_call(
        functools.partial(_out_mlp_kernel, final),
        grid=(t // tm,),
        in_specs=[row(x2d), row(oa), row(ob), row(oc), full(wo), full(g), hbm, hbm, full(gf)],
        out_specs=row(x2d),
        out_shape=jax.ShapeDtypeStruct(x2d.shape, jnp.float32),
        scratch_shapes=[pltpu.VMEM(wup.shape, wup.dtype), pltpu.VMEM(wdn.shape, wdn.dtype),
                        pltpu.SemaphoreType.DMA((2,))],
        compiler_params=pltpu.CompilerParams(dimension_semantics=("arbitrary",), vmem_limit_bytes=VMEM_LIMIT),
        name="out_mlp",
    )(x2d, oa, ob, oc, wo, g, wup, wdn, gf)


def _place(dst_cols, pieces):
    rows = pieces[0][1].shape[0]
    out = jnp.zeros((rows, dst_cols), jnp.float32)
    for off, a in pieces:
        out = lax.dynamic_update_slice(out, a.astype(jnp.float32), (0, off))
    return out


def _rot_half_cols(w):
    half = w.shape[1] // 2
    return jnp.concatenate([-w[:, half:], w[:, :half]], axis=1)


def _prep_in_proj(w_in):
    o = _OFF
    col = lambda k: w_in[:, o[k]:o[k + 1]]
    kr = col(2)
    s_dsa = (DSA_DIM ** -0.5) * LOG2E
    s_diff = (DIFF_QK ** -0.5) * LOG2E
    s_idx = (D_IDX ** -0.5) * (H_IDX ** -0.5)
    pieces = [(C_CQ, col(0)), (C_CKV, col(1)),
              (C_KR + MLA_NOPE, kr), (C_KRR + MLA_NOPE, _rot_half_cols(kr)),
              (C_KB, col(4)), (C_KC, col(10))]
    for hh in range(H_IDX):
        pieces.append((C_KI + hh * D_IDX, col(7)))
    t_pieces = [(R_QI, col(6)), (R_QC, col(9) * s_diff), (R_VB, col(5)), (R_WI, col(8) * s_idx)]
    qb, vc = col(3), col(11)
    for hh in range(H_DSA):
        t_pieces.append((R_QB + hh * LANES, qb[:, hh * DSA_DIM:(hh + 1) * DSA_DIM] * s_dsa))
    for hh in range(H_DIFF):
        t_pieces.append((R_VC + hh * LANES, vc[:, hh * DIFF_V:(hh + 1) * DIFF_V]))
    return _place(N1, pieces).astype(MXU_DTYPE), _place(N1T, t_pieces).T.astype(MXU_DTYPE)


def _prep_mla_up(w_uq, w_ukv):
    dq = MLA_NOPE + MLA_ROPE
    s_mla = (dq ** -0.5) * LOG2E
    q_p, qr_p, k_p, v_p = [], [], [], []
    for hh in range(H_MLA):
        wq = w_uq[:, hh * dq:(hh + 1) * dq] * s_mla
        q_p.append((hh * LANES, wq))
        qr_p.append((hh * LANES + MLA_NOPE, _rot_half_cols(wq[:, MLA_NOPE:])))
        wkv = w_ukv[:, hh * (MLA_NOPE + MLA_V):(hh + 1) * (MLA_NOPE + MLA_V)]
        k_p.append((hh * LANES, wkv[:, :MLA_NOPE]))
        v_p.append((hh * LANES, wkv[:, MLA_NOPE:]))
    width = H_MLA * LANES
    wuq, wuqr, wuk, wuv = (_place(width, p) for p in (q_p, qr_p, k_p, v_p))
    return wuq.T.astype(MXU_DTYPE), wuqr.T.astype(MXU_DTYPE), wuk.astype(MXU_DTYPE), wuv.T.astype(MXU_DTYPE)


def _rope_tables(positions, tm):
    half = MLA_ROPE // 2
    freqs = ROPE_BASE ** (-jnp.arange(half, dtype=jnp.float32) / half)
    ang = positions.astype(jnp.float32)[..., None] * freqs
    cos, sin = jnp.cos(ang), jnp.sin(ang)
    lead = cos.shape[:-1]
    ones = jnp.ones(lead + (MLA_NOPE,), jnp.float32)
    zeros = jnp.zeros(lead + (LANES - MLA_NOPE - MLA_ROPE,), jnp.float32)
    cos_r = jnp.concatenate([ones, cos, cos, zeros], axis=-1).reshape(-1, LANES)
    sin_r = jnp.concatenate([0.0 * ones, sin, sin, zeros], axis=-1).reshape(-1, LANES)
    to_cols = lambda a: jnp.transpose(a.reshape(-1, tm, LANES), (0, 2, 1))
    return cos_r, sin_r, to_cols(cos_r), to_cols(sin_r)


def _t5_bucket(rel):
    n = jnp.maximum(rel, 0)
    max_exact = N_BUCKETS // 2
    nf = jnp.maximum(n, 1).astype(jnp.float32)
    large = max_exact + (jnp.log(nf / max_exact) / math.log(MAX_DISTANCE / max_exact)
                         * (N_BUCKETS - max_exact)).astype(jnp.int32)
    large = jnp.minimum(large, N_BUCKETS - 1)
    return jnp.where(n < max_exact, n, large)


def _bias_tiles(rel_bias, tq):
    n_heads = rel_bias.shape[1]
    dist = jnp.arange(2 * tq, dtype=jnp.int32)
    b = ((rel_bias[_t5_bucket(dist)] - rel_bias[N_BUCKETS - 1]) * LOG2E).T

    def toeplitz(u):
        flat = jnp.tile(u, (1, tq))[:, :tq * (2 * tq - 1)]
        return flat.reshape(n_heads, tq, 2 * tq - 1)[:, :, :tq]

    diag = toeplitz(jnp.concatenate([b[:, :tq], jnp.zeros_like(b[:, :tq])], axis=1))
    prev = toeplitz(jnp.concatenate([b[:, tq:], b[:, :tq]], axis=1))
    return diag, prev


def kernel(x, positions, rel_bias, norm_attn, w_in, q_norm, w_uq, kv_norm, w_ukv, diff_lambda, diff_subln,
           w_out, norm_mlp, w_up, w_down, norm_final):
    b, s, d = x.shape
    depth = w_in.shape[0]
    assert s % TQ == 0
    nk = s // TQ
    cos_r, sin_r, cos_c, sin_c = _rope_tables(positions, TQ)
    bias_diag, bias_prev = _bias_tiles(rel_bias.astype(jnp.float32), TQ)
    row2 = lambda v: v.reshape(1, -1).astype(jnp.float32)
    x2d = x.reshape(b * s, d)
    for l in range(depth):
        w1, w1t = _prep_in_proj(w_in[l])
        wuqt, wuqrt, wuk, wuvt = _prep_mla_up(w_uq[l], w_ukv[l])
        qmt, km, vmt, qdt, kd, vdt, qit, ki, wit, qct, kc, vct = _proj_call(
            x2d, row2(norm_attn[l]), w1, w1t, row2(q_norm[l]), wuqt, wuqrt, row2(kv_norm[l]), wuk, wuvt,
            cos_r, sin_r, cos_c, sin_c)
        r3 = lambda a: a.reshape(b, s, a.shape[-1])
        r4 = lambda a: a.reshape(b, nk, a.shape[-2], a.shape[-1])
        o_a = _mla_call(r4(qmt), r3(km), r4(vmt))
        o_b = _dsa_call(r4(qdt), r4(qit), r4(wit), r3(ki), r3(kd), r4(vdt), bias_diag[:H_DSA], bias_prev[:H_DSA])
        lam_init = 0.8 - 0.6 * math.exp(-0.3 * l)
        lp = diff_lambda[l].astype(jnp.float32)
        lam = (jnp.exp(jnp.sum(lp[0] * lp[1])) - jnp.exp(jnp.sum(lp[2] * lp[3])) + lam_init).reshape(1, 1)
        o_c = _diff_call(r4(qct), r3(kc), r4(vct), bias_diag[H_DSA:], bias_prev[H_DSA:], lam,
                         diff_subln[l].reshape(-1, 1).astype(jnp.float32), lam_init)
        f2 = lambda a: a.reshape(b * s, a.shape[-1])
        x2d = _out_mlp_call(x2d, f2(o_a), f2(o_b), f2(o_c), w_out[l].astype(MXU_DTYPE), row2(norm_mlp[l]),
                            w_up[l].astype(MXU_DTYPE), w_down[l].astype(MXU_DTYPE), row2(norm_final), l == depth - 1)
    return x2d.reshape(b, s, d)
```

```python
import functools
import math

import jax
import jax.numpy as jnp
import numpy as np
from jax import lax
from jax.experimental import pallas as pl
from jax.experimental.pallas import tpu as pltpu

D_MODEL = 1024
H_MLA, MLA_NOPE, MLA_ROPE, MLA_V = 8, 64, 32, 64
Q_LORA, KV_LORA = 384, 256
H_DSA, DSA_DIM, H_IDX, D_IDX, TOPK_MAX = 4, 64, 8, 32, 256
H_DIFF, DIFF_QK = 4, 32
DIFF_V = 2 * DIFF_QK
D_FF = 4 * D_MODEL
N_BUCKETS, MAX_DISTANCE = 32, 128
ROPE_BASE = 10000.0
EPS = 1e-6
NEG = -1e30
LOG2E = math.log2(math.e)

_SPLITS = (Q_LORA, KV_LORA, MLA_ROPE, H_DSA * DSA_DIM, DSA_DIM, DSA_DIM, H_IDX * D_IDX, D_IDX, H_IDX,
           H_DIFF * 2 * DIFF_QK, H_DIFF * 2 * DIFF_QK, H_DIFF * DIFF_V)
_OFF = tuple(int(o) for o in np.concatenate([[0], np.cumsum(_SPLITS)]))

LANES = 128
SUBLANES = 8
MXU_DTYPE = jnp.bfloat16
VMEM_LIMIT = 56 * 1024 * 1024

C_CQ = 0
C_CKV = C_CQ + Q_LORA
C_KR = C_CKV + KV_LORA
C_KRR = C_KR + LANES
C_KB = C_KRR + LANES
C_KI = C_KB + LANES
C_KC = C_KI + H_IDX * D_IDX
N1 = C_KC + H_DIFF * 2 * DIFF_QK
R_QB = 0
R_QI = R_QB + H_DSA * LANES
R_QC = R_QI + H_IDX * D_IDX
R_VB = R_QC + H_DIFF * 2 * DIFF_QK
V_ROWS = 80
R_VC = R_VB + V_ROWS
R_WI = R_VC + H_DIFF * V_ROWS
WI_ROWS = 16
N1T = R_WI + WI_ROWS

ONES_ROW = 64

TQ = 256
MLA_HEADS_PER_STEP = 8
TM_MLP = 512
FF_CHUNK = 1024

_NT = (((1,), (1,)), ((), ()))


def _dot(a, b):
    return jnp.dot(a, b, preferred_element_type=jnp.float32)


def _dot_nt(a, b):
    return lax.dot_general(a, b, _NT, preferred_element_type=jnp.float32)


def _ones_rows(rows):
    r = lax.broadcasted_iota(jnp.int32, (rows, 1), 0)
    return jnp.where(r % V_ROWS == ONES_ROW, 1.0, 0.0).astype(jnp.float32)


def _proj_kernel(x_ref, g_ref, w1_ref, w1t_ref, qn_ref, wuqt_ref, wuqrt_ref, kvn_ref, wuk_ref, wuvt_ref,
                 c_ref, s_ref, ct_ref, st_ref,
                 qmt_ref, km_ref, vmt_ref, qdt_ref, kd_ref, vdt_ref, qit_ref, ki_ref, wit_ref, qct_ref, kc_ref, vct_ref):
    x = x_ref[...]
    h = x * lax.rsqrt(jnp.mean(x * x, axis=-1, keepdims=True) + EPS)
    h = (h * g_ref[...]).astype(MXU_DTYPE)

    def seg(a, b):
        return _dot(h, w1_ref[:, a:b])

    def seg_t(a, b):
        return _dot_nt(w1t_ref[a:b, :], h)

    c_q = seg(C_CQ, C_CKV)
    c_q = c_q * lax.rsqrt(jnp.mean(c_q * c_q, axis=-1, keepdims=True) + EPS)
    c_q = (c_q * qn_ref[...]).astype(MXU_DTYPE)
    qa = _dot_nt(wuqt_ref[...], c_q)
    qb = _dot_nt(wuqrt_ref[...], c_q)
    cos_t = ct_ref[0]
    sin_t = st_ref[0]
    for hh in range(H_MLA):
        sl = slice(hh * LANES, (hh + 1) * LANES)
        qmt_ref[0, sl, :] = (qa[sl, :] * cos_t + qb[sl, :] * sin_t).astype(qmt_ref.dtype)

    c_kv = seg(C_CKV, C_KR)
    c_kv = c_kv * lax.rsqrt(jnp.mean(c_kv * c_kv, axis=-1, keepdims=True) + EPS)
    c_kv = (c_kv * kvn_ref[...]).astype(MXU_DTYPE)
    k_rope = seg(C_KR, C_KRR) * c_ref[...] + seg(C_KRR, C_KB) * s_ref[...]
    kk = _dot(c_kv, wuk_ref[...])
    for hh in range(H_MLA):
        sl = slice(hh * LANES, (hh + 1) * LANES)
        km_ref[:, sl] = (kk[:, sl] + k_rope).astype(km_ref.dtype)
    vmt_ref[0] = (_dot_nt(wuvt_ref[...], c_kv) + _ones_rows(H_MLA * V_ROWS)).astype(vmt_ref.dtype)

    kd_ref[...] = seg(C_KB, C_KI).astype(kd_ref.dtype)
    ki_ref[...] = seg(C_KI, C_KC).astype(ki_ref.dtype)
    kc_ref[...] = seg(C_KC, N1).astype(kc_ref.dtype)
    qdt_ref[0] = seg_t(R_QB, R_QI).astype(qdt_ref.dtype)
    qit_ref[0] = seg_t(R_QI, R_QC).astype(qit_ref.dtype)
    qct_ref[0] = seg_t(R_QC, R_VB).astype(qct_ref.dtype)
    vdt_ref[0] = (seg_t(R_VB, R_VC) + _ones_rows(V_ROWS)).astype(vdt_ref.dtype)
    vct_ref[0] = (seg_t(R_VC, R_WI) + _ones_rows(H_DIFF * V_ROWS)).astype(vct_ref.dtype)
    wit_ref[0] = seg_t(R_WI, R_WI + WI_ROWS)[:H_IDX, :]


def _proj_call(x2d, g, w1, w1t, qn, wuqt, wuqrt, kvn, wuk, wuvt, cos_r, sin_r, cos_c, sin_c):
    t = x2d.shape[0]
    tm = TQ
    assert t % tm == 0
    n = t // tm
    row = lambda w: pl.BlockSpec((tm, w), lambda i: (i, 0))
    colmajor = lambda r: pl.BlockSpec((1, r, tm), lambda i: (i, 0, 0))
    full = lambda a: pl.BlockSpec(a.shape, lambda i: (0, 0))
    bf = MXU_DTYPE
    width_c = 2 * H_DIFF * DIFF_QK
    outs = [
        (colmajor(H_MLA * LANES), (n, H_MLA * LANES, tm), bf),
        (row(H_MLA * LANES), (t, H_MLA * LANES), bf),
        (colmajor(H_MLA * V_ROWS), (n, H_MLA * V_ROWS, tm), bf),
        (colmajor(H_DSA * LANES), (n, H_DSA * LANES, tm), bf),
        (row(LANES), (t, LANES), bf),
        (colmajor(V_ROWS), (n, V_ROWS, tm), bf),
        (colmajor(H_IDX * D_IDX), (n, H_IDX * D_IDX, tm), bf),
        (row(H_IDX * D_IDX), (t, H_IDX * D_IDX), bf),
        (colmajor(H_IDX), (n, H_IDX, tm), jnp.float32),
        (colmajor(width_c), (n, width_c, tm), bf),
        (row(width_c), (t, width_c), bf),
        (colmajor(H_DIFF * V_ROWS), (n, H_DIFF * V_ROWS, tm), bf),
    ]
    return pl.pallas_call(
        _proj_kernel,
        grid=(n,),
        in_specs=[row(D_MODEL), full(g), full(w1), full(w1t), full(qn), full(wuqt), full(wuqrt), full(kvn), full(wuk),
                  full(wuvt), row(LANES), row(LANES), colmajor(LANES), colmajor(LANES)],
        out_specs=[o[0] for o in outs],
        out_shape=[jax.ShapeDtypeStruct(o[1], o[2]) for o in outs],
        compiler_params=pltpu.CompilerParams(dimension_semantics=("arbitrary",), vmem_limit_bytes=VMEM_LIMIT),
        name="proj",
    )(x2d, g, w1, w1t, qn, wuqt, wuqrt, kvn, wuk, wuvt, cos_r, sin_r, cos_c, sin_c)


def _flash_update(s, m_ref, acc_ref, v_t):
    m_prev = m_ref[...]
    m_new = jnp.maximum(m_prev, jnp.max(s, axis=0, keepdims=True))
    alpha = jnp.exp2(m_prev - m_new)
    p = jnp.exp2(s - m_new).astype(MXU_DTYPE)
    acc_ref[...] = alpha * acc_ref[...] + _dot(v_t, p)
    m_ref[...] = m_new


def _normalized(acc, width):
    return acc[:width, :] / acc[ONES_ROW:ONES_ROW + 1, :]


def _causal_tile(tq):
    key = lax.broadcasted_iota(jnp.int32, (tq, tq), 0)
    qry = lax.broadcasted_iota(jnp.int32, (tq, tq), 1)
    return key <= qry


def _pipelined_key_loop(i, n_heads, produce, consume, prepare=None):
    prepare = prepare or (lambda c, kind: None)

    def step(c, slot, kind, nxt):
        ctx = prepare(c, kind)
        for hh in range(n_heads):
            consume(c, slot, hh, kind, ctx)
            if nxt:
                produce(c + 1, 1 - slot, hh)

    for hh in range(n_heads):
        produce(0, 0, hh)
    n_far_pairs = jnp.maximum(i - 1, 0) // 2

    def body(p, carry):
        step(2 * p, 0, "far", True)
        step(2 * p + 1, 1, "far", True)
        return carry

    lax.fori_loop(0, n_far_pairs, body, 0)
    c0 = 2 * n_far_pairs
    tail = i + 1 - c0

    @pl.when(tail == 3)
    def _():
        step(c0, 0, "far", True)
        step(c0 + 1, 1, "prev", True)
        step(c0 + 2, 0, "diag", False)

    @pl.when(tail == 2)
    def _():
        step(c0, 0, "prev", True)
        step(c0 + 1, 1, "diag", False)

    @pl.when(tail == 1)
    def _():
        step(c0, 0, "diag", False)


def _mla_kernel(qt_ref, k_ref, vt_ref, o_ref, s_ref, m_ref, acc_ref):
    i = pl.program_id(2)
    tq = qt_ref.shape[-1]
    nh = MLA_HEADS_PER_STEP
    m_ref[...] = jnp.full(m_ref.shape, NEG, jnp.float32)
    acc_ref[...] = jnp.zeros(acc_ref.shape, jnp.float32)
    causal = _causal_tile(tq)

    def produce(c, slot, hh):
        sl = slice(hh * LANES, (hh + 1) * LANES)
        kb = k_ref[0, pl.ds(pl.multiple_of(c * tq, tq), tq), sl]
        s_ref[slot, hh] = _dot(kb, qt_ref[0, 0, sl, :])

    def consume(c, slot, hh, kind, ctx):
        s = s_ref[slot, hh]
        if kind == "diag":
            s = jnp.where(causal, s, NEG)
        _flash_update(s, m_ref.at[hh], acc_ref.at[hh], vt_ref[0, c, hh * V_ROWS:(hh + 1) * V_ROWS, :])

    _pipelined_key_loop(i, nh, produce, consume)
    o_ref[0] = jnp.concatenate([_normalized(acc_ref[hh], MLA_V).T for hh in range(nh)], axis=-1).astype(o_ref.dtype)


def _mla_call(qmt, km, vmt):
    b, nk, _, tq = qmt.shape
    s = nk * tq
    nh = MLA_HEADS_PER_STEP
    return pl.pallas_call(
        _mla_kernel,
        grid=(b, H_MLA // nh, nk),
        in_specs=[pl.BlockSpec((1, 1, nh * LANES, tq), lambda bb, hp, i: (bb, i, hp, 0)),
                  pl.BlockSpec((1, s, nh * LANES), lambda bb, hp, i: (bb, 0, hp)),
                  pl.BlockSpec((1, nk, nh * V_ROWS, tq), lambda bb, hp, i: (bb, 0, hp, 0))],
        out_specs=pl.BlockSpec((1, tq, nh * MLA_V), lambda bb, hp, i: (bb, i, hp)),
        out_shape=jax.ShapeDtypeStruct((b, s, H_MLA * MLA_V), MXU_DTYPE),
        scratch_shapes=[pltpu.VMEM((2, nh, tq, tq), jnp.float32),
                        pltpu.VMEM((nh, 1, tq), jnp.float32), pltpu.VMEM((nh, V_ROWS, tq), jnp.float32)],
        compiler_params=pltpu.CompilerParams(dimension_semantics=("arbitrary",) * 3, vmem_limit_bytes=VMEM_LIMIT),
        name="mla",
    )(qmt, km, vmt)


def _ordered_to_float(k):
    bits = k ^ ((k >> 31) & jnp.int32(0x7FFFFFFF))
    return lax.bitcast_convert_type(bits, jnp.float32)


def _dsa_kernel(n_sel, qdt_ref, qit_ref, wit_ref, ki_ref, kd_ref, vdt_ref, bd_ref, bp_ref, o_ref,
                sc_ref, qim_ref, thr_ref, s_ref, m_ref, acc_ref):
    i = pl.program_id(1)
    tq = qdt_ref.shape[-1]
    n_chunks = i + 1
    causal = _causal_tile(tq)
    int_min = jnp.int32(-2 ** 31)

    qi = qit_ref[0, 0]
    row_head = lax.broadcasted_iota(jnp.int32, qi.shape, 0) // D_IDX
    for hh in range(H_IDX):
        qim_ref[hh] = jnp.where(row_head == hh, qi, jnp.zeros_like(qi))

    def score_chunk(jc, diag):
        kc = ki_ref[0, pl.ds(pl.multiple_of(jc * tq, tq), tq), :]
        w = wit_ref[0, 0]
        acc = jnp.zeros((tq, tq), jnp.float32)
        for hh in range(H_IDX):
            acc = acc + jnp.maximum(_dot(kc, qim_ref[hh]), 0.0) * w[hh:hh + 1, :]
        if diag:
            acc = jnp.where(causal, acc, NEG)
        sc_ref[jc] = acc

    def score_body(jc, carry):
        score_chunk(jc, False)
        return carry

    lax.fori_loop(0, i, score_body, 0)
    score_chunk(i, True)

    @pl.when(n_chunks % 2 == 1)
    def _():
        sc_ref[n_chunks] = jnp.full((tq, tq), -jnp.inf, jnp.float32)

    n_pairs = (n_chunks + 1) // 2

    def count_where(hits):
        def fold(jc):
            return jnp.sum(hits(sc_ref[jc], jc).reshape(tq // SUBLANES, SUBLANES, tq), axis=0)

        def body(jp, acc):
            return acc + fold(2 * jp) + fold(2 * jp + 1)
        acc = lax.fori_loop(0, n_pairs, body, jnp.zeros((SUBLANES, tq), jnp.float32))
        return jnp.sum(acc, axis=0, keepdims=True)

    def count_ge(thr):
        return count_where(lambda sc, jc: jnp.where(sc >= thr, 1.0, 0.0))

    c0 = count_ge(jnp.zeros((1, tq), jnp.float32))
    start = (jnp.where(c0 >= n_sel, jnp.int32(0), int_min), jnp.where(c0 >= n_sel, c0, jnp.float32(2 * n_sel)))

    def bit_body(t, carry):
        k, c_k = carry
        cand = k + lax.shift_left(jnp.int32(1), jnp.int32(30) - t)
        c = count_ge(_ordered_to_float(cand))
        take = c >= n_sel
        return jnp.where(take, cand, k), jnp.where(take, c, c_k)

    kth, c_kth = lax.fori_loop(0, 31, bit_body, start)
    thr_ref[...] = _ordered_to_float(kth)

    @pl.when(jnp.max(c_kth) > n_sel)
    def _():
        thr_ = thr_ref[...]
        need = n_sel - count_where(lambda sc, jc: jnp.where(sc > thr_, 1.0, 0.0))
        key = lax.broadcasted_iota(jnp.int32, (tq, tq), 0)
        col = lax.broadcasted_iota(jnp.int32, (tq, tq), 1)
        tri = jnp.where(col <= key, 1.0, 0.0).astype(MXU_DTYPE)

        def drop_body(jc, before):
            sc = sc_ref[jc]
            tie = jnp.where(sc == thr_, 1.0, 0.0)
            rank = _dot(tri, tie.astype(MXU_DTYPE)) + before
            sc_ref[jc] = jnp.where(rank > need, jnp.where(sc == thr_, -jnp.inf, sc), sc)
            return rank[tq - 1:tq, :]

        lax.fori_loop(0, n_chunks, drop_body, jnp.zeros((1, tq), jnp.float32))

    thr = thr_ref[...]

    m_ref[...] = jnp.full(m_ref.shape, NEG, jnp.float32)
    acc_ref[...] = jnp.zeros(acc_ref.shape, jnp.float32)

    def produce(c, slot, hh):
        kb = kd_ref[0, pl.ds(pl.multiple_of(c * tq, tq), tq), :]
        s_ref[slot, hh] = _dot(kb, qdt_ref[0, 0, hh * LANES:(hh + 1) * LANES, :])

    def prepare(c, kind):
        return sc_ref[c] >= thr

    def consume(c, slot, hh, kind, sel):
        s = s_ref[slot, hh]
        if kind == "diag":
            s = jnp.where(causal, s + bd_ref[hh], NEG)
        elif kind == "prev":
            s = s + bp_ref[hh]
        s = jnp.where(sel, s, NEG)
        _flash_update(s, m_ref.at[hh], acc_ref.at[hh], vdt_ref[0, c])

    _pipelined_key_loop(i, H_DSA, produce, consume, prepare)
    o_ref[0] = jnp.concatenate([_normalized(acc_ref[hh], DSA_DIM).T for hh in range(H_DSA)], axis=-1).astype(o_ref.dtype)


def _dsa_call(qdt, qit, wit, ki, kd, vdt, bias_diag, bias_prev):
    b, nk, _, tq = qdt.shape
    s = nk * tq
    n_sel = min(TOPK_MAX, s // 4)
    assert tq >= n_sel and tq + 1 >= MAX_DISTANCE
    qblk = lambda r: pl.BlockSpec((1, 1, r, tq), lambda bb, i: (bb, i, 0, 0))
    kblk = lambda w: pl.BlockSpec((1, s, w), lambda bb, i: (bb, 0, 0))
    bblk = pl.BlockSpec((H_DSA, tq, tq), lambda bb, i: (0, 0, 0))
    return pl.pallas_call(
        functools.partial(_dsa_kernel, n_sel),
        grid=(b, nk),
        in_specs=[qblk(H_DSA * LANES), qblk(H_IDX * D_IDX), qblk(H_IDX),
                  kblk(H_IDX * D_IDX), kblk(LANES),
                  pl.BlockSpec((1, nk, V_ROWS, tq), lambda bb, i: (bb, 0, 0, 0)), bblk, bblk],
        out_specs=pl.BlockSpec((1, tq, H_DSA * DSA_DIM), lambda bb, i: (bb, i, 0)),
        out_shape=jax.ShapeDtypeStruct((b, s, H_DSA * DSA_DIM), MXU_DTYPE),
        scratch_shapes=[pltpu.VMEM((nk + 1, tq, tq), jnp.float32),
                        pltpu.VMEM((H_IDX, H_IDX * D_IDX, tq), MXU_DTYPE),
                        pltpu.VMEM((1, tq), jnp.float32),
                        pltpu.VMEM((2, H_DSA, tq, tq), jnp.float32),
                        pltpu.VMEM((H_DSA, 1, tq), jnp.float32),
                        pltpu.VMEM((H_DSA, V_ROWS, tq), jnp.float32)],
        compiler_params=pltpu.CompilerParams(dimension_semantics=("arbitrary",) * 2, vmem_limit_bytes=VMEM_LIMIT),
        name="dsa",
    )(qdt, qit, wit, ki, kd, vdt, bias_diag, bias_prev)


def _diff_kernel(lam_scale, qct_ref, kc_ref, vct_ref, bd_ref, bp_ref, lam_ref, g_ref, o_ref,
                 qm_ref, s_ref, m_ref, acc_ref):
    i = pl.program_id(1)
    tq = qct_ref.shape[-1]
    n_maps = 2 * H_DIFF
    causal = _causal_tile(tq)

    q = qct_ref[0, 0]
    row_map = lax.broadcasted_iota(jnp.int32, q.shape, 0) // DIFF_QK
    for mm in range(n_maps):
        qm_ref[mm] = jnp.where(row_map == mm, q, jnp.zeros_like(q))
    m_ref[...] = jnp.full(m_ref.shape, NEG, jnp.float32)
    acc_ref[...] = jnp.zeros(acc_ref.shape, jnp.float32)

    def produce(c, slot, mm):
        kb = kc_ref[0, pl.ds(pl.multiple_of(c * tq, tq), tq), :]
        s_ref[slot, mm] = _dot(kb, qm_ref[mm])

    def consume(c, slot, mm, kind, ctx):
        hh = mm // 2
        s = s_ref[slot, mm]
        if kind == "diag":
            s = jnp.where(causal, s + bd_ref[hh], NEG)
        elif kind == "prev":
            s = s + bp_ref[hh]
        _flash_update(s, m_ref.at[mm], acc_ref.at[mm], vct_ref[0, c, hh * V_ROWS:(hh + 1) * V_ROWS, :])

    _pipelined_key_loop(i, n_maps, produce, consume)

    lam = lam_ref[...]
    outs = []
    for hh in range(H_DIFF):
        o = _normalized(acc_ref[2 * hh], DIFF_V) - lam * _normalized(acc_ref[2 * hh + 1], DIFF_V)
        o = o * lax.rsqrt(jnp.mean(o * o, axis=0, keepdims=True) + EPS)
        outs.append((o * g_ref[...] * lam_scale).T)
    o_ref[0] = jnp.concatenate(outs, axis=-1).astype(o_ref.dtype)


def _diff_call(qct, kc, vct, bias_diag, bias_prev, lam, subln, lam_init):
    b, nk, width, tq = qct.shape
    s = nk * tq
    assert tq + 1 >= MAX_DISTANCE
    bblk = pl.BlockSpec((H_DIFF, tq, tq), lambda bb, i: (0, 0, 0))
    return pl.pallas_call(
        functools.partial(_diff_kernel, 1.0 - lam_init),
        grid=(b, nk),
        in_specs=[pl.BlockSpec((1, 1, width, tq), lambda bb, i: (bb, i, 0, 0)),
                  pl.BlockSpec((1, s, width), lambda bb, i: (bb, 0, 0)),
                  pl.BlockSpec((1, nk, H_DIFF * V_ROWS, tq), lambda bb, i: (bb, 0, 0, 0)), bblk, bblk,
                  pl.BlockSpec((1, 1), lambda bb, i: (0, 0)), pl.BlockSpec((DIFF_V, 1), lambda bb, i: (0, 0))],
        out_specs=pl.BlockSpec((1, tq, H_DIFF * DIFF_V), lambda bb, i: (bb, i, 0)),
        out_shape=jax.ShapeDtypeStruct((b, s, H_DIFF * DIFF_V), MXU_DTYPE),
        scratch_shapes=[pltpu.VMEM((2 * H_DIFF, width, tq), MXU_DTYPE),
                        pltpu.VMEM((2, 2 * H_DIFF, tq, tq), jnp.float32),
                        pltpu.VMEM((2 * H_DIFF, 1, tq), jnp.float32),
                        pltpu.VMEM((2 * H_DIFF, V_ROWS, tq), jnp.float32)],
        compiler_params=pltpu.CompilerParams(dimension_semantics=("arbitrary",) * 2, vmem_limit_bytes=VMEM_LIMIT),
        name="diff",
    )(qct, kc, vct, bias_diag, bias_prev, lam, subln)


def _weight_copies(wup_hbm, wdn_hbm, wup_ref, wdn_ref, sem):
    return (pltpu.make_async_copy(wup_hbm, wup_ref, sem.at[0]), pltpu.make_async_copy(wdn_hbm, wdn_ref, sem.at[1]))


def _out_mlp_kernel(final, x_ref, oa_ref, ob_ref, oc_ref, wo_ref, g_ref, wup_hbm, wdn_hbm, gf_ref, y_ref,
                    wup_ref, wdn_ref, sem):
    @pl.when(pl.program_id(0) == 0)
    def _():
        copies = _weight_copies(wup_hbm, wdn_hbm, wup_ref, wdn_ref, sem)
        for c in copies:
            c.start()
        for c in copies:
            c.wait()

    mix = jnp.concatenate([oa_ref[...], ob_ref[...], oc_ref[...]], axis=-1)
    x = x_ref[...] + _dot(mix, wo_ref[...])
    h = x * lax.rsqrt(jnp.mean(x * x, axis=-1, keepdims=True) + EPS)
    h = (h * g_ref[...]).astype(MXU_DTYPE)
    y = x
    for c in range(D_FF // FF_CHUNK):
        cs = slice(c * FF_CHUNK, (c + 1) * FF_CHUNK)
        u = jnp.maximum(_dot(h, wup_ref[:, cs]), 0.0)
        y = y + _dot((u * u).astype(MXU_DTYPE), wdn_ref[cs, :])
    if final:
        y = y * lax.rsqrt(jnp.mean(y * y, axis=-1, keepdims=True) + EPS) * gf_ref[...]
    y_ref[...] = y


def _out_mlp_call(x2d, oa, ob, oc, wo, g, wup, wdn, gf, final):
    t = x2d.shape[0]
    tm = TM_MLP
    assert t % tm == 0
    row = lambda a: pl.BlockSpec((tm, a.shape[1]), lambda i: (i, 0))
    full = lambda a: pl.BlockSpec(a.shape, lambda i: (0, 0))
    hbm = pl.BlockSpec(memory_space=pl.ANY)
    return pl.pallas_call(
        functools.partial(_out_mlp_kernel, final),
        grid=(t // tm,),
        in_specs=[row(x2d), row(oa), row(ob), row(oc), full(wo), full(g), hbm, hbm, full(gf)],
        out_specs=row(x2d),
        out_shape=jax.ShapeDtypeStruct(x2d.shape, jnp.float32),
        scratch_shapes=[pltpu.VMEM(wup.shape, wup.dtype), pltpu.VMEM(wdn.shape, wdn.dtype),
                        pltpu.SemaphoreType.DMA((2,))],
        compiler_params=pltpu.CompilerParams(dimension_semantics=("arbitrary",), vmem_limit_bytes=VMEM_LIMIT),
        name="out_mlp",
    )(x2d, oa, ob, oc, wo, g, wup, wdn, gf)


def _place(dst_cols, pieces):
    rows = pieces[0][1].shape[0]
    out = jnp.zeros((rows, dst_cols), jnp.float32)
    for off, a in pieces:
        out = lax.dynamic_update_slice(out, a.astype(jnp.float32), (0, off))
    return out


def _rot_half_cols(w):
    half = w.shape[1] // 2
    return jnp.concatenate([-w[:, half:], w[:, :half]], axis=1)


def _prep_in_proj(w_in):
    o = _OFF
    col = lambda k: w_in[:, o[k]:o[k + 1]]
    kr = col(2)
    s_dsa = (DSA_DIM ** -0.5) * LOG2E
    s_diff = (DIFF_QK ** -0.5) * LOG2E
    s_idx = (D_IDX ** -0.5) * (H_IDX ** -0.5)
    pieces = [(C_CQ, col(0)), (C_CKV, col(1)),
              (C_KR + MLA_NOPE, kr), (C_KRR + MLA_NOPE, _rot_half_cols(kr)),
              (C_KB, col(4)), (C_KC, col(10))]
    for hh in range(H_IDX):
        pieces.append((C_KI + hh * D_IDX, col(7)))
    t_pieces = [(R_QI, col(6)), (R_QC, col(9) * s_diff), (R_VB, col(5)), (R_WI, col(8) * s_idx)]
    qb, vc = col(3), col(11)
    for hh in range(H_DSA):
        t_pieces.append((R_QB + hh * LANES, qb[:, hh * DSA_DIM:(hh + 1) * DSA_DIM] * s_dsa))
    for hh in range(H_DIFF):
        t_pieces.append((R_VC + hh * V_ROWS, vc[:, hh * DIFF_V:(hh + 1) * DIFF_V]))
    return _place(N1, pieces).astype(MXU_DTYPE), _place(N1T, t_pieces).T.astype(MXU_DTYPE)


def _prep_mla_up(w_uq, w_ukv):
    dq = MLA_NOPE + MLA_ROPE
    s_mla = (dq ** -0.5) * LOG2E
    q_p, qr_p, k_p, v_p = [], [], [], []
    for hh in range(H_MLA):
        wq = w_uq[:, hh * dq:(hh + 1) * dq] * s_mla
        q_p.append((hh * LANES, wq))
        qr_p.append((hh * LANES + MLA_NOPE, _rot_half_cols(wq[:, MLA_NOPE:])))
        wkv = w_ukv[:, hh * (MLA_NOPE + MLA_V):(hh + 1) * (MLA_NOPE + MLA_V)]
        k_p.append((hh * LANES, wkv[:, :MLA_NOPE]))
        v_p.append((hh * V_ROWS, wkv[:, MLA_NOPE:]))
    width = H_MLA * LANES
    wuq, wuqr, wuk = (_place(width, p) for p in (q_p, qr_p, k_p))
    wuv = _place(H_MLA * V_ROWS, v_p)
    return wuq.T.astype(MXU_DTYPE), wuqr.T.astype(MXU_DTYPE), wuk.astype(MXU_DTYPE), wuv.T.astype(MXU_DTYPE)


def _rope_tables(positions, tm):
    half = MLA_ROPE // 2
    freqs = ROPE_BASE ** (-jnp.arange(half, dtype=jnp.float32) / half)
    ang = positions.astype(jnp.float32)[..., None] * freqs
    cos, sin = jnp.cos(ang), jnp.sin(ang)
    lead = cos.shape[:-1]
    ones = jnp.ones(lead + (MLA_NOPE,), jnp.float32)
    zeros = jnp.zeros(lead + (LANES - MLA_NOPE - MLA_ROPE,), jnp.float32)
    cos_r = jnp.concatenate([ones, cos, cos, zeros], axis=-1).reshape(-1, LANES)
    sin_r = jnp.concatenate([0.0 * ones, sin, sin, zeros], axis=-1).reshape(-1, LANES)
    to_cols = lambda a: jnp.transpose(a.reshape(-1, tm, LANES), (0, 2, 1))
    return cos_r, sin_r, to_cols(cos_r), to_cols(sin_r)


def _t5_bucket(rel):
    n = jnp.maximum(rel, 0)
    max_exact = N_BUCKETS // 2
    nf = jnp.maximum(n, 1).astype(jnp.float32)
    large = max_exact + (jnp.log(nf / max_exact) / math.log(MAX_DISTANCE / max_exact)
                         * (N_BUCKETS - max_exact)).astype(jnp.int32)
    large = jnp.minimum(large, N_BUCKETS - 1)
    return jnp.where(n < max_exact, n, large)


def _bias_tiles(rel_bias, tq):
    n_heads = rel_bias.shape[1]
    dist = jnp.arange(2 * tq, dtype=jnp.int32)
    b = ((rel_bias[_t5_bucket(dist)] - rel_bias[N_BUCKETS - 1]) * LOG2E).T

    def toeplitz(u):
        flat = jnp.tile(u, (1, tq))[:, :tq * (2 * tq - 1)]
        return flat.reshape(n_heads, tq, 2 * tq - 1)[:, :, :tq]

    diag = toeplitz(jnp.concatenate([b[:, :tq], jnp.zeros_like(b[:, :tq])], axis=1))
    prev = toeplitz(jnp.concatenate([b[:, tq:], b[:, :tq]], axis=1))
    return diag, prev


def kernel(x, positions, rel_bias, norm_attn, w_in, q_norm, w_uq, kv_norm, w_ukv, diff_lambda, diff_subln,
           w_out, norm_mlp, w_up, w_down, norm_final):
    b, s, d = x.shape
    depth = w_in.shape[0]
    assert s % TQ == 0
    nk = s // TQ
    cos_r, sin_r, cos_c, sin_c = _rope_tables(positions, TQ)
    bias_diag, bias_prev = _bias_tiles(rel_bias.astype(jnp.float32), TQ)
    row2 = lambda v: v.reshape(1, -1).astype(jnp.float32)
    x2d = x.reshape(b * s, d)
    for l in range(depth):
        w1, w1t = _prep_in_proj(w_in[l])
        wuqt, wuqrt, wuk, wuvt = _prep_mla_up(w_uq[l], w_ukv[l])
        qmt, km, vmt, qdt, kd, vdt, qit, ki, wit, qct, kc, vct = _proj_call(
            x2d, row2(norm_attn[l]), w1, w1t, row2(q_norm[l]), wuqt, wuqrt, row2(kv_norm[l]), wuk, wuvt,
            cos_r, sin_r, cos_c, sin_c)
        r3 = lambda a: a.reshape(b, s, a.shape[-1])
        r4 = lambda a: a.reshape(b, nk, a.shape[-2], a.shape[-1])
        o_a = _mla_call(r4(qmt), r3(km), r4(vmt))
        o_b = _dsa_call(r4(qdt), r4(qit), r4(wit), r3(ki), r3(kd), r4(vdt), bias_diag[:H_DSA], bias_prev[:H_DSA])
        lam_init = 0.8 - 0.6 * math.exp(-0.3 * l)
        lp = diff_lambda[l].astype(jnp.float32)
        lam = (jnp.exp(jnp.sum(lp[0] * lp[1])) - jnp.exp(jnp.sum(lp[2] * lp[3])) + lam_init).reshape(1, 1)
        o_c = _diff_call(r4(qct), r3(kc), r4(vct), bias_diag[H_DSA:], bias_prev[H_DSA:], lam,
                         diff_subln[l].reshape(-1, 1).astype(jnp.float32), lam_init)
        f2 = lambda a: a.reshape(b * s, a.shape[-1])
        x2d = _out_mlp_call(x2d, f2(o_a), f2(o_b), f2(o_c), w_out[l].astype(MXU_DTYPE), row2(norm_mlp[l]),
                            w_up[l].astype(MXU_DTYPE), w_down[l].astype(MXU_DTYPE), row2(norm_final), l == depth - 1)
    return x2d.reshape(b, s, d)
```

```python
import functools
import math

import jax
import jax.numpy as jnp
import numpy as np
from jax import lax
from jax.experimental import pallas as pl
from jax.experimental.pallas import tpu as pltpu

D_MODEL = 1024
H_MLA, MLA_NOPE, MLA_ROPE, MLA_V = 8, 64, 32, 64
Q_LORA, KV_LORA = 384, 256
H_DSA, DSA_DIM, H_IDX, D_IDX, TOPK_MAX = 4, 64, 8, 32, 256
H_DIFF, DIFF_QK = 4, 32
DIFF_V = 2 * DIFF_QK
D_FF = 4 * D_MODEL
N_BUCKETS, MAX_DISTANCE = 32, 128
ROPE_BASE = 10000.0
EPS = 1e-6
NEG = -1e30
LOG2E = math.log2(math.e)

_SPLITS = (Q_LORA, KV_LORA, MLA_ROPE, H_DSA * DSA_DIM, DSA_DIM, DSA_DIM, H_IDX * D_IDX, D_IDX, H_IDX,
           H_DIFF * 2 * DIFF_QK, H_DIFF * 2 * DIFF_QK, H_DIFF * DIFF_V)
_OFF = tuple(int(o) for o in np.concatenate([[0], np.cumsum(_SPLITS)]))

LANES = 128
SUBLANES = 8
MXU_DTYPE = jnp.bfloat16
VMEM_LIMIT = 56 * 1024 * 1024

C_CQ = 0
C_CKV = C_CQ + Q_LORA
C_KR = C_CKV + KV_LORA
C_KRR = C_KR + LANES
C_KB = C_KRR + LANES
C_KI = C_KB + LANES
C_KC = C_KI + H_IDX * D_IDX
N1 = C_KC + H_DIFF * 2 * DIFF_QK
R_QB = 0
R_QI = R_QB + H_DSA * LANES
R_QC = R_QI + H_IDX * D_IDX
R_VB = R_QC + H_DIFF * 2 * DIFF_QK
V_ROWS = 80
R_VC = R_VB + V_ROWS
R_WI = R_VC + H_DIFF * V_ROWS
WI_ROWS = 16
N1T = R_WI + WI_ROWS

ONES_ROW = 64

TQ = 256
TM_PROJ = 512
MLA_HEADS_PER_STEP = 8
TM_MLP = 512
FF_CHUNK = 1024

_NT = (((1,), (1,)), ((), ()))


def _dot(a, b):
    return jnp.dot(a, b, preferred_element_type=jnp.float32)


def _dot_nt(a, b):
    return lax.dot_general(a, b, _NT, preferred_element_type=jnp.float32)


def _ones_rows(rows):
    r = lax.broadcasted_iota(jnp.int32, (rows, 1), 0)
    return jnp.where(r % V_ROWS == ONES_ROW, 1.0, 0.0).astype(jnp.float32)


def _proj_kernel(x_ref, g_ref, w1_ref, w1t_ref, qn_ref, wuqt_ref, kvn_ref, wuk_ref, wuvt_ref,
                 c_ref, s_ref, ct_ref, st_ref,
                 qmt_ref, km_ref, vmt_ref, qdt_ref, kd_ref, vdt_ref, qit_ref, ki_ref, wit_ref, qct_ref, kc_ref, vct_ref):
    x = x_ref[...]
    h = x * lax.rsqrt(jnp.mean(x * x, axis=-1, keepdims=True) + EPS)
    h = (h * g_ref[...]).astype(MXU_DTYPE)

    def seg(a, b):
        return _dot(h, w1_ref[:, a:b])

    def seg_t(a, b):
        return _dot_nt(w1t_ref[a:b, :], h)

    n_sub = ct_ref.shape[0]
    tq = ct_ref.shape[2]

    def put_t(ref, val, rows=slice(None)):
        for j in range(n_sub):
            ref[j, rows, :] = val[:, j * tq:(j + 1) * tq].astype(ref.dtype)

    c_q = seg(C_CQ, C_CKV)
    c_q = c_q * lax.rsqrt(jnp.mean(c_q * c_q, axis=-1, keepdims=True) + EPS)
    c_q = (c_q * qn_ref[...]).astype(MXU_DTYPE)
    qa = _dot_nt(wuqt_ref[...], c_q)
    half = MLA_ROPE // 2
    rope = slice(MLA_NOPE, MLA_NOPE + half)
    cos_h = jnp.concatenate([ct_ref[j, rope, :] for j in range(n_sub)], axis=1)
    sin_h = jnp.concatenate([st_ref[j, rope, :] for j in range(n_sub)], axis=1)
    for hh in range(H_MLA):
        b0 = hh * LANES
        r1 = slice(b0 + MLA_NOPE, b0 + MLA_NOPE + half)
        r2 = slice(b0 + MLA_NOPE + half, b0 + MLA_NOPE + MLA_ROPE)
        rest = slice(b0 + MLA_NOPE + MLA_ROPE, b0 + LANES)
        x1, x2 = qa[r1, :], qa[r2, :]
        put_t(qmt_ref, qa[b0:b0 + MLA_NOPE, :], slice(b0, b0 + MLA_NOPE))
        put_t(qmt_ref, x1 * cos_h - x2 * sin_h, r1)
        put_t(qmt_ref, x2 * cos_h + x1 * sin_h, r2)
        put_t(qmt_ref, qa[rest, :], rest)

    c_kv = seg(C_CKV, C_KR)
    c_kv = c_kv * lax.rsqrt(jnp.mean(c_kv * c_kv, axis=-1, keepdims=True) + EPS)
    c_kv = (c_kv * kvn_ref[...]).astype(MXU_DTYPE)
    k_rope = seg(C_KR, C_KRR) * c_ref[...] + seg(C_KRR, C_KB) * s_ref[...]
    kk = _dot(c_kv, wuk_ref[...])
    for hh in range(H_MLA):
        sl = slice(hh * LANES, (hh + 1) * LANES)
        km_ref[:, sl] = (kk[:, sl] + k_rope).astype(km_ref.dtype)
    put_t(vmt_ref, _dot_nt(wuvt_ref[...], c_kv) + _ones_rows(H_MLA * V_ROWS))

    kd_ref[...] = seg(C_KB, C_KI).astype(kd_ref.dtype)
    ki_ref[...] = seg(C_KI, C_KC).astype(ki_ref.dtype)
    kc_ref[...] = seg(C_KC, N1).astype(kc_ref.dtype)
    put_t(qdt_ref, seg_t(R_QB, R_QI))
    put_t(qit_ref, seg_t(R_QI, R_QC))
    put_t(qct_ref, seg_t(R_QC, R_VB))
    put_t(vdt_ref, seg_t(R_VB, R_VC) + _ones_rows(V_ROWS))
    put_t(vct_ref, seg_t(R_VC, R_WI) + _ones_rows(H_DIFF * V_ROWS))
    put_t(wit_ref, seg_t(R_WI, R_WI + WI_ROWS)[:H_IDX, :])


def _proj_call(x2d, g, w1, w1t, qn, wuqt, kvn, wuk, wuvt, cos_r, sin_r, cos_c, sin_c):
    t = x2d.shape[0]
    tm, tq = TM_PROJ, TQ
    assert t % tm == 0 and tm % tq == 0
    n = t // tq
    row = lambda w: pl.BlockSpec((tm, w), lambda i: (i, 0))
    colmajor = lambda r: pl.BlockSpec((tm // tq, r, tq), lambda i: (i, 0, 0))
    full = lambda a: pl.BlockSpec(a.shape, lambda i: (0, 0))
    bf = MXU_DTYPE
    width_c = 2 * H_DIFF * DIFF_QK
    outs = [
        (colmajor(H_MLA * LANES), (n, H_MLA * LANES, tq), bf),
        (row(H_MLA * LANES), (t, H_MLA * LANES), bf),
        (colmajor(H_MLA * V_ROWS), (n, H_MLA * V_ROWS, tq), bf),
        (colmajor(H_DSA * LANES), (n, H_DSA * LANES, tq), bf),
        (row(LANES), (t, LANES), bf),
        (colmajor(V_ROWS), (n, V_ROWS, tq), bf),
        (colmajor(H_IDX * D_IDX), (n, H_IDX * D_IDX, tq), bf),
        (row(H_IDX * D_IDX), (t, H_IDX * D_IDX), bf),
        (colmajor(H_IDX), (n, H_IDX, tq), jnp.float32),
        (colmajor(width_c), (n, width_c, tq), bf),
        (row(width_c), (t, width_c), bf),
        (colmajor(H_DIFF * V_ROWS), (n, H_DIFF * V_ROWS, tq), bf),
    ]
    return pl.pallas_call(
        _proj_kernel,
        grid=(t // tm,),
        in_specs=[row(D_MODEL), full(g), full(w1), full(w1t), full(qn), full(wuqt), full(kvn), full(wuk),
                  full(wuvt), row(LANES), row(LANES), colmajor(LANES), colmajor(LANES)],
        out_specs=[o[0] for o in outs],
        out_shape=[jax.ShapeDtypeStruct(o[1], o[2]) for o in outs],
        compiler_params=pltpu.CompilerParams(dimension_semantics=("arbitrary",), vmem_limit_bytes=VMEM_LIMIT),
        name="proj",
    )(x2d, g, w1, w1t, qn, wuqt, kvn, wuk, wuvt, cos_r, sin_r, cos_c, sin_c)


def _flash_update(s, m_ref, acc_ref, v_t):
    m_prev = m_ref[...]
    m_new = jnp.maximum(m_prev, jnp.max(s, axis=0, keepdims=True))
    alpha = jnp.exp2(m_prev - m_new)
    p = jnp.exp2(s - m_new).astype(MXU_DTYPE)
    acc_ref[...] = alpha * acc_ref[...] + _dot(v_t, p)
    m_ref[...] = m_new


def _normalized(acc, width):
    return acc[:width, :] / acc[ONES_ROW:ONES_ROW + 1, :]


def _causal_tile(tq):
    key = lax.broadcasted_iota(jnp.int32, (tq, tq), 0)
    qry = lax.broadcasted_iota(jnp.int32, (tq, tq), 1)
    return key <= qry


def _pipelined_key_loop(i, n_heads, produce, consume, prepare=None):
    prepare = prepare or (lambda c, kind: None)

    def step(c, slot, kind, nxt):
        ctx = prepare(c, kind)
        for hh in range(n_heads):
            consume(c, slot, hh, kind, ctx)
            if nxt:
                produce(c + 1, 1 - slot, hh)

    for hh in range(n_heads):
        produce(0, 0, hh)
    n_far_pairs = jnp.maximum(i - 1, 0) // 2

    def body(p, carry):
        step(2 * p, 0, "far", True)
        step(2 * p + 1, 1, "far", True)
        return carry

    lax.fori_loop(0, n_far_pairs, body, 0)
    c0 = 2 * n_far_pairs
    tail = i + 1 - c0

    @pl.when(tail == 3)
    def _():
        step(c0, 0, "far", True)
        step(c0 + 1, 1, "prev", True)
        step(c0 + 2, 0, "diag", False)

    @pl.when(tail == 2)
    def _():
        step(c0, 0, "prev", True)
        step(c0 + 1, 1, "diag", False)

    @pl.when(tail == 1)
    def _():
        step(c0, 0, "diag", False)


def _mla_kernel(qt_ref, k_ref, vt_ref, o_ref, s_ref, m_ref, acc_ref):
    i = pl.program_id(2)
    tq = qt_ref.shape[-1]
    nh = MLA_HEADS_PER_STEP
    m_ref[...] = jnp.full(m_ref.shape, NEG, jnp.float32)
    acc_ref[...] = jnp.zeros(acc_ref.shape, jnp.float32)
    causal = _causal_tile(tq)

    def produce(c, slot, hh):
        sl = slice(hh * LANES, (hh + 1) * LANES)
        kb = k_ref[0, pl.ds(pl.multiple_of(c * tq, tq), tq), sl]
        s_ref[slot, hh] = _dot(kb, qt_ref[0, 0, sl, :])

    def consume(c, slot, hh, kind, ctx):
        s = s_ref[slot, hh]
        if kind == "diag":
            s = jnp.where(causal, s, NEG)
        _flash_update(s, m_ref.at[hh], acc_ref.at[hh], vt_ref[0, c, hh * V_ROWS:(hh + 1) * V_ROWS, :])

    _pipelined_key_loop(i, nh, produce, consume)
    o_ref[0] = jnp.concatenate([_normalized(acc_ref[hh], MLA_V).T for hh in range(nh)], axis=-1).astype(o_ref.dtype)


def _mla_call(qmt, km, vmt):
    b, nk, _, tq = qmt.shape
    s = nk * tq
    nh = MLA_HEADS_PER_STEP
    return pl.pallas_call(
        _mla_kernel,
        grid=(b, H_MLA // nh, nk),
        in_specs=[pl.BlockSpec((1, 1, nh * LANES, tq), lambda bb, hp, i: (bb, i, hp, 0)),
                  pl.BlockSpec((1, s, nh * LANES), lambda bb, hp, i: (bb, 0, hp)),
                  pl.BlockSpec((1, nk, nh * V_ROWS, tq), lambda bb, hp, i: (bb, 0, hp, 0))],
        out_specs=pl.BlockSpec((1, tq, nh * MLA_V), lambda bb, hp, i: (bb, i, hp)),
        out_shape=jax.ShapeDtypeStruct((b, s, H_MLA * MLA_V), MXU_DTYPE),
        scratch_shapes=[pltpu.VMEM((2, nh, tq, tq), jnp.float32),
                        pltpu.VMEM((nh, 1, tq), jnp.float32), pltpu.VMEM((nh, V_ROWS, tq), jnp.float32)],
        compiler_params=pltpu.CompilerParams(dimension_semantics=("arbitrary",) * 3, vmem_limit_bytes=VMEM_LIMIT),
        name="mla",
    )(qmt, km, vmt)


def _ordered_to_float(k):
    bits = k ^ ((k >> 31) & jnp.int32(0x7FFFFFFF))
    return lax.bitcast_convert_type(bits, jnp.float32)


def _coarse(x):
    bits = lax.bitcast_convert_type(x, jnp.int32) & jnp.int32(-65536)
    return lax.bitcast_convert_type(bits, jnp.float32).astype(jnp.bfloat16)


def _dsa_kernel(n_sel, qdt_ref, qit_ref, wit_ref, ki_ref, kd_ref, vdt_ref, bd_ref, bp_ref, o_ref,
                sc_ref, sc16_ref, qim_ref, thr_ref, s_ref, m_ref, acc_ref):
    i = pl.program_id(1)
    tq = qdt_ref.shape[-1]
    n_chunks = i + 1
    causal = _causal_tile(tq)
    int_min = jnp.int32(-2 ** 31)

    qi = qit_ref[0, 0]
    row_head = lax.broadcasted_iota(jnp.int32, qi.shape, 0) // D_IDX
    for hh in range(H_IDX):
        qim_ref[hh] = jnp.where(row_head == hh, qi, jnp.zeros_like(qi))

    def score_chunk(jc, diag):
        kc = ki_ref[0, pl.ds(pl.multiple_of(jc * tq, tq), tq), :]
        w = wit_ref[0, 0]
        acc = jnp.zeros((tq, tq), jnp.float32)
        for hh in range(H_IDX):
            acc = acc + jnp.maximum(_dot(kc, qim_ref[hh]), 0.0) * w[hh:hh + 1, :]
        if diag:
            acc = jnp.where(causal, acc, NEG)
        sc_ref[jc] = acc
        sc16_ref[jc] = acc.astype(jnp.bfloat16)

    def score_body(jc, carry):
        score_chunk(jc, False)
        return carry

    lax.fori_loop(0, i, score_body, 0)
    score_chunk(i, True)

    @pl.when(n_chunks % 2 == 1)
    def _():
        sc_ref[n_chunks] = jnp.full((tq, tq), -jnp.inf, jnp.float32)
        sc16_ref[n_chunks] = jnp.full((tq, tq), -jnp.inf, jnp.bfloat16)

    n_pairs = (n_chunks + 1) // 2

    def count_where(hits):
        def fold(jc):
            return jnp.sum(hits(sc_ref[jc], jc).reshape(tq // SUBLANES, SUBLANES, tq), axis=0)

        def body(jp, acc):
            return acc + fold(2 * jp) + fold(2 * jp + 1)
        acc = lax.fori_loop(0, n_pairs, body, jnp.zeros((SUBLANES, tq), jnp.float32))
        return jnp.sum(acc, axis=0, keepdims=True)

    def count_ge(thr):
        return count_where(lambda sc, jc: jnp.where(sc >= thr, 1.0, 0.0))

    pack = 2 * SUBLANES
    one16 = jnp.ones((tq, tq), jnp.bfloat16)
    zero16 = jnp.zeros((tq, tq), jnp.bfloat16)

    def count_ge_coarse(thr):
        thr16 = _coarse(thr)

        def fold(jc):
            hit = jnp.where(sc16_ref[jc] >= thr16, one16, zero16).reshape(tq // pack, pack, tq)
            parts = [hit[r] for r in range(tq // pack)]
            while len(parts) > 1:
                parts = [parts[r] + parts[r + 1] for r in range(0, len(parts), 2)]
            return parts[0].astype(jnp.float32)

        def body(jp, acc):
            return acc + fold(2 * jp) + fold(2 * jp + 1)
        acc = lax.fori_loop(0, n_pairs, body, jnp.zeros((pack, tq), jnp.float32))
        return jnp.sum(acc, axis=0, keepdims=True)

    c0 = count_ge_coarse(jnp.zeros((1, tq), jnp.float32))
    start = (jnp.where(c0 >= n_sel, jnp.int32(0), int_min), jnp.where(c0 >= n_sel, c0, jnp.float32(2 * n_sel)))

    def bit_body(count, top):
        def body(t, carry):
            k, c_k = carry
            cand = k + lax.shift_left(jnp.int32(1), jnp.int32(top) - t)
            c = count(_ordered_to_float(cand))
            take = c >= n_sel
            return jnp.where(take, cand, k), jnp.where(take, c, c_k)
        return body

    k16, _ = lax.fori_loop(0, 15, bit_body(count_ge_coarse, 30), start)
    g_key = k16 + jnp.where(k16 < 0, jnp.int32(0xFFFF), jnp.int32(0))
    base = (g_key - jnp.int32(1 << 16), jnp.full((1, tq), 2.0 * n_sel, jnp.float32))
    kth, c_kth = lax.fori_loop(0, 17, bit_body(count_ge, 16), base)
    thr_ref[...] = _ordered_to_float(kth)

    @pl.when(jnp.max(c_kth) > n_sel)
    def _():
        thr_ = thr_ref[...]
        need = n_sel - count_where(lambda sc, jc: jnp.where(sc > thr_, 1.0, 0.0))
        key = lax.broadcasted_iota(jnp.int32, (tq, tq), 0)
        col = lax.broadcasted_iota(jnp.int32, (tq, tq), 1)
        tri = jnp.where(col <= key, 1.0, 0.0).astype(MXU_DTYPE)

        def drop_body(jc, before):
            sc = sc_ref[jc]
            tie = jnp.where(sc == thr_, 1.0, 0.0)
            rank = _dot(tri, tie.astype(MXU_DTYPE)) + before
            sc_ref[jc] = jnp.where(rank > need, jnp.where(sc == thr_, -jnp.inf, sc), sc)
            return rank[tq - 1:tq, :]

        lax.fori_loop(0, n_chunks, drop_body, jnp.zeros((1, tq), jnp.float32))

    thr = thr_ref[...]

    m_ref[...] = jnp.full(m_ref.shape, NEG, jnp.float32)
    acc_ref[...] = jnp.zeros(acc_ref.shape, jnp.float32)

    def produce(c, slot, hh):
        kb = kd_ref[0, pl.ds(pl.multiple_of(c * tq, tq), tq), :]
        s_ref[slot, hh] = _dot(kb, qdt_ref[0, 0, hh * LANES:(hh + 1) * LANES, :])

    def prepare(c, kind):
        return sc_ref[c] >= thr

    def consume(c, slot, hh, kind, sel):
        s = s_ref[slot, hh]
        if kind == "diag":
            s = jnp.where(causal, s + bd_ref[hh], NEG)
        elif kind == "prev":
            s = s + bp_ref[hh]
        s = jnp.where(sel, s, NEG)
        _flash_update(s, m_ref.at[hh], acc_ref.at[hh], vdt_ref[0, c])

    _pipelined_key_loop(i, H_DSA, produce, consume, prepare)
    o_ref[0] = jnp.concatenate([_normalized(acc_ref[hh], DSA_DIM).T for hh in range(H_DSA)], axis=-1).astype(o_ref.dtype)


def _dsa_call(qdt, qit, wit, ki, kd, vdt, bias_diag, bias_prev):
    b, nk, _, tq = qdt.shape
    s = nk * tq
    n_sel = min(TOPK_MAX, s // 4)
    assert tq >= n_sel and tq + 1 >= MAX_DISTANCE
    qblk = lambda r: pl.BlockSpec((1, 1, r, tq), lambda bb, i: (bb, i, 0, 0))
    kblk = lambda w: pl.BlockSpec((1, s, w), lambda bb, i: (bb, 0, 0))
    bblk = pl.BlockSpec((H_DSA, tq, tq), lambda bb, i: (0, 0, 0))
    return pl.pallas_call(
        functools.partial(_dsa_kernel, n_sel),
        grid=(b, nk),
        in_specs=[qblk(H_DSA * LANES), qblk(H_IDX * D_IDX), qblk(H_IDX),
                  kblk(H_IDX * D_IDX), kblk(LANES),
                  pl.BlockSpec((1, nk, V_ROWS, tq), lambda bb, i: (bb, 0, 0, 0)), bblk, bblk],
        out_specs=pl.BlockSpec((1, tq, H_DSA * DSA_DIM), lambda bb, i: (bb, i, 0)),
        out_shape=jax.ShapeDtypeStruct((b, s, H_DSA * DSA_DIM), MXU_DTYPE),
        scratch_shapes=[pltpu.VMEM((nk + 1, tq, tq), jnp.float32),
                        pltpu.VMEM((nk + 1, tq, tq), jnp.bfloat16),
                        pltpu.VMEM((H_IDX, H_IDX * D_IDX, tq), MXU_DTYPE),
                        pltpu.VMEM((1, tq), jnp.float32),
                        pltpu.VMEM((2, H_DSA, tq, tq), jnp.float32),
                        pltpu.VMEM((H_DSA, 1, tq), jnp.float32),
                        pltpu.VMEM((H_DSA, V_ROWS, tq), jnp.float32)],
        compiler_params=pltpu.CompilerParams(dimension_semantics=("arbitrary",) * 2, vmem_limit_bytes=VMEM_LIMIT),
        name="dsa",
    )(qdt, qit, wit, ki, kd, vdt, bias_diag, bias_prev)


def _diff_kernel(lam_scale, qct_ref, kc_ref, vct_ref, bd_ref, bp_ref, lam_ref, g_ref, o_ref,
                 qm_ref, s_ref, m_ref, acc_ref):
    i = pl.program_id(1)
    tq = qct_ref.shape[-1]
    n_maps = 2 * H_DIFF
    causal = _causal_tile(tq)

    q = qct_ref[0, 0]
    row_map = lax.broadcasted_iota(jnp.int32, q.shape, 0) // DIFF_QK
    for mm in range(n_maps):
        qm_ref[mm] = jnp.where(row_map == mm, q, jnp.zeros_like(q))
    m_ref[...] = jnp.full(m_ref.shape, NEG, jnp.float32)
    acc_ref[...] = jnp.zeros(acc_ref.shape, jnp.float32)

    def produce(c, slot, mm):
        kb = kc_ref[0, pl.ds(pl.multiple_of(c * tq, tq), tq), :]
        s_ref[slot, mm] = _dot(kb, qm_ref[mm])

    def consume(c, slot, mm, kind, ctx):
        hh = mm // 2
        s = s_ref[slot, mm]
        if kind == "diag":
            s = jnp.where(causal, s + bd_ref[hh], NEG)
        elif kind == "prev":
            s = s + bp_ref[hh]
        _flash_update(s, m_ref.at[mm], acc_ref.at[mm], vct_ref[0, c, hh * V_ROWS:(hh + 1) * V_ROWS, :])

    _pipelined_key_loop(i, n_maps, produce, consume)

    lam = lam_ref[...]
    outs = []
    for hh in range(H_DIFF):
        o = _normalized(acc_ref[2 * hh], DIFF_V) - lam * _normalized(acc_ref[2 * hh + 1], DIFF_V)
        o = o * lax.rsqrt(jnp.mean(o * o, axis=0, keepdims=True) + EPS)
        outs.append((o * g_ref[...] * lam_scale).T)
    o_ref[0] = jnp.concatenate(outs, axis=-1).astype(o_ref.dtype)


def _diff_call(qct, kc, vct, bias_diag, bias_prev, lam, subln, lam_init):
    b, nk, width, tq = qct.shape
    s = nk * tq
    assert tq + 1 >= MAX_DISTANCE
    bblk = pl.BlockSpec((H_DIFF, tq, tq), lambda bb, i: (0, 0, 0))
    return pl.pallas_call(
        functools.partial(_diff_kernel, 1.0 - lam_init),
        grid=(b, nk),
        in_specs=[pl.BlockSpec((1, 1, width, tq), lambda bb, i: (bb, i, 0, 0)),
                  pl.BlockSpec((1, s, width), lambda bb, i: (bb, 0, 0)),
                  pl.BlockSpec((1, nk, H_DIFF * V_ROWS, tq), lambda bb, i: (bb, 0, 0, 0)), bblk, bblk,
                  pl.BlockSpec((1, 1), lambda bb, i: (0, 0)), pl.BlockSpec((DIFF_V, 1), lambda bb, i: (0, 0))],
        out_specs=pl.BlockSpec((1, tq, H_DIFF * DIFF_V), lambda bb, i: (bb, i, 0)),
        out_shape=jax.ShapeDtypeStruct((b, s, H_DIFF * DIFF_V), MXU_DTYPE),
        scratch_shapes=[pltpu.VMEM((2 * H_DIFF, width, tq), MXU_DTYPE),
                        pltpu.VMEM((2, 2 * H_DIFF, tq, tq), jnp.float32),
                        pltpu.VMEM((2 * H_DIFF, 1, tq), jnp.float32),
                        pltpu.VMEM((2 * H_DIFF, V_ROWS, tq), jnp.float32)],
        compiler_params=pltpu.CompilerParams(dimension_semantics=("arbitrary",) * 2, vmem_limit_bytes=VMEM_LIMIT),
        name="diff",
    )(qct, kc, vct, bias_diag, bias_prev, lam, subln)


def _weight_copies(wup_hbm, wdn_hbm, wup_ref, wdn_ref, sem):
    return (pltpu.make_async_copy(wup_hbm, wup_ref, sem.at[0]), pltpu.make_async_copy(wdn_hbm, wdn_ref, sem.at[1]))


def _out_mlp_kernel(final, x_ref, oa_ref, ob_ref, oc_ref, wo_ref, g_ref, wup_hbm, wdn_hbm, gf_ref, y_ref,
                    wup_ref, wdn_ref, sem):
    @pl.when(pl.program_id(0) == 0)
    def _():
        copies = _weight_copies(wup_hbm, wdn_hbm, wup_ref, wdn_ref, sem)
        for c in copies:
            c.start()
        for c in copies:
            c.wait()

    mix = jnp.concatenate([oa_ref[...], ob_ref[...], oc_ref[...]], axis=-1)
    x = x_ref[...] + _dot(mix, wo_ref[...])
    h = x * lax.rsqrt(jnp.mean(x * x, axis=-1, keepdims=True) + EPS)
    h = (h * g_ref[...]).astype(MXU_DTYPE)
    y = x
    for c in range(D_FF // FF_CHUNK):
        cs = slice(c * FF_CHUNK, (c + 1) * FF_CHUNK)
        u = jnp.maximum(_dot(h, wup_ref[:, cs]), 0.0)
        y = y + _dot((u * u).astype(MXU_DTYPE), wdn_ref[cs, :])
    if final:
        y = y * lax.rsqrt(jnp.mean(y * y, axis=-1, keepdims=True) + EPS) * gf_ref[...]
    y_ref[...] = y


def _out_mlp_call(x2d, oa, ob, oc, wo, g, wup, wdn, gf, final):
    t = x2d.shape[0]
    tm = TM_MLP
    assert t % tm == 0
    row = lambda a: pl.BlockSpec((tm, a.shape[1]), lambda i: (i, 0))
    full = lambda a: pl.BlockSpec(a.shape, lambda i: (0, 0))
    hbm = pl.BlockSpec(memory_space=pl.ANY)
    return pl.pallas_call(
        functools.partial(_out_mlp_kernel, final),
        grid=(t // tm,),
        in_specs=[row(x2d), row(oa), row(ob), row(oc), full(wo), full(g), hbm, hbm, full(gf)],
        out_specs=row(x2d),
        out_shape=jax.ShapeDtypeStruct(x2d.shape, jnp.float32),
        scratch_shapes=[pltpu.VMEM(wup.shape, wup.dtype), pltpu.VMEM(wdn.shape, wdn.dtype),
                        pltpu.SemaphoreType.DMA((2,))],
        compiler_params=pltpu.CompilerParams(dimension_semantics=("arbitrary",), vmem_limit_bytes=VMEM_LIMIT),
        name="out_mlp",
    )(x2d, oa, ob, oc, wo, g, wup, wdn, gf)


def _place(dst_cols, pieces):
    rows = pieces[0][1].shape[0]
    out = jnp.zeros((rows, dst_cols), jnp.float32)
    for off, a in pieces:
        out = lax.dynamic_update_slice(out, a.astype(jnp.float32), (0, off))
    return out


def _rot_half_cols(w):
    half = w.shape[1] // 2
    return jnp.concatenate([-w[:, half:], w[:, :half]], axis=1)


def _prep_in_proj(w_in):
    o = _OFF
    col = lambda k: w_in[:, o[k]:o[k + 1]]
    kr = col(2)
    s_dsa = (DSA_DIM ** -0.5) * LOG2E
    s_diff = (DIFF_QK ** -0.5) * LOG2E
    s_idx = (D_IDX ** -0.5) * (H_IDX ** -0.5)
    pieces = [(C_CQ, col(0)), (C_CKV, col(1)),
              (C_KR + MLA_NOPE, kr), (C_KRR + MLA_NOPE, _rot_half_cols(kr)),
              (C_KB, col(4)), (C_KC, col(10))]
    for hh in range(H_IDX):
        pieces.append((C_KI + hh * D_IDX, col(7)))
    t_pieces = [(R_QI, col(6)), (R_QC, col(9) * s_diff), (R_VB, col(5)), (R_WI, col(8) * s_idx)]
    qb, vc = col(3), col(11)
    for hh in range(H_DSA):
        t_pieces.append((R_QB + hh * LANES, qb[:, hh * DSA_DIM:(hh + 1) * DSA_DIM] * s_dsa))
    for hh in range(H_DIFF):
        t_pieces.append((R_VC + hh * V_ROWS, vc[:, hh * DIFF_V:(hh + 1) * DIFF_V]))
    return _place(N1, pieces).astype(MXU_DTYPE), _place(N1T, t_pieces).T.astype(MXU_DTYPE)


def _prep_mla_up(w_uq, w_ukv):
    dq = MLA_NOPE + MLA_ROPE
    s_mla = (dq ** -0.5) * LOG2E
    q_p, k_p, v_p = [], [], []
    for hh in range(H_MLA):
        wq = w_uq[:, hh * dq:(hh + 1) * dq] * s_mla
        q_p.append((hh * LANES, wq))
        wkv = w_ukv[:, hh * (MLA_NOPE + MLA_V):(hh + 1) * (MLA_NOPE + MLA_V)]
        k_p.append((hh * LANES, wkv[:, :MLA_NOPE]))
        v_p.append((hh * V_ROWS, wkv[:, MLA_NOPE:]))
    width = H_MLA * LANES
    wuq, wuk = (_place(width, p) for p in (q_p, k_p))
    wuv = _place(H_MLA * V_ROWS, v_p)
    return wuq.T.astype(MXU_DTYPE), wuk.astype(MXU_DTYPE), wuv.T.astype(MXU_DTYPE)


def _rope_tables(positions, tm):
    half = MLA_ROPE // 2
    freqs = ROPE_BASE ** (-jnp.arange(half, dtype=jnp.float32) / half)
    ang = positions.astype(jnp.float32)[..., None] * freqs
    cos, sin = jnp.cos(ang), jnp.sin(ang)
    lead = cos.shape[:-1]
    ones = jnp.ones(lead + (MLA_NOPE,), jnp.float32)
    zeros = jnp.zeros(lead + (LANES - MLA_NOPE - MLA_ROPE,), jnp.float32)
    cos_r = jnp.concatenate([ones, cos, cos, zeros], axis=-1).reshape(-1, LANES)
    sin_r = jnp.concatenate([0.0 * ones, sin, sin, zeros], axis=-1).reshape(-1, LANES)
    to_cols = lambda a: jnp.transpose(a.reshape(-1, tm, LANES), (0, 2, 1))
    return cos_r, sin_r, to_cols(cos_r), to_cols(sin_r)


def _t5_bucket(rel):
    n = jnp.maximum(rel, 0)
    max_exact = N_BUCKETS // 2
    nf = jnp.maximum(n, 1).astype(jnp.float32)
    large = max_exact + (jnp.log(nf / max_exact) / math.log(MAX_DISTANCE / max_exact)
                         * (N_BUCKETS - max_exact)).astype(jnp.int32)
    large = jnp.minimum(large, N_BUCKETS - 1)
    return jnp.where(n < max_exact, n, large)


def _bias_tiles(rel_bias, tq):
    n_heads = rel_bias.shape[1]
    dist = jnp.arange(2 * tq, dtype=jnp.int32)
    b = ((rel_bias[_t5_bucket(dist)] - rel_bias[N_BUCKETS - 1]) * LOG2E).T

    def toeplitz(u):
        flat = jnp.tile(u, (1, tq))[:, :tq * (2 * tq - 1)]
        return flat.reshape(n_heads, tq, 2 * tq - 1)[:, :, :tq]

    diag = toeplitz(jnp.concatenate([b[:, :tq], jnp.zeros_like(b[:, :tq])], axis=1))
    prev = toeplitz(jnp.concatenate([b[:, tq:], b[:, :tq]], axis=1))
    return diag, prev


def kernel(x, positions, rel_bias, norm_attn, w_in, q_norm, w_uq, kv_norm, w_ukv, diff_lambda, diff_subln,
           w_out, norm_mlp, w_up, w_down, norm_final):
    b, s, d = x.shape
    depth = w_in.shape[0]
    assert s % TQ == 0
    nk = s // TQ
    cos_r, sin_r, cos_c, sin_c = _rope_tables(positions, TQ)
    bias_diag, bias_prev = _bias_tiles(rel_bias.astype(jnp.float32), TQ)
    row2 = lambda v: v.reshape(1, -1).astype(jnp.float32)
    x2d = x.reshape(b * s, d)
    for l in range(depth):
        w1, w1t = _prep_in_proj(w_in[l])
        wuqt, wuk, wuvt = _prep_mla_up(w_uq[l], w_ukv[l])
        qmt, km, vmt, qdt, kd, vdt, qit, ki, wit, qct, kc, vct = _proj_call(
            x2d, row2(norm_attn[l]), w1, w1t, row2(q_norm[l]), wuqt, row2(kv_norm[l]), wuk, wuvt,
            cos_r, sin_r, cos_c, sin_c)
        r3 = lambda a: a.reshape(b, s, a.shape[-1])
        r4 = lambda a: a.reshape(b, nk, a.shape[-2], a.shape[-1])
        o_a = _mla_call(r4(qmt), r3(km), r4(vmt))
        o_b = _dsa_call(r4(qdt), r4(qit), r4(wit), r3(ki), r3(kd), r4(vdt), bias_diag[:H_DSA], bias_prev[:H_DSA])
        lam_init = 0.8 - 0.6 * math.exp(-0.3 * l)
        lp = diff_lambda[l].astype(jnp.float32)
        lam = (jnp.exp(jnp.sum(lp[0] * lp[1])) - jnp.exp(jnp.sum(lp[2] * lp[3])) + lam_init).reshape(1, 1)
        o_c = _diff_call(r4(qct), r3(kc), r4(vct), bias_diag[H_DSA:], bias_prev[H_DSA:], lam,
                         diff_subln[l].reshape(-1, 1).astype(jnp.float32), lam_init)
        f2 = lambda a: a.reshape(b * s, a.shape[-1])
        x2d = _out_mlp_call(x2d, f2(o_a), f2(o_b), f2(o_c), w_out[l].astype(MXU_DTYPE), row2(norm_mlp[l]),
                            w_up[l].astype(MXU_DTYPE), w_down[l].astype(MXU_DTYPE), row2(norm_final), l == depth - 1)
    return x2d.reshape(b, s, d)
```

```python
import functools
import math

import jax
import jax.numpy as jnp
import numpy as np
from jax import lax
from jax.experimental import pallas as pl
from jax.experimental.pallas import tpu as pltpu

D_MODEL = 1024
H_MLA, MLA_NOPE, MLA_ROPE, MLA_V = 8, 64, 32, 64
Q_LORA, KV_LORA = 384, 256
H_DSA, DSA_DIM, H_IDX, D_IDX, TOPK_MAX = 4, 64, 8, 32, 256
H_DIFF, DIFF_QK = 4, 32
DIFF_V = 2 * DIFF_QK
D_FF = 4 * D_MODEL
N_BUCKETS, MAX_DISTANCE = 32, 128
ROPE_BASE = 10000.0
EPS = 1e-6
NEG = -1e30
LOG2E = math.log2(math.e)

_SPLITS = (Q_LORA, KV_LORA, MLA_ROPE, H_DSA * DSA_DIM, DSA_DIM, DSA_DIM, H_IDX * D_IDX, D_IDX, H_IDX,
           H_DIFF * 2 * DIFF_QK, H_DIFF * 2 * DIFF_QK, H_DIFF * DIFF_V)
_OFF = tuple(int(o) for o in np.concatenate([[0], np.cumsum(_SPLITS)]))

LANES = 128
SUBLANES = 8
MXU_DTYPE = jnp.bfloat16
VMEM_LIMIT = 56 * 1024 * 1024

C_CQ = 0
C_CKV = C_CQ + Q_LORA
C_KR = C_CKV + KV_LORA
C_KRR = C_KR + LANES
C_KB = C_KRR + LANES
C_KI = C_KB + LANES
C_KC = C_KI + LANES
N1 = C_KC + H_DIFF * 2 * DIFF_QK
R_QB = 0
R_QI = R_QB + H_DSA * LANES
R_QC = R_QI + H_IDX * D_IDX
R_VB = R_QC + H_DIFF * 2 * DIFF_QK
V_ROWS = 80
R_VC = R_VB + V_ROWS
R_WI = R_VC + H_DIFF * V_ROWS
WI_ROWS = 16
N1T = R_WI + WI_ROWS

ONES_ROW = 64

TQ = 256
TM_PROJ = 512
MLA_HEADS_PER_STEP = 8
TM_MLP = 512
FF_CHUNK = 1024

_NT = (((1,), (1,)), ((), ()))


def _dot(a, b):
    return jnp.dot(a, b, preferred_element_type=jnp.float32)


def _dot_nt(a, b):
    return lax.dot_general(a, b, _NT, preferred_element_type=jnp.float32)


def _ones_rows(rows):
    r = lax.broadcasted_iota(jnp.int32, (rows, 1), 0)
    return jnp.where(r % V_ROWS == ONES_ROW, 1.0, 0.0).astype(jnp.float32)


def _proj_kernel(x_ref, g_ref, w1_ref, w1t_ref, qn_ref, wuqt_ref, kvn_ref, wuk_ref, wuvt_ref,
                 c_ref, s_ref, ct_ref, st_ref,
                 qmt_ref, km_ref, vmt_ref, qdt_ref, kd_ref, vdt_ref, qit_ref, ki_ref, wit_ref, qct_ref, kc_ref, vct_ref):
    x = x_ref[...]
    h = x * lax.rsqrt(jnp.mean(x * x, axis=-1, keepdims=True) + EPS)
    h = (h * g_ref[...]).astype(MXU_DTYPE)

    def seg(a, b):
        return _dot(h, w1_ref[:, a:b])

    def seg_t(a, b):
        return _dot_nt(w1t_ref[a:b, :], h)

    n_sub = ct_ref.shape[0]
    tq = ct_ref.shape[2]

    def put_t(ref, val, rows=slice(None)):
        for j in range(n_sub):
            ref[j, rows, :] = val[:, j * tq:(j + 1) * tq].astype(ref.dtype)

    c_q = seg(C_CQ, C_CKV)
    c_q = c_q * lax.rsqrt(jnp.mean(c_q * c_q, axis=-1, keepdims=True) + EPS)
    c_q = (c_q * qn_ref[...]).astype(MXU_DTYPE)
    qa = _dot_nt(wuqt_ref[...], c_q)
    half = MLA_ROPE // 2
    rope = slice(MLA_NOPE, MLA_NOPE + half)
    cos_h = jnp.concatenate([ct_ref[j, rope, :] for j in range(n_sub)], axis=1)
    sin_h = jnp.concatenate([st_ref[j, rope, :] for j in range(n_sub)], axis=1)
    for hh in range(H_MLA):
        b0 = hh * LANES
        r1 = slice(b0 + MLA_NOPE, b0 + MLA_NOPE + half)
        r2 = slice(b0 + MLA_NOPE + half, b0 + MLA_NOPE + MLA_ROPE)
        rest = slice(b0 + MLA_NOPE + MLA_ROPE, b0 + LANES)
        x1, x2 = qa[r1, :], qa[r2, :]
        put_t(qmt_ref, qa[b0:b0 + MLA_NOPE, :], slice(b0, b0 + MLA_NOPE))
        put_t(qmt_ref, x1 * cos_h - x2 * sin_h, r1)
        put_t(qmt_ref, x2 * cos_h + x1 * sin_h, r2)
        put_t(qmt_ref, qa[rest, :], rest)

    c_kv = seg(C_CKV, C_KR)
    c_kv = c_kv * lax.rsqrt(jnp.mean(c_kv * c_kv, axis=-1, keepdims=True) + EPS)
    c_kv = (c_kv * kvn_ref[...]).astype(MXU_DTYPE)
    k_rope = seg(C_KR, C_KRR) * c_ref[...] + seg(C_KRR, C_KB) * s_ref[...]
    kk = _dot(c_kv, wuk_ref[...])
    for hh in range(H_MLA):
        sl = slice(hh * LANES, (hh + 1) * LANES)
        km_ref[:, sl] = (kk[:, sl] + k_rope).astype(km_ref.dtype)
    put_t(vmt_ref, _dot_nt(wuvt_ref[...], c_kv) + _ones_rows(H_MLA * V_ROWS))

    kd_ref[...] = seg(C_KB, C_KI).astype(kd_ref.dtype)
    ki_ref[...] = seg(C_KI, C_KC).astype(ki_ref.dtype)
    kc_ref[...] = seg(C_KC, N1).astype(kc_ref.dtype)
    put_t(qdt_ref, seg_t(R_QB, R_QI))
    put_t(qit_ref, seg_t(R_QI, R_QC))
    put_t(qct_ref, seg_t(R_QC, R_VB))
    put_t(vdt_ref, seg_t(R_VB, R_VC) + _ones_rows(V_ROWS))
    put_t(vct_ref, seg_t(R_VC, R_WI) + _ones_rows(H_DIFF * V_ROWS))
    put_t(wit_ref, seg_t(R_WI, R_WI + WI_ROWS)[:H_IDX, :])


def _proj_call(x2d, g, w1, w1t, qn, wuqt, kvn, wuk, wuvt, cos_r, sin_r, cos_c, sin_c):
    t = x2d.shape[0]
    tm, tq = TM_PROJ, TQ
    assert t % tm == 0 and tm % tq == 0
    n = t // tq
    row = lambda w: pl.BlockSpec((tm, w), lambda i: (i, 0))
    colmajor = lambda r: pl.BlockSpec((tm // tq, r, tq), lambda i: (i, 0, 0))
    full = lambda a: pl.BlockSpec(a.shape, lambda i: (0, 0))
    bf = MXU_DTYPE
    width_c = 2 * H_DIFF * DIFF_QK
    outs = [
        (colmajor(H_MLA * LANES), (n, H_MLA * LANES, tq), bf),
        (row(H_MLA * LANES), (t, H_MLA * LANES), bf),
        (colmajor(H_MLA * V_ROWS), (n, H_MLA * V_ROWS, tq), bf),
        (colmajor(H_DSA * LANES), (n, H_DSA * LANES, tq), bf),
        (row(LANES), (t, LANES), bf),
        (colmajor(V_ROWS), (n, V_ROWS, tq), bf),
        (colmajor(H_IDX * D_IDX), (n, H_IDX * D_IDX, tq), bf),
        (row(LANES), (t, LANES), bf),
        (colmajor(H_IDX), (n, H_IDX, tq), jnp.float32),
        (colmajor(width_c), (n, width_c, tq), bf),
        (row(width_c), (t, width_c), bf),
        (colmajor(H_DIFF * V_ROWS), (n, H_DIFF * V_ROWS, tq), bf),
    ]
    return pl.pallas_call(
        _proj_kernel,
        grid=(t // tm,),
        in_specs=[row(D_MODEL), full(g), full(w1), full(w1t), full(qn), full(wuqt), full(kvn), full(wuk),
                  full(wuvt), row(LANES), row(LANES), colmajor(LANES), colmajor(LANES)],
        out_specs=[o[0] for o in outs],
        out_shape=[jax.ShapeDtypeStruct(o[1], o[2]) for o in outs],
        compiler_params=pltpu.CompilerParams(dimension_semantics=("arbitrary",), vmem_limit_bytes=VMEM_LIMIT),
        name="proj",
    )(x2d, g, w1, w1t, qn, wuqt, kvn, wuk, wuvt, cos_r, sin_r, cos_c, sin_c)


def _flash_update(s, m_ref, acc_ref, v_t):
    m_prev = m_ref[...]
    m_new = jnp.maximum(m_prev, jnp.max(s, axis=0, keepdims=True))
    alpha = jnp.exp2(m_prev - m_new)
    p = jnp.exp2(s - m_new).astype(MXU_DTYPE)
    acc_ref[...] = alpha * acc_ref[...] + _dot(v_t, p)
    m_ref[...] = m_new


def _normalized(acc, width):
    return acc[:width, :] / acc[ONES_ROW:ONES_ROW + 1, :]


def _causal_tile(tq):
    key = lax.broadcasted_iota(jnp.int32, (tq, tq), 0)
    qry = lax.broadcasted_iota(jnp.int32, (tq, tq), 1)
    return key <= qry


def _pipelined_key_loop(i, n_heads, produce, consume, prepare=None):
    prepare = prepare or (lambda c, kind: None)

    def step(c, slot, kind, nxt):
        ctx = prepare(c, kind)
        for hh in range(n_heads):
            consume(c, slot, hh, kind, ctx)
            if nxt:
                produce(c + 1, 1 - slot, hh)

    for hh in range(n_heads):
        produce(0, 0, hh)
    n_far_pairs = jnp.maximum(i - 1, 0) // 2
    n_far_quads = n_far_pairs // 2

    def quad(q, carry):
        for r in range(4):
            step(4 * q + r, r % 2, "far", True)
        return carry

    def body(p, carry):
        step(2 * p, 0, "far", True)
        step(2 * p + 1, 1, "far", True)
        return carry

    lax.fori_loop(0, n_far_quads, quad, 0)
    lax.fori_loop(2 * n_far_quads, n_far_pairs, body, 0)
    c0 = 2 * n_far_pairs
    tail = i + 1 - c0

    @pl.when(tail == 3)
    def _():
        step(c0, 0, "far", True)
        step(c0 + 1, 1, "prev", True)
        step(c0 + 2, 0, "diag", False)

    @pl.when(tail == 2)
    def _():
        step(c0, 0, "prev", True)
        step(c0 + 1, 1, "diag", False)

    @pl.when(tail == 1)
    def _():
        step(c0, 0, "diag", False)


def _mla_kernel(qt_ref, k_ref, vt_ref, o_ref, s_ref, m_ref, acc_ref):
    i = pl.program_id(2)
    tq = qt_ref.shape[-1]
    nh = MLA_HEADS_PER_STEP
    m_ref[...] = jnp.full(m_ref.shape, NEG, jnp.float32)
    acc_ref[...] = jnp.zeros(acc_ref.shape, jnp.float32)
    causal = _causal_tile(tq)

    def produce(c, slot, hh):
        sl = slice(hh * LANES, (hh + 1) * LANES)
        kb = k_ref[0, pl.ds(pl.multiple_of(c * tq, tq), tq), sl]
        s_ref[slot, hh] = _dot(kb, qt_ref[0, 0, sl, :])

    def consume(c, slot, hh, kind, ctx):
        s = s_ref[slot, hh]
        if kind == "diag":
            s = jnp.where(causal, s, NEG)
        _flash_update(s, m_ref.at[hh], acc_ref.at[hh], vt_ref[0, c, hh * V_ROWS:(hh + 1) * V_ROWS, :])

    _pipelined_key_loop(i, nh, produce, consume)
    o_ref[0] = jnp.concatenate([_normalized(acc_ref[hh], MLA_V).T for hh in range(nh)], axis=-1).astype(o_ref.dtype)


def _mla_call(qmt, km, vmt):
    b, nk, _, tq = qmt.shape
    s = nk * tq
    nh = MLA_HEADS_PER_STEP
    return pl.pallas_call(
        _mla_kernel,
        grid=(b, H_MLA // nh, nk),
        in_specs=[pl.BlockSpec((1, 1, nh * LANES, tq), lambda bb, hp, i: (bb, i, hp, 0)),
                  pl.BlockSpec((1, s, nh * LANES), lambda bb, hp, i: (bb, 0, hp)),
                  pl.BlockSpec((1, nk, nh * V_ROWS, tq), lambda bb, hp, i: (bb, 0, hp, 0))],
        out_specs=pl.BlockSpec((1, tq, nh * MLA_V), lambda bb, hp, i: (bb, i, hp)),
        out_shape=jax.ShapeDtypeStruct((b, s, H_MLA * MLA_V), MXU_DTYPE),
        scratch_shapes=[pltpu.VMEM((2, nh, tq, tq), jnp.float32),
                        pltpu.VMEM((nh, 1, tq), jnp.float32), pltpu.VMEM((nh, V_ROWS, tq), jnp.float32)],
        compiler_params=pltpu.CompilerParams(dimension_semantics=("arbitrary",) * 3, vmem_limit_bytes=VMEM_LIMIT),
        name="mla",
    )(qmt, km, vmt)


def _ordered_to_float(k):
    bits = k ^ ((k >> 31) & jnp.int32(0x7FFFFFFF))
    return lax.bitcast_convert_type(bits, jnp.float32)


def _coarse(x):
    bits = lax.bitcast_convert_type(x, jnp.int32) & jnp.int32(-65536)
    return lax.bitcast_convert_type(bits, jnp.float32).astype(jnp.bfloat16)


def _dsa_kernel(n_sel, qdt_ref, qit_ref, wit_ref, ki_ref, kd_ref, vdt_ref, bd_ref, bp_ref, o_ref,
                sc_ref, sc16_ref, qim_ref, thr_ref, s_ref, m_ref, acc_ref):
    i = pl.program_id(1)
    tq = qdt_ref.shape[-1]
    n_chunks = i + 1
    causal = _causal_tile(tq)
    int_min = jnp.int32(-2 ** 31)

    pad = jnp.zeros((LANES - D_IDX, tq), MXU_DTYPE)
    for hh in range(H_IDX):
        qim_ref[hh] = jnp.concatenate([qit_ref[0, 0, hh * D_IDX:(hh + 1) * D_IDX, :], pad], axis=0)

    def score_chunk(jc, diag):
        kc = ki_ref[0, pl.ds(pl.multiple_of(jc * tq, tq), tq), :]
        w = wit_ref[0, 0]
        acc = jnp.zeros((tq, tq), jnp.float32)
        for hh in range(H_IDX):
            acc = acc + jnp.maximum(_dot(kc, qim_ref[hh]), 0.0) * w[hh:hh + 1, :]
        if diag:
            acc = jnp.where(causal, acc, NEG)
        sc_ref[jc] = acc
        sc16_ref[jc] = acc.astype(jnp.bfloat16)

    def score_quad(jq, carry):
        for r in range(4):
            score_chunk(4 * jq + r, False)
        return carry

    lax.fori_loop(0, i // 4, score_quad, 0)
    rest = i % 4

    @pl.when(rest >= 2)
    def _():
        score_chunk(i - rest, False)
        score_chunk(i - rest + 1, False)

    @pl.when(rest % 2 == 1)
    def _():
        score_chunk(i - 1, False)

    score_chunk(i, True)

    @pl.when(n_chunks % 2 == 1)
    def _():
        sc_ref[n_chunks] = jnp.full((tq, tq), -jnp.inf, jnp.float32)
        sc16_ref[n_chunks] = jnp.full((tq, tq), -jnp.inf, jnp.bfloat16)

    n_pairs = (n_chunks + 1) // 2

    def count_where(hits):
        def fold(jc):
            return jnp.sum(hits(sc_ref[jc], jc).reshape(tq // SUBLANES, SUBLANES, tq), axis=0)

        def body(jp, acc):
            return acc + fold(2 * jp) + fold(2 * jp + 1)
        acc = lax.fori_loop(0, n_pairs, body, jnp.zeros((SUBLANES, tq), jnp.float32))
        return jnp.sum(acc, axis=0, keepdims=True)

    def count_ge(thr):
        return count_where(lambda sc, jc: jnp.where(sc >= thr, 1.0, 0.0))

    pack = 2 * SUBLANES
    one16 = jnp.ones((tq, tq), jnp.bfloat16)
    zero16 = jnp.zeros((tq, tq), jnp.bfloat16)

    def count_ge_coarse(thr):
        thr16 = _coarse(thr)

        def fold(jc):
            hit = jnp.where(sc16_ref[jc] >= thr16, one16, zero16).reshape(tq // pack, pack, tq)
            parts = [hit[r] for r in range(tq // pack)]
            while len(parts) > 1:
                parts = [parts[r] + parts[r + 1] for r in range(0, len(parts), 2)]
            return parts[0].astype(jnp.float32)

        def body(jp, acc):
            return acc + fold(2 * jp) + fold(2 * jp + 1)
        acc = lax.fori_loop(0, n_pairs, body, jnp.zeros((pack, tq), jnp.float32))
        return jnp.sum(acc, axis=0, keepdims=True)

    c0 = count_ge_coarse(jnp.zeros((1, tq), jnp.float32))
    start = (jnp.where(c0 >= n_sel, jnp.int32(0), int_min), jnp.where(c0 >= n_sel, c0, jnp.float32(2 * n_sel)))

    def bit_body(count, top):
        def body(t, carry):
            k, c_k = carry
            cand = k + lax.shift_left(jnp.int32(1), jnp.int32(top) - t)
            c = count(_ordered_to_float(cand))
            take = c >= n_sel
            return jnp.where(take, cand, k), jnp.where(take, c, c_k)
        return body

    k16, _ = lax.fori_loop(0, 15, bit_body(count_ge_coarse, 30), start)
    g_key = k16 + jnp.where(k16 < 0, jnp.int32(0xFFFF), jnp.int32(0))
    base = (g_key - jnp.int32(1 << 16), jnp.full((1, tq), 2.0 * n_sel, jnp.float32))
    kth, c_kth = lax.fori_loop(0, 17, bit_body(count_ge, 16), base)
    thr_ref[...] = _ordered_to_float(kth)

    @pl.when(jnp.max(c_kth) > n_sel)
    def _():
        thr_ = thr_ref[...]
        need = n_sel - count_where(lambda sc, jc: jnp.where(sc > thr_, 1.0, 0.0))
        key = lax.broadcasted_iota(jnp.int32, (tq, tq), 0)
        col = lax.broadcasted_iota(jnp.int32, (tq, tq), 1)
        tri = jnp.where(col <= key, 1.0, 0.0).astype(MXU_DTYPE)

        def drop_body(jc, before):
            sc = sc_ref[jc]
            tie = jnp.where(sc == thr_, 1.0, 0.0)
            rank = _dot(tri, tie.astype(MXU_DTYPE)) + before
            sc_ref[jc] = jnp.where(rank > need, jnp.where(sc == thr_, -jnp.inf, sc), sc)
            return rank[tq - 1:tq, :]

        lax.fori_loop(0, n_chunks, drop_body, jnp.zeros((1, tq), jnp.float32))

    thr = thr_ref[...]

    m_ref[...] = jnp.full(m_ref.shape, NEG, jnp.float32)
    acc_ref[...] = jnp.zeros(acc_ref.shape, jnp.float32)

    def produce(c, slot, hh):
        kb = kd_ref[0, pl.ds(pl.multiple_of(c * tq, tq), tq), :]
        s_ref[slot, hh] = _dot(kb, qdt_ref[0, 0, hh * LANES:(hh + 1) * LANES, :])

    def prepare(c, kind):
        return sc_ref[c] >= thr

    def consume(c, slot, hh, kind, sel):
        s = s_ref[slot, hh]
        if kind == "diag":
            s = jnp.where(causal, s + bd_ref[hh], NEG)
        elif kind == "prev":
            s = s + bp_ref[hh]
        s = jnp.where(sel, s, NEG)
        _flash_update(s, m_ref.at[hh], acc_ref.at[hh], vdt_ref[0, c])

    _pipelined_key_loop(i, H_DSA, produce, consume, prepare)
    o_ref[0] = jnp.concatenate([_normalized(acc_ref[hh], DSA_DIM).T for hh in range(H_DSA)], axis=-1).astype(o_ref.dtype)


def _dsa_call(qdt, qit, wit, ki, kd, vdt, bias_diag, bias_prev):
    b, nk, _, tq = qdt.shape
    s = nk * tq
    n_sel = min(TOPK_MAX, s // 4)
    assert tq >= n_sel and tq + 1 >= MAX_DISTANCE
    qblk = lambda r: pl.BlockSpec((1, 1, r, tq), lambda bb, i: (bb, i, 0, 0))
    kblk = lambda w: pl.BlockSpec((1, s, w), lambda bb, i: (bb, 0, 0))
    bblk = pl.BlockSpec((H_DSA, tq, tq), lambda bb, i: (0, 0, 0))
    return pl.pallas_call(
        functools.partial(_dsa_kernel, n_sel),
        grid=(b, nk),
        in_specs=[qblk(H_DSA * LANES), qblk(H_IDX * D_IDX), qblk(H_IDX),
                  kblk(LANES), kblk(LANES),
                  pl.BlockSpec((1, nk, V_ROWS, tq), lambda bb, i: (bb, 0, 0, 0)), bblk, bblk],
        out_specs=pl.BlockSpec((1, tq, H_DSA * DSA_DIM), lambda bb, i: (bb, i, 0)),
        out_shape=jax.ShapeDtypeStruct((b, s, H_DSA * DSA_DIM), MXU_DTYPE),
        scratch_shapes=[pltpu.VMEM((nk + 1, tq, tq), jnp.float32),
                        pltpu.VMEM((nk + 1, tq, tq), jnp.bfloat16),
                        pltpu.VMEM((H_IDX, LANES, tq), MXU_DTYPE),
                        pltpu.VMEM((1, tq), jnp.float32),
                        pltpu.VMEM((2, H_DSA, tq, tq), jnp.float32),
                        pltpu.VMEM((H_DSA, 1, tq), jnp.float32),
                        pltpu.VMEM((H_DSA, V_ROWS, tq), jnp.float32)],
        compiler_params=pltpu.CompilerParams(dimension_semantics=("arbitrary",) * 2, vmem_limit_bytes=VMEM_LIMIT),
        name="dsa",
    )(qdt, qit, wit, ki, kd, vdt, bias_diag, bias_prev)


def _diff_kernel(lam_scale, qct_ref, kc_ref, vct_ref, bd_ref, bp_ref, lam_ref, g_ref, o_ref,
                 qm_ref, s_ref, m_ref, acc_ref):
    i = pl.program_id(1)
    tq = qct_ref.shape[-1]
    n_maps = 2 * H_DIFF
    causal = _causal_tile(tq)

    q = qct_ref[0, 0]
    row_map = lax.broadcasted_iota(jnp.int32, q.shape, 0) // DIFF_QK
    for mm in range(n_maps):
        qm_ref[mm] = jnp.where(row_map == mm, q, jnp.zeros_like(q))
    m_ref[...] = jnp.full(m_ref.shape, NEG, jnp.float32)
    acc_ref[...] = jnp.zeros(acc_ref.shape, jnp.float32)

    def produce(c, slot, mm):
        kb = kc_ref[0, pl.ds(pl.multiple_of(c * tq, tq), tq), :]
        s_ref[slot, mm] = _dot(kb, qm_ref[mm])

    def consume(c, slot, mm, kind, ctx):
        hh = mm // 2
        s = s_ref[slot, mm]
        if kind == "diag":
            s = jnp.where(causal, s + bd_ref[hh], NEG)
        elif kind == "prev":
            s = s + bp_ref[hh]
        _flash_update(s, m_ref.at[mm], acc_ref.at[mm], vct_ref[0, c, hh * V_ROWS:(hh + 1) * V_ROWS, :])

    _pipelined_key_loop(i, n_maps, produce, consume)

    lam = lam_ref[...]
    outs = []
    for hh in range(H_DIFF):
        o = _normalized(acc_ref[2 * hh], DIFF_V) - lam * _normalized(acc_ref[2 * hh + 1], DIFF_V)
        o = o * lax.rsqrt(jnp.mean(o * o, axis=0, keepdims=True) + EPS)
        outs.append((o * g_ref[...] * lam_scale).T)
    o_ref[0] = jnp.concatenate(outs, axis=-1).astype(o_ref.dtype)


def _diff_call(qct, kc, vct, bias_diag, bias_prev, lam, subln, lam_init):
    b, nk, width, tq = qct.shape
    s = nk * tq
    assert tq + 1 >= MAX_DISTANCE
    bblk = pl.BlockSpec((H_DIFF, tq, tq), lambda bb, i: (0, 0, 0))
    return pl.pallas_call(
        functools.partial(_diff_kernel, 1.0 - lam_init),
        grid=(b, nk),
        in_specs=[pl.BlockSpec((1, 1, width, tq), lambda bb, i: (bb, i, 0, 0)),
                  pl.BlockSpec((1, s, width), lambda bb, i: (bb, 0, 0)),
                  pl.BlockSpec((1, nk, H_DIFF * V_ROWS, tq), lambda bb, i: (bb, 0, 0, 0)), bblk, bblk,
                  pl.BlockSpec((1, 1), lambda bb, i: (0, 0)), pl.BlockSpec((DIFF_V, 1), lambda bb, i: (0, 0))],
        out_specs=pl.BlockSpec((1, tq, H_DIFF * DIFF_V), lambda bb, i: (bb, i, 0)),
        out_shape=jax.ShapeDtypeStruct((b, s, H_DIFF * DIFF_V), MXU_DTYPE),
        scratch_shapes=[pltpu.VMEM((2 * H_DIFF, width, tq), MXU_DTYPE),
                        pltpu.VMEM((2, 2 * H_DIFF, tq, tq), jnp.float32),
                        pltpu.VMEM((2 * H_DIFF, 1, tq), jnp.float32),
                        pltpu.VMEM((2 * H_DIFF, V_ROWS, tq), jnp.float32)],
        compiler_params=pltpu.CompilerParams(dimension_semantics=("arbitrary",) * 2, vmem_limit_bytes=VMEM_LIMIT),
        name="diff",
    )(qct, kc, vct, bias_diag, bias_prev, lam, subln)


def _weight_copies(wup_hbm, wdn_hbm, wup_ref, wdn_ref, sem):
    return (pltpu.make_async_copy(wup_hbm, wup_ref, sem.at[0]), pltpu.make_async_copy(wdn_hbm, wdn_ref, sem.at[1]))


def _out_mlp_kernel(final, x_ref, oa_ref, ob_ref, oc_ref, wo_ref, g_ref, wup_hbm, wdn_hbm, gf_ref, y_ref,
                    wup_ref, wdn_ref, sem):
    @pl.when(pl.program_id(0) == 0)
    def _():
        copies = _weight_copies(wup_hbm, wdn_hbm, wup_ref, wdn_ref, sem)
        for c in copies:
            c.start()
        for c in copies:
            c.wait()

    mix = jnp.concatenate([oa_ref[...], ob_ref[...], oc_ref[...]], axis=-1)
    x = x_ref[...] + _dot(mix, wo_ref[...])
    h = x * lax.rsqrt(jnp.mean(x * x, axis=-1, keepdims=True) + EPS)
    h = (h * g_ref[...]).astype(MXU_DTYPE)
    y = x
    for c in range(D_FF // FF_CHUNK):
        cs = slice(c * FF_CHUNK, (c + 1) * FF_CHUNK)
        u = jnp.maximum(_dot(h, wup_ref[:, cs]), 0.0)
        y = y + _dot((u * u).astype(MXU_DTYPE), wdn_ref[cs, :])
    if final:
        y = y * lax.rsqrt(jnp.mean(y * y, axis=-1, keepdims=True) + EPS) * gf_ref[...]
    y_ref[...] = y


def _out_mlp_call(x2d, oa, ob, oc, wo, g, wup, wdn, gf, final):
    t = x2d.shape[0]
    tm = TM_MLP
    assert t % tm == 0
    row = lambda a: pl.BlockSpec((tm, a.shape[1]), lambda i: (i, 0))
    full = lambda a: pl.BlockSpec(a.shape, lambda i: (0, 0))
    hbm = pl.BlockSpec(memory_space=pl.ANY)
    return pl.pallas_call(
        functools.partial(_out_mlp_kernel, final),
        grid=(t // tm,),
        in_specs=[row(x2d), row(oa), row(ob), row(oc), full(wo), full(g), hbm, hbm, full(gf)],
        out_specs=row(x2d),
        out_shape=jax.ShapeDtypeStruct(x2d.shape, jnp.float32),
        scratch_shapes=[pltpu.VMEM(wup.shape, wup.dtype), pltpu.VMEM(wdn.shape, wdn.dtype),
                        pltpu.SemaphoreType.DMA((2,))],
        compiler_params=pltpu.CompilerParams(dimension_semantics=("arbitrary",), vmem_limit_bytes=VMEM_LIMIT),
        name="out_mlp",
    )(x2d, oa, ob, oc, wo, g, wup, wdn, gf)


def _place(dst_cols, pieces):
    rows = pieces[0][1].shape[0]
    out = jnp.zeros((rows, dst_cols), jnp.float32)
    for off, a in pieces:
        out = lax.dynamic_update_slice(out, a.astype(jnp.float32), (0, off))
    return out


def _rot_half_cols(w):
    half = w.shape[1] // 2
    return jnp.concatenate([-w[:, half:], w[:, :half]], axis=1)


def _prep_in_proj(w_in):
    o = _OFF
    col = lambda k: w_in[:, o[k]:o[k + 1]]
    kr = col(2)
    s_dsa = (DSA_DIM ** -0.5) * LOG2E
    s_diff = (DIFF_QK ** -0.5) * LOG2E
    s_idx = (D_IDX ** -0.5) * (H_IDX ** -0.5)
    pieces = [(C_CQ, col(0)), (C_CKV, col(1)),
              (C_KR + MLA_NOPE, kr), (C_KRR + MLA_NOPE, _rot_half_cols(kr)),
              (C_KB, col(4)), (C_KC, col(10))]
    pieces.append((C_KI, col(7)))
    t_pieces = [(R_QI, col(6)), (R_QC, col(9) * s_diff), (R_VB, col(5)), (R_WI, col(8) * s_idx)]
    qb, vc = col(3), col(11)
    for hh in range(H_DSA):
        t_pieces.append((R_QB + hh * LANES, qb[:, hh * DSA_DIM:(hh + 1) * DSA_DIM] * s_dsa))
    for hh in range(H_DIFF):
        t_pieces.append((R_VC + hh * V_ROWS, vc[:, hh * DIFF_V:(hh + 1) * DIFF_V]))
    return _place(N1, pieces).astype(MXU_DTYPE), _place(N1T, t_pieces).T.astype(MXU_DTYPE)


def _prep_mla_up(w_uq, w_ukv):
    dq = MLA_NOPE + MLA_ROPE
    s_mla = (dq ** -0.5) * LOG2E
    q_p, k_p, v_p = [], [], []
    for hh in range(H_MLA):
        wq = w_uq[:, hh * dq:(hh + 1) * dq] * s_mla
        q_p.append((hh * LANES, wq))
        wkv = w_ukv[:, hh * (MLA_NOPE + MLA_V):(hh + 1) * (MLA_NOPE + MLA_V)]
        k_p.append((hh * LANES, wkv[:, :MLA_NOPE]))
        v_p.append((hh * V_ROWS, wkv[:, MLA_NOPE:]))
    width = H_MLA * LANES
    wuq, wuk = (_place(width, p) for p in (q_p, k_p))
    wuv = _place(H_MLA * V_ROWS, v_p)
    return wuq.T.astype(MXU_DTYPE), wuk.astype(MXU_DTYPE), wuv.T.astype(MXU_DTYPE)


def _rope_tables(positions, tm):
    half = MLA_ROPE // 2
    freqs = ROPE_BASE ** (-jnp.arange(half, dtype=jnp.float32) / half)
    ang = positions.astype(jnp.float32)[..., None] * freqs
    cos, sin = jnp.cos(ang), jnp.sin(ang)
    lead = cos.shape[:-1]
    ones = jnp.ones(lead + (MLA_NOPE,), jnp.float32)
    zeros = jnp.zeros(lead + (LANES - MLA_NOPE - MLA_ROPE,), jnp.float32)
    cos_r = jnp.concatenate([ones, cos, cos, zeros], axis=-1).reshape(-1, LANES)
    sin_r = jnp.concatenate([0.0 * ones, sin, sin, zeros], axis=-1).reshape(-1, LANES)
    to_cols = lambda a: jnp.transpose(a.reshape(-1, tm, LANES), (0, 2, 1))
    return cos_r, sin_r, to_cols(cos_r), to_cols(sin_r)


def _t5_bucket(rel):
    n = jnp.maximum(rel, 0)
    max_exact = N_BUCKETS // 2
    nf = jnp.maximum(n, 1).astype(jnp.float32)
    large = max_exact + (jnp.log(nf / max_exact) / math.log(MAX_DISTANCE / max_exact)
                         * (N_BUCKETS - max_exact)).astype(jnp.int32)
    large = jnp.minimum(large, N_BUCKETS - 1)
    return jnp.where(n < max_exact, n, large)


def _bias_tiles(rel_bias, tq):
    n_heads = rel_bias.shape[1]
    dist = jnp.arange(2 * tq, dtype=jnp.int32)
    b = ((rel_bias[_t5_bucket(dist)] - rel_bias[N_BUCKETS - 1]) * LOG2E).T

    def toeplitz(u):
        flat = jnp.tile(u, (1, tq))[:, :tq * (2 * tq - 1)]
        return flat.reshape(n_heads, tq, 2 * tq - 1)[:, :, :tq]

    diag = toeplitz(jnp.concatenate([b[:, :tq], jnp.zeros_like(b[:, :tq])], axis=1))
    prev = toeplitz(jnp.concatenate([b[:, tq:], b[:, :tq]], axis=1))
    return diag, prev


def kernel(x, positions, rel_bias, norm_attn, w_in, q_norm, w_uq, kv_norm, w_ukv, diff_lambda, diff_subln,
           w_out, norm_mlp, w_up, w_down, norm_final):
    b, s, d = x.shape
    depth = w_in.shape[0]
    assert s % TQ == 0
    nk = s // TQ
    cos_r, sin_r, cos_c, sin_c = _rope_tables(positions, TQ)
    bias_diag, bias_prev = _bias_tiles(rel_bias.astype(jnp.float32), TQ)
    row2 = lambda v: v.reshape(1, -1).astype(jnp.float32)
    x2d = x.reshape(b * s, d)
    for l in range(depth):
        w1, w1t = _prep_in_proj(w_in[l])
        wuqt, wuk, wuvt = _prep_mla_up(w_uq[l], w_ukv[l])
        qmt, km, vmt, qdt, kd, vdt, qit, ki, wit, qct, kc, vct = _proj_call(
            x2d, row2(norm_attn[l]), w1, w1t, row2(q_norm[l]), wuqt, row2(kv_norm[l]), wuk, wuvt,
            cos_r, sin_r, cos_c, sin_c)
        r3 = lambda a: a.reshape(b, s, a.shape[-1])
        r4 = lambda a: a.reshape(b, nk, a.shape[-2], a.shape[-1])
        o_a = _mla_call(r4(qmt), r3(km), r4(vmt))
        o_b = _dsa_call(r4(qdt), r4(qit), r4(wit), r3(ki), r3(kd), r4(vdt), bias_diag[:H_DSA], bias_prev[:H_DSA])
        lam_init = 0.8 - 0.6 * math.exp(-0.3 * l)
        lp = diff_lambda[l].astype(jnp.float32)
        lam = (jnp.exp(jnp.sum(lp[0] * lp[1])) - jnp.exp(jnp.sum(lp[2] * lp[3])) + lam_init).reshape(1, 1)
        o_c = _diff_call(r4(qct), r3(kc), r4(vct), bias_diag[H_DSA:], bias_prev[H_DSA:], lam,
                         diff_subln[l].reshape(-1, 1).astype(jnp.float32), lam_init)
        f2 = lambda a: a.reshape(b * s, a.shape[-1])
        x2d = _out_mlp_call(x2d, f2(o_a), f2(o_b), f2(o_c), w_out[l].astype(MXU_DTYPE), row2(norm_mlp[l]),
                            w_up[l].astype(MXU_DTYPE), w_down[l].astype(MXU_DTYPE), row2(norm_final), l == depth - 1)
    return x2d.reshape(b, s, d)
```

```python
import functools
import math

import jax
import jax.numpy as jnp
import numpy as np
from jax import lax
from jax.experimental import pallas as pl
from jax.experimental.pallas import tpu as pltpu

D_MODEL = 1024
H_MLA, MLA_NOPE, MLA_ROPE, MLA_V = 8, 64, 32, 64
Q_LORA, KV_LORA = 384, 256
H_DSA, DSA_DIM, H_IDX, D_IDX, TOPK_MAX = 4, 64, 8, 32, 256
H_DIFF, DIFF_QK = 4, 32
DIFF_V = 2 * DIFF_QK
D_FF = 4 * D_MODEL
N_BUCKETS, MAX_DISTANCE = 32, 128
ROPE_BASE = 10000.0
EPS = 1e-6
NEG = -1e30
LOG2E = math.log2(math.e)

_SPLITS = (Q_LORA, KV_LORA, MLA_ROPE, H_DSA * DSA_DIM, DSA_DIM, DSA_DIM, H_IDX * D_IDX, D_IDX, H_IDX,
           H_DIFF * 2 * DIFF_QK, H_DIFF * 2 * DIFF_QK, H_DIFF * DIFF_V)
_OFF = tuple(int(o) for o in np.concatenate([[0], np.cumsum(_SPLITS)]))

LANES = 128
SUBLANES = 8
MXU_DTYPE = jnp.bfloat16
VMEM_LIMIT = 56 * 1024 * 1024

C_CQ = 0
C_CKV = C_CQ + Q_LORA
C_KR = C_CKV + KV_LORA
C_KRR = C_KR + LANES
C_KB = C_KRR + LANES
C_KI = C_KB + LANES
C_KC = C_KI + LANES
N1 = C_KC + H_DIFF * 2 * DIFF_QK
R_QB = 0
R_QI = R_QB + H_DSA * LANES
R_QC = R_QI + H_IDX * D_IDX
R_VB = R_QC + H_DIFF * 2 * DIFF_QK
V_ROWS = 80
R_VC = R_VB + V_ROWS
R_WI = R_VC + H_DIFF * V_ROWS
WI_ROWS = 16
N1T = R_WI + WI_ROWS

ONES_ROW = 64

TQ = 256
TM_PROJ = 512
MLA_HEADS_PER_STEP = 8
TM_MLP = 512
FF_CHUNK = 1024

_NT = (((1,), (1,)), ((), ()))


def _dot(a, b):
    return jnp.dot(a, b, preferred_element_type=jnp.float32)


def _dot_nt(a, b):
    return lax.dot_general(a, b, _NT, preferred_element_type=jnp.float32)


def _ones_rows(rows):
    r = lax.broadcasted_iota(jnp.int32, (rows, 1), 0)
    return jnp.where(r % V_ROWS == ONES_ROW, 1.0, 0.0).astype(jnp.float32)


def _proj_kernel(x_ref, g_ref, w1_ref, w1t_ref, qn_ref, wuqt_ref, kvn_ref, wuk_ref, wuvt_ref,
                 c_ref, s_ref, ct_ref, st_ref,
                 qmt_ref, km_ref, vmt_ref, qdt_ref, kd_ref, vdt_ref, qit_ref, ki_ref, wit_ref, qct_ref, kc_ref, vct_ref):
    x = x_ref[...]
    h = x * lax.rsqrt(jnp.mean(x * x, axis=-1, keepdims=True) + EPS)
    h = (h * g_ref[...]).astype(MXU_DTYPE)

    def seg(a, b):
        return _dot(h, w1_ref[:, a:b])

    def seg_t(a, b):
        return _dot_nt(w1t_ref[a:b, :], h)

    n_sub = ct_ref.shape[0]
    tq = ct_ref.shape[2]

    def put_t(ref, val, rows=slice(None)):
        for j in range(n_sub):
            ref[j, rows, :] = val[:, j * tq:(j + 1) * tq].astype(ref.dtype)

    c_q = seg(C_CQ, C_CKV)
    c_q = c_q * lax.rsqrt(jnp.mean(c_q * c_q, axis=-1, keepdims=True) + EPS)
    c_q = (c_q * qn_ref[...]).astype(MXU_DTYPE)
    qa = _dot_nt(wuqt_ref[...], c_q)
    half = MLA_ROPE // 2
    rope = slice(MLA_NOPE, MLA_NOPE + half)
    cos_h = jnp.concatenate([ct_ref[j, rope, :] for j in range(n_sub)], axis=1)
    sin_h = jnp.concatenate([st_ref[j, rope, :] for j in range(n_sub)], axis=1)
    for hh in range(H_MLA):
        b0 = hh * LANES
        r1 = slice(b0 + MLA_NOPE, b0 + MLA_NOPE + half)
        r2 = slice(b0 + MLA_NOPE + half, b0 + MLA_NOPE + MLA_ROPE)
        rest = slice(b0 + MLA_NOPE + MLA_ROPE, b0 + LANES)
        x1, x2 = qa[r1, :], qa[r2, :]
        put_t(qmt_ref, qa[b0:b0 + MLA_NOPE, :], slice(b0, b0 + MLA_NOPE))
        put_t(qmt_ref, x1 * cos_h - x2 * sin_h, r1)
        put_t(qmt_ref, x2 * cos_h + x1 * sin_h, r2)
        put_t(qmt_ref, qa[rest, :], rest)

    c_kv = seg(C_CKV, C_KR)
    c_kv = c_kv * lax.rsqrt(jnp.mean(c_kv * c_kv, axis=-1, keepdims=True) + EPS)
    c_kv = (c_kv * kvn_ref[...]).astype(MXU_DTYPE)
    k_rope = seg(C_KR, C_KRR) * c_ref[...] + seg(C_KRR, C_KB) * s_ref[...]
    kk = _dot(c_kv, wuk_ref[...])
    for hh in range(H_MLA):
        sl = slice(hh * LANES, (hh + 1) * LANES)
        km_ref[:, sl] = (kk[:, sl] + k_rope).astype(km_ref.dtype)
    put_t(vmt_ref, _dot_nt(wuvt_ref[...], c_kv) + _ones_rows(H_MLA * V_ROWS))

    kd_ref[...] = seg(C_KB, C_KI).astype(kd_ref.dtype)
    ki_ref[...] = seg(C_KI, C_KC).astype(ki_ref.dtype)
    kc_ref[...] = seg(C_KC, N1).astype(kc_ref.dtype)
    put_t(qdt_ref, seg_t(R_QB, R_QI))
    put_t(qit_ref, seg_t(R_QI, R_QC))
    put_t(qct_ref, seg_t(R_QC, R_VB))
    put_t(vdt_ref, seg_t(R_VB, R_VC) + _ones_rows(V_ROWS))
    put_t(vct_ref, seg_t(R_VC, R_WI) + _ones_rows(H_DIFF * V_ROWS))
    put_t(wit_ref, seg_t(R_WI, R_WI + WI_ROWS)[:H_IDX, :])


def _proj_call(x2d, g, w1, w1t, qn, wuqt, kvn, wuk, wuvt, cos_r, sin_r, cos_c, sin_c):
    t = x2d.shape[0]
    tm, tq = TM_PROJ, TQ
    assert t % tm == 0 and tm % tq == 0
    n = t // tq
    row = lambda w: pl.BlockSpec((tm, w), lambda i: (i, 0))
    colmajor = lambda r: pl.BlockSpec((tm // tq, r, tq), lambda i: (i, 0, 0))
    full = lambda a: pl.BlockSpec(a.shape, lambda i: (0, 0))
    bf = MXU_DTYPE
    width_c = 2 * H_DIFF * DIFF_QK
    outs = [
        (colmajor(H_MLA * LANES), (n, H_MLA * LANES, tq), bf),
        (row(H_MLA * LANES), (t, H_MLA * LANES), bf),
        (colmajor(H_MLA * V_ROWS), (n, H_MLA * V_ROWS, tq), bf),
        (colmajor(H_DSA * LANES), (n, H_DSA * LANES, tq), bf),
        (row(LANES), (t, LANES), bf),
        (colmajor(V_ROWS), (n, V_ROWS, tq), bf),
        (colmajor(H_IDX * D_IDX), (n, H_IDX * D_IDX, tq), bf),
        (row(LANES), (t, LANES), bf),
        (colmajor(H_IDX), (n, H_IDX, tq), jnp.float32),
        (colmajor(width_c), (n, width_c, tq), bf),
        (row(width_c), (t, width_c), bf),
        (colmajor(H_DIFF * V_ROWS), (n, H_DIFF * V_ROWS, tq), bf),
    ]
    return pl.pallas_call(
        _proj_kernel,
        grid=(t // tm,),
        in_specs=[row(D_MODEL), full(g), full(w1), full(w1t), full(qn), full(wuqt), full(kvn), full(wuk),
                  full(wuvt), row(LANES), row(LANES), colmajor(LANES), colmajor(LANES)],
        out_specs=[o[0] for o in outs],
        out_shape=[jax.ShapeDtypeStruct(o[1], o[2]) for o in outs],
        compiler_params=pltpu.CompilerParams(dimension_semantics=("arbitrary",), vmem_limit_bytes=VMEM_LIMIT),
        name="proj",
    )(x2d, g, w1, w1t, qn, wuqt, kvn, wuk, wuvt, cos_r, sin_r, cos_c, sin_c)


def _flash_update(s, m_ref, acc_ref, v_t):
    m_prev = m_ref[...]
    m_new = jnp.maximum(m_prev, jnp.max(s, axis=0, keepdims=True))
    alpha = jnp.exp2(m_prev - m_new)
    p = jnp.exp2(s - m_new).astype(MXU_DTYPE)
    acc_ref[...] = alpha * acc_ref[...] + _dot(v_t, p)
    m_ref[...] = m_new


def _normalized(acc, width):
    return acc[:width, :] / acc[ONES_ROW:ONES_ROW + 1, :]


def _causal_tile(tq):
    key = lax.broadcasted_iota(jnp.int32, (tq, tq), 0)
    qry = lax.broadcasted_iota(jnp.int32, (tq, tq), 1)
    return key <= qry


def _pipelined_key_loop(i, n_heads, produce, consume, prepare=None):
    prepare = prepare or (lambda c, kind: None)

    def step(c, slot, kind, nxt):
        ctx = prepare(c, kind)
        for hh in range(n_heads):
            consume(c, slot, hh, kind, ctx)
            if nxt:
                produce(c + 1, 1 - slot, hh)

    for hh in range(n_heads):
        produce(0, 0, hh)
    n_far_pairs = jnp.maximum(i - 1, 0) // 2
    n_far_quads = n_far_pairs // 2

    def quad(q, carry):
        for r in range(4):
            step(4 * q + r, r % 2, "far", True)
        return carry

    def body(p, carry):
        step(2 * p, 0, "far", True)
        step(2 * p + 1, 1, "far", True)
        return carry

    lax.fori_loop(0, n_far_quads, quad, 0)
    lax.fori_loop(2 * n_far_quads, n_far_pairs, body, 0)
    c0 = 2 * n_far_pairs
    tail = i + 1 - c0

    @pl.when(tail == 3)
    def _():
        step(c0, 0, "far", True)
        step(c0 + 1, 1, "prev", True)
        step(c0 + 2, 0, "diag", False)

    @pl.when(tail == 2)
    def _():
        step(c0, 0, "prev", True)
        step(c0 + 1, 1, "diag", False)

    @pl.when(tail == 1)
    def _():
        step(c0, 0, "diag", False)


def _mla_kernel(qt_ref, k_ref, vt_ref, o_ref, s_ref, m_ref, acc_ref):
    i = pl.program_id(2)
    tq = qt_ref.shape[-1]
    nh = MLA_HEADS_PER_STEP
    m_ref[...] = jnp.full(m_ref.shape, NEG, jnp.float32)
    acc_ref[...] = jnp.zeros(acc_ref.shape, jnp.float32)
    causal = _causal_tile(tq)

    def produce(c, slot, hh):
        sl = slice(hh * LANES, (hh + 1) * LANES)
        kb = k_ref[0, pl.ds(pl.multiple_of(c * tq, tq), tq), sl]
        s_ref[slot, hh] = _dot(kb, qt_ref[0, 0, sl, :])

    def consume(c, slot, hh, kind, ctx):
        s = s_ref[slot, hh]
        if kind == "diag":
            s = jnp.where(causal, s, NEG)
        _flash_update(s, m_ref.at[hh], acc_ref.at[hh], vt_ref[0, c, hh * V_ROWS:(hh + 1) * V_ROWS, :])

    _pipelined_key_loop(i, nh, produce, consume)
    o_ref[0] = jnp.concatenate([_normalized(acc_ref[hh], MLA_V).T for hh in range(nh)], axis=-1).astype(o_ref.dtype)


def _mla_call(qmt, km, vmt):
    b, nk, _, tq = qmt.shape
    s = nk * tq
    nh = MLA_HEADS_PER_STEP
    return pl.pallas_call(
        _mla_kernel,
        grid=(b, H_MLA // nh, nk),
        in_specs=[pl.BlockSpec((1, 1, nh * LANES, tq), lambda bb, hp, i: (bb, i, hp, 0)),
                  pl.BlockSpec((1, s, nh * LANES), lambda bb, hp, i: (bb, 0, hp)),
                  pl.BlockSpec((1, nk, nh * V_ROWS, tq), lambda bb, hp, i: (bb, 0, hp, 0))],
        out_specs=pl.BlockSpec((1, tq, nh * MLA_V), lambda bb, hp, i: (bb, i, hp)),
        out_shape=jax.ShapeDtypeStruct((b, s, H_MLA * MLA_V), MXU_DTYPE),
        scratch_shapes=[pltpu.VMEM((2, nh, tq, tq), jnp.float32),
                        pltpu.VMEM((nh, 1, tq), jnp.float32), pltpu.VMEM((nh, V_ROWS, tq), jnp.float32)],
        compiler_params=pltpu.CompilerParams(dimension_semantics=("arbitrary",) * 3, vmem_limit_bytes=VMEM_LIMIT),
        name="mla",
    )(qmt, km, vmt)


def _ordered_to_float(k):
    bits = k ^ ((k >> 31) & jnp.int32(0x7FFFFFFF))
    return lax.bitcast_convert_type(bits, jnp.float32)


def _coarse(x):
    bits = lax.bitcast_convert_type(x, jnp.int32) & jnp.int32(-65536)
    return lax.bitcast_convert_type(bits, jnp.float32).astype(jnp.bfloat16)


def _dsa_kernel(n_sel, qdt_ref, qit_ref, wit_ref, ki_ref, kd_ref, vdt_ref, bd_ref, bp_ref, o_ref,
                sc_ref, sc16_ref, qim_ref, thr_ref, s_ref, m_ref, acc_ref):
    i = pl.program_id(1)
    tq = qdt_ref.shape[-1]
    n_chunks = i + 1
    causal = _causal_tile(tq)
    int_min = jnp.int32(-2 ** 31)

    pad = jnp.zeros((LANES - D_IDX, tq), MXU_DTYPE)
    for hh in range(H_IDX):
        qim_ref[hh] = jnp.concatenate([qit_ref[0, 0, hh * D_IDX:(hh + 1) * D_IDX, :], pad], axis=0)

    def score_chunk(jc, diag):
        kc = ki_ref[0, pl.ds(pl.multiple_of(jc * tq, tq), tq), :]
        w = wit_ref[0, 0]
        acc = jnp.zeros((tq, tq), jnp.float32)
        for hh in range(H_IDX):
            acc = acc + jnp.maximum(_dot(kc, qim_ref[hh]), 0.0) * w[hh:hh + 1, :]
        if diag:
            acc = jnp.where(causal, acc, NEG)
        sc_ref[jc] = acc
        sc16_ref[jc] = acc.astype(jnp.bfloat16)

    def score_quad(jq, carry):
        for r in range(4):
            score_chunk(4 * jq + r, False)
        return carry

    lax.fori_loop(0, i // 4, score_quad, 0)
    rest = i % 4

    @pl.when(rest >= 2)
    def _():
        score_chunk(i - rest, False)
        score_chunk(i - rest + 1, False)

    @pl.when(rest % 2 == 1)
    def _():
        score_chunk(i - 1, False)

    score_chunk(i, True)

    @pl.when(n_chunks % 2 == 1)
    def _():
        sc_ref[n_chunks] = jnp.full((tq, tq), -jnp.inf, jnp.float32)
        sc16_ref[n_chunks] = jnp.full((tq, tq), -jnp.inf, jnp.bfloat16)

    n_pairs = (n_chunks + 1) // 2

    def count_where(hits):
        def fold(jc):
            return jnp.sum(hits(sc_ref[jc], jc).reshape(tq // SUBLANES, SUBLANES, tq), axis=0)

        def body(jp, acc):
            return acc + fold(2 * jp) + fold(2 * jp + 1)
        acc = lax.fori_loop(0, n_pairs, body, jnp.zeros((SUBLANES, tq), jnp.float32))
        return jnp.sum(acc, axis=0, keepdims=True)

    def count_ge(thr):
        return count_where(lambda sc, jc: jnp.where(sc >= thr, 1.0, 0.0))

    pack = 2 * SUBLANES
    one16 = jnp.ones((tq, tq), jnp.bfloat16)
    zero16 = jnp.zeros((tq, tq), jnp.bfloat16)

    def count_ge_coarse(thr):
        thr16 = _coarse(thr)

        def fold(jc):
            hit = jnp.where(sc16_ref[jc] >= thr16, one16, zero16).reshape(tq // pack, pack, tq)
            parts = [hit[r] for r in range(tq // pack)]
            while len(parts) > 1:
                parts = [parts[r] + parts[r + 1] for r in range(0, len(parts), 2)]
            return parts[0].astype(jnp.float32)

        def body(jp, acc):
            return acc + fold(2 * jp) + fold(2 * jp + 1)
        acc = lax.fori_loop(0, n_pairs, body, jnp.zeros((pack, tq), jnp.float32))
        return jnp.sum(acc, axis=0, keepdims=True)

    c0 = count_ge_coarse(jnp.zeros((1, tq), jnp.float32))
    start = (jnp.where(c0 >= n_sel, jnp.int32(0), int_min), jnp.where(c0 >= n_sel, c0, jnp.float32(2 * n_sel)))

    def bit_body(count, top):
        def body(t, carry):
            k, c_k = carry
            cand = k + lax.shift_left(jnp.int32(1), jnp.int32(top) - t)
            c = count(_ordered_to_float(cand))
            take = c >= n_sel
            return jnp.where(take, cand, k), jnp.where(take, c, c_k)
        return body

    k16, _ = lax.fori_loop(0, 15, bit_body(count_ge_coarse, 30), start)
    g_key = k16 + jnp.where(k16 < 0, jnp.int32(0xFFFF), jnp.int32(0))
    base = (g_key - jnp.int32(1 << 16), jnp.full((1, tq), 2.0 * n_sel, jnp.float32))
    kth, c_kth = lax.fori_loop(0, 17, bit_body(count_ge, 16), base)
    thr_ref[...] = _ordered_to_float(kth)

    @pl.when(jnp.max(c_kth) > n_sel)
    def _():
        thr_ = thr_ref[...]
        excess = c_kth - n_sel
        key = lax.broadcasted_iota(jnp.int32, (tq, tq), 0)
        col = lax.broadcasted_iota(jnp.int32, (tq, tq), 1)
        tri = jnp.where(col >= key, 1.0, 0.0).astype(MXU_DTYPE)

        def drop_chunk(jc, after):
            sc = sc_ref[jc]
            tie = jnp.where(sc == thr_, 1.0, 0.0)
            rank = _dot(tri, tie.astype(MXU_DTYPE)) + after
            sc_ref[jc] = jnp.where(rank <= excess, jnp.where(sc == thr_, -jnp.inf, sc), sc)
            return rank[0:1, :]

        def drop_pair(j, after):
            jp = n_pairs - 1 - j
            return drop_chunk(2 * jp, drop_chunk(2 * jp + 1, after))

        lax.fori_loop(0, n_pairs, drop_pair, jnp.zeros((1, tq), jnp.float32))

    thr = thr_ref[...]

    m_ref[...] = jnp.full(m_ref.shape, NEG, jnp.float32)
    acc_ref[...] = jnp.zeros(acc_ref.shape, jnp.float32)

    def produce(c, slot, hh):
        kb = kd_ref[0, pl.ds(pl.multiple_of(c * tq, tq), tq), :]
        s_ref[slot, hh] = _dot(kb, qdt_ref[0, 0, hh * LANES:(hh + 1) * LANES, :])

    def prepare(c, kind):
        return sc_ref[c] >= thr

    def consume(c, slot, hh, kind, sel):
        s = s_ref[slot, hh]
        if kind == "diag":
            s = jnp.where(causal, s + bd_ref[hh], NEG)
        elif kind == "prev":
            s = s + bp_ref[hh]
        s = jnp.where(sel, s, NEG)
        _flash_update(s, m_ref.at[hh], acc_ref.at[hh], vdt_ref[0, c])

    _pipelined_key_loop(i, H_DSA, produce, consume, prepare)
    o_ref[0] = jnp.concatenate([_normalized(acc_ref[hh], DSA_DIM).T for hh in range(H_DSA)], axis=-1).astype(o_ref.dtype)


def _dsa_call(qdt, qit, wit, ki, kd, vdt, bias_diag, bias_prev):
    b, nk, _, tq = qdt.shape
    s = nk * tq
    n_sel = min(TOPK_MAX, s // 4)
    assert tq >= n_sel and tq + 1 >= MAX_DISTANCE
    qblk = lambda r: pl.BlockSpec((1, 1, r, tq), lambda bb, i: (bb, i, 0, 0))
    kblk = lambda w: pl.BlockSpec((1, s, w), lambda bb, i: (bb, 0, 0))
    bblk = pl.BlockSpec((H_DSA, tq, tq), lambda bb, i: (0, 0, 0))
    return pl.pallas_call(
        functools.partial(_dsa_kernel, n_sel),
        grid=(b, nk),
        in_specs=[qblk(H_DSA * LANES), qblk(H_IDX * D_IDX), qblk(H_IDX),
                  kblk(LANES), kblk(LANES),
                  pl.BlockSpec((1, nk, V_ROWS, tq), lambda bb, i: (bb, 0, 0, 0)), bblk, bblk],
        out_specs=pl.BlockSpec((1, tq, H_DSA * DSA_DIM), lambda bb, i: (bb, i, 0)),
        out_shape=jax.ShapeDtypeStruct((b, s, H_DSA * DSA_DIM), MXU_DTYPE),
        scratch_shapes=[pltpu.VMEM((nk + 1, tq, tq), jnp.float32),
                        pltpu.VMEM((nk + 1, tq, tq), jnp.bfloat16),
                        pltpu.VMEM((H_IDX, LANES, tq), MXU_DTYPE),
                        pltpu.VMEM((1, tq), jnp.float32),
                        pltpu.VMEM((2, H_DSA, tq, tq), jnp.float32),
                        pltpu.VMEM((H_DSA, 1, tq), jnp.float32),
                        pltpu.VMEM((H_DSA, V_ROWS, tq), jnp.float32)],
        compiler_params=pltpu.CompilerParams(dimension_semantics=("arbitrary",) * 2, vmem_limit_bytes=VMEM_LIMIT),
        name="dsa",
    )(qdt, qit, wit, ki, kd, vdt, bias_diag, bias_prev)


def _diff_kernel(lam_scale, qct_ref, kc_ref, vct_ref, bd_ref, bp_ref, lam_ref, g_ref, o_ref,
                 qm_ref, s_ref, m_ref, acc_ref):
    i = pl.program_id(1)
    tq = qct_ref.shape[-1]
    n_maps = 2 * H_DIFF
    causal = _causal_tile(tq)

    q = qct_ref[0, 0]
    row_map = lax.broadcasted_iota(jnp.int32, q.shape, 0) // DIFF_QK
    for mm in range(n_maps):
        qm_ref[mm] = jnp.where(row_map == mm, q, jnp.zeros_like(q))
    m_ref[...] = jnp.full(m_ref.shape, NEG, jnp.float32)
    acc_ref[...] = jnp.zeros(acc_ref.shape, jnp.float32)

    def produce(c, slot, mm):
        kb = kc_ref[0, pl.ds(pl.multiple_of(c * tq, tq), tq), :]
        s_ref[slot, mm] = _dot(kb, qm_ref[mm])

    def consume(c, slot, mm, kind, ctx):
        hh = mm // 2
        s = s_ref[slot, mm]
        if kind == "diag":
            s = jnp.where(causal, s + bd_ref[hh], NEG)
        elif kind == "prev":
            s = s + bp_ref[hh]
        _flash_update(s, m_ref.at[mm], acc_ref.at[mm], vct_ref[0, c, hh * V_ROWS:(hh + 1) * V_ROWS, :])

    _pipelined_key_loop(i, n_maps, produce, consume)

    lam = lam_ref[...]
    outs = []
    for hh in range(H_DIFF):
        o = _normalized(acc_ref[2 * hh], DIFF_V) - lam * _normalized(acc_ref[2 * hh + 1], DIFF_V)
        o = o * lax.rsqrt(jnp.mean(o * o, axis=0, keepdims=True) + EPS)
        outs.append((o * g_ref[...] * lam_scale).T)
    o_ref[0] = jnp.concatenate(outs, axis=-1).astype(o_ref.dtype)


def _diff_call(qct, kc, vct, bias_diag, bias_prev, lam, subln, lam_init):
    b, nk, width, tq = qct.shape
    s = nk * tq
    assert tq + 1 >= MAX_DISTANCE
    bblk = pl.BlockSpec((H_DIFF, tq, tq), lambda bb, i: (0, 0, 0))
    return pl.pallas_call(
        functools.partial(_diff_kernel, 1.0 - lam_init),
        grid=(b, nk),
        in_specs=[pl.BlockSpec((1, 1, width, tq), lambda bb, i: (bb, i, 0, 0)),
                  pl.BlockSpec((1, s, width), lambda bb, i: (bb, 0, 0)),
                  pl.BlockSpec((1, nk, H_DIFF * V_ROWS, tq), lambda bb, i: (bb, 0, 0, 0)), bblk, bblk,
                  pl.BlockSpec((1, 1), lambda bb, i: (0, 0)), pl.BlockSpec((DIFF_V, 1), lambda bb, i: (0, 0))],
        out_specs=pl.BlockSpec((1, tq, H_DIFF * DIFF_V), lambda bb, i: (bb, i, 0)),
        out_shape=jax.ShapeDtypeStruct((b, s, H_DIFF * DIFF_V), MXU_DTYPE),
        scratch_shapes=[pltpu.VMEM((2 * H_DIFF, width, tq), MXU_DTYPE),
                        pltpu.VMEM((2, 2 * H_DIFF, tq, tq), jnp.float32),
                        pltpu.VMEM((2 * H_DIFF, 1, tq), jnp.float32),
                        pltpu.VMEM((2 * H_DIFF, V_ROWS, tq), jnp.float32)],
        compiler_params=pltpu.CompilerParams(dimension_semantics=("arbitrary",) * 2, vmem_limit_bytes=VMEM_LIMIT),
        name="diff",
    )(qct, kc, vct, bias_diag, bias_prev, lam, subln)


def _weight_copies(wup_hbm, wdn_hbm, wup_ref, wdn_ref, sem):
    return (pltpu.make_async_copy(wup_hbm, wup_ref, sem.at[0]), pltpu.make_async_copy(wdn_hbm, wdn_ref, sem.at[1]))


def _out_mlp_kernel(final, x_ref, oa_ref, ob_ref, oc_ref, wo_ref, g_ref, wup_hbm, wdn_hbm, gf_ref, y_ref,
                    wup_ref, wdn_ref, sem):
    @pl.when(pl.program_id(0) == 0)
    def _():
        copies = _weight_copies(wup_hbm, wdn_hbm, wup_ref, wdn_ref, sem)
        for c in copies:
            c.start()
        for c in copies:
            c.wait()

    mix = jnp.concatenate([oa_ref[...], ob_ref[...], oc_ref[...]], axis=-1)
    x = x_ref[...] + _dot(mix, wo_ref[...])
    h = x * lax.rsqrt(jnp.mean(x * x, axis=-1, keepdims=True) + EPS)
    h = (h * g_ref[...]).astype(MXU_DTYPE)
    y = x
    for c in range(D_FF // FF_CHUNK):
        cs = slice(c * FF_CHUNK, (c + 1) * FF_CHUNK)
        u = jnp.maximum(_dot(h, wup_ref[:, cs]), 0.0)
        y = y + _dot((u * u).astype(MXU_DTYPE), wdn_ref[cs, :])
    if final:
        y = y * lax.rsqrt(jnp.mean(y * y, axis=-1, keepdims=True) + EPS) * gf_ref[...]
    y_ref[...] = y


def _out_mlp_call(x2d, oa, ob, oc, wo, g, wup, wdn, gf, final):
    t = x2d.shape[0]
    tm = TM_MLP
    assert t % tm == 0
    row = lambda a: pl.BlockSpec((tm, a.shape[1]), lambda i: (i, 0))
    full = lambda a: pl.BlockSpec(a.shape, lambda i: (0, 0))
    hbm = pl.BlockSpec(memory_space=pl.ANY)
    return pl.pallas_call(
        functools.partial(_out_mlp_kernel, final),
        grid=(t // tm,),
        in_specs=[row(x2d), row(oa), row(ob), row(oc), full(wo), full(g), hbm, hbm, full(gf)],
        out_specs=row(x2d),
        out_shape=jax.ShapeDtypeStruct(x2d.shape, jnp.float32),
        scratch_shapes=[pltpu.VMEM(wup.shape, wup.dtype), pltpu.VMEM(wdn.shape, wdn.dtype),
                        pltpu.SemaphoreType.DMA((2,))],
        compiler_params=pltpu.CompilerParams(dimension_semantics=("arbitrary",), vmem_limit_bytes=VMEM_LIMIT),
        name="out_mlp",
    )(x2d, oa, ob, oc, wo, g, wup, wdn, gf)


def _place(dst_cols, pieces):
    rows = pieces[0][1].shape[0]
    out = jnp.zeros((rows, dst_cols), jnp.float32)
    for off, a in pieces:
        out = lax.dynamic_update_slice(out, a.astype(jnp.float32), (0, off))
    return out


def _rot_half_cols(w):
    half = w.shape[1] // 2
    return jnp.concatenate([-w[:, half:], w[:, :half]], axis=1)


def _prep_in_proj(w_in):
    o = _OFF
    col = lambda k: w_in[:, o[k]:o[k + 1]]
    kr = col(2)
    s_dsa = (DSA_DIM ** -0.5) * LOG2E
    s_diff = (DIFF_QK ** -0.5) * LOG2E
    s_idx = (D_IDX ** -0.5) * (H_IDX ** -0.5)
    pieces = [(C_CQ, col(0)), (C_CKV, col(1)),
              (C_KR + MLA_NOPE, kr), (C_KRR + MLA_NOPE, _rot_half_cols(kr)),
              (C_KB, col(4)), (C_KC, col(10))]
    pieces.append((C_KI, col(7)))
    t_pieces = [(R_QI, col(6)), (R_QC, col(9) * s_diff), (R_VB, col(5)), (R_WI, col(8) * s_idx)]
    qb, vc = col(3), col(11)
    for hh in range(H_DSA):
        t_pieces.append((R_QB + hh * LANES, qb[:, hh * DSA_DIM:(hh + 1) * DSA_DIM] * s_dsa))
    for hh in range(H_DIFF):
        t_pieces.append((R_VC + hh * V_ROWS, vc[:, hh * DIFF_V:(hh + 1) * DIFF_V]))
    return _place(N1, pieces).astype(MXU_DTYPE), _place(N1T, t_pieces).T.astype(MXU_DTYPE)


def _prep_mla_up(w_uq, w_ukv):
    dq = MLA_NOPE + MLA_ROPE
    s_mla = (dq ** -0.5) * LOG2E
    q_p, k_p, v_p = [], [], []
    for hh in range(H_MLA):
        wq = w_uq[:, hh * dq:(hh + 1) * dq] * s_mla
        q_p.append((hh * LANES, wq))
        wkv = w_ukv[:, hh * (MLA_NOPE + MLA_V):(hh + 1) * (MLA_NOPE + MLA_V)]
        k_p.append((hh * LANES, wkv[:, :MLA_NOPE]))
        v_p.append((hh * V_ROWS, wkv[:, MLA_NOPE:]))
    width = H_MLA * LANES
    wuq, wuk = (_place(width, p) for p in (q_p, k_p))
    wuv = _place(H_MLA * V_ROWS, v_p)
    return wuq.T.astype(MXU_DTYPE), wuk.astype(MXU_DTYPE), wuv.T.astype(MXU_DTYPE)


def _rope_tables(positions, tm):
    half = MLA_ROPE // 2
    freqs = ROPE_BASE ** (-jnp.arange(half, dtype=jnp.float32) / half)
    ang = positions.astype(jnp.float32)[..., None] * freqs
    cos, sin = jnp.cos(ang), jnp.sin(ang)
    lead = cos.shape[:-1]
    ones = jnp.ones(lead + (MLA_NOPE,), jnp.float32)
    zeros = jnp.zeros(lead + (LANES - MLA_NOPE - MLA_ROPE,), jnp.float32)
    cos_r = jnp.concatenate([ones, cos, cos, zeros], axis=-1).reshape(-1, LANES)
    sin_r = jnp.concatenate([0.0 * ones, sin, sin, zeros], axis=-1).reshape(-1, LANES)
    to_cols = lambda a: jnp.transpose(a.reshape(-1, tm, LANES), (0, 2, 1))
    return cos_r, sin_r, to_cols(cos_r), to_cols(sin_r)


def _t5_bucket(rel):
    n = jnp.maximum(rel, 0)
    max_exact = N_BUCKETS // 2
    nf = jnp.maximum(n, 1).astype(jnp.float32)
    large = max_exact + (jnp.log(nf / max_exact) / math.log(MAX_DISTANCE / max_exact)
                         * (N_BUCKETS - max_exact)).astype(jnp.int32)
    large = jnp.minimum(large, N_BUCKETS - 1)
    return jnp.where(n < max_exact, n, large)


def _bias_tiles(rel_bias, tq):
    n_heads = rel_bias.shape[1]
    dist = jnp.arange(2 * tq, dtype=jnp.int32)
    b = ((rel_bias[_t5_bucket(dist)] - rel_bias[N_BUCKETS - 1]) * LOG2E).T

    def toeplitz(u):
        flat = jnp.tile(u, (1, tq))[:, :tq * (2 * tq - 1)]
        return flat.reshape(n_heads, tq, 2 * tq - 1)[:, :, :tq]

    diag = toeplitz(jnp.concatenate([b[:, :tq], jnp.zeros_like(b[:, :tq])], axis=1))
    prev = toeplitz(jnp.concatenate([b[:, tq:], b[:, :tq]], axis=1))
    return diag, prev


def kernel(x, positions, rel_bias, norm_attn, w_in, q_norm, w_uq, kv_norm, w_ukv, diff_lambda, diff_subln,
           w_out, norm_mlp, w_up, w_down, norm_final):
    b, s, d = x.shape
    depth = w_in.shape[0]
    assert s % TQ == 0
    nk = s // TQ
    cos_r, sin_r, cos_c, sin_c = _rope_tables(positions, TQ)
    bias_diag, bias_prev = _bias_tiles(rel_bias.astype(jnp.float32), TQ)
    row2 = lambda v: v.reshape(1, -1).astype(jnp.float32)
    x2d = x.reshape(b * s, d)
    for l in range(depth):
        w1, w1t = _prep_in_proj(w_in[l])
        wuqt, wuk, wuvt = _prep_mla_up(w_uq[l], w_ukv[l])
        qmt, km, vmt, qdt, kd, vdt, qit, ki, wit, qct, kc, vct = _proj_call(
            x2d, row2(norm_attn[l]), w1, w1t, row2(q_norm[l]), wuqt, row2(kv_norm[l]), wuk, wuvt,
            cos_r, sin_r, cos_c, sin_c)
        r3 = lambda a: a.reshape(b, s, a.shape[-1])
        r4 = lambda a: a.reshape(b, nk, a.shape[-2], a.shape[-1])
        o_a = _mla_call(r4(qmt), r3(km), r4(vmt))
        o_b = _dsa_call(r4(qdt), r4(qit), r4(wit), r3(ki), r3(kd), r4(vdt), bias_diag[:H_DSA], bias_prev[:H_DSA])
        lam_init = 0.8 - 0.6 * math.exp(-0.3 * l)
        lp = diff_lambda[l].astype(jnp.float32)
        lam = (jnp.exp(jnp.sum(lp[0] * lp[1])) - jnp.exp(jnp.sum(lp[2] * lp[3])) + lam_init).reshape(1, 1)
        o_c = _diff_call(r4(qct), r3(kc), r4(vct), bias_diag[H_DSA:], bias_prev[H_DSA:], lam,
                         diff_subln[l].reshape(-1, 1).astype(jnp.float32), lam_init)
        f2 = lambda a: a.reshape(b * s, a.shape[-1])
        x2d = _out_mlp_call(x2d, f2(o_a), f2(o_b), f2(o_c), w_out[l].astype(MXU_DTYPE), row2(norm_mlp[l]),
                            w_up[l].astype(MXU_DTYPE), w_down[l].astype(MXU_DTYPE), row2(norm_final), l == depth - 1)
    return x2d.reshape(b, s, d)
```

```python
import functools
import math

import jax
import jax.numpy as jnp
import numpy as np
from jax import lax
from jax.experimental import pallas as pl
from jax.experimental.pallas import tpu as pltpu

D_MODEL = 1024
H_MLA, MLA_NOPE, MLA_ROPE, MLA_V = 8, 64, 32, 64
Q_LORA, KV_LORA = 384, 256
H_DSA, DSA_DIM, H_IDX, D_IDX, TOPK_MAX = 4, 64, 8, 32, 256
H_DIFF, DIFF_QK = 4, 32
DIFF_V = 2 * DIFF_QK
D_FF = 4 * D_MODEL
N_BUCKETS, MAX_DISTANCE = 32, 128
ROPE_BASE = 10000.0
EPS = 1e-6
NEG = -1e30
LOG2E = math.log2(math.e)

_SPLITS = (Q_LORA, KV_LORA, MLA_ROPE, H_DSA * DSA_DIM, DSA_DIM, DSA_DIM, H_IDX * D_IDX, D_IDX, H_IDX,
           H_DIFF * 2 * DIFF_QK, H_DIFF * 2 * DIFF_QK, H_DIFF * DIFF_V)
_OFF = tuple(int(o) for o in np.concatenate([[0], np.cumsum(_SPLITS)]))

LANES = 128
SUBLANES = 8
MXU_DTYPE = jnp.bfloat16
VMEM_LIMIT = 56 * 1024 * 1024

C_CQ = 0
C_CKV = C_CQ + Q_LORA
C_KR = C_CKV + KV_LORA
C_KRR = C_KR + LANES
C_KB = C_KRR + LANES
C_KI = C_KB + LANES
C_KC = C_KI + LANES
N1 = C_KC + H_DIFF * 2 * DIFF_QK
R_QB = 0
R_QI = R_QB + H_DSA * LANES
R_QC = R_QI + H_IDX * D_IDX
R_VB = R_QC + H_DIFF * 2 * DIFF_QK
V_ROWS = 80
R_VC = R_VB + V_ROWS
R_WI = R_VC + H_DIFF * V_ROWS
WI_ROWS = 16
N1T = R_WI + WI_ROWS

ONES_ROW = 64

TQ = 256
TM_PROJ = 512
MLA_HEADS_PER_STEP = 8
TM_MLP = 512
FF_CHUNK = 1024

_NT = (((1,), (1,)), ((), ()))


def _dot(a, b):
    return jnp.dot(a, b, preferred_element_type=jnp.float32)


def _dot_nt(a, b):
    return lax.dot_general(a, b, _NT, preferred_element_type=jnp.float32)


def _ones_rows(rows):
    r = lax.broadcasted_iota(jnp.int32, (rows, 1), 0)
    return jnp.where(r % V_ROWS == ONES_ROW, 1.0, 0.0).astype(jnp.float32)


def _proj_kernel(x_ref, g_ref, w1_ref, w1t_ref, qn_ref, wuqt_ref, kvn_ref, wuk_ref, wuvt_ref,
                 c_ref, s_ref, ct_ref, st_ref,
                 qmt_ref, km_ref, vmt_ref, qdt_ref, kd_ref, vdt_ref, qit_ref, ki_ref, wit_ref, qct_ref, kc_ref, vct_ref):
    x = x_ref[...]
    h = x * lax.rsqrt(jnp.mean(x * x, axis=-1, keepdims=True) + EPS)
    h = (h * g_ref[...]).astype(MXU_DTYPE)

    def seg(a, b):
        return _dot(h, w1_ref[:, a:b])

    def seg_t(a, b):
        return _dot_nt(w1t_ref[a:b, :], h)

    n_sub = ct_ref.shape[0]
    tq = ct_ref.shape[2]

    def put_t(ref, val, rows=slice(None)):
        for j in range(n_sub):
            ref[j, rows, :] = val[:, j * tq:(j + 1) * tq].astype(ref.dtype)

    c_q = seg(C_CQ, C_CKV)
    c_q = c_q * lax.rsqrt(jnp.mean(c_q * c_q, axis=-1, keepdims=True) + EPS)
    c_q = (c_q * qn_ref[...]).astype(MXU_DTYPE)
    qa = _dot_nt(wuqt_ref[...], c_q)
    half = MLA_ROPE // 2
    rope = slice(MLA_NOPE, MLA_NOPE + half)
    cos_h = jnp.concatenate([ct_ref[j, rope, :] for j in range(n_sub)], axis=1)
    sin_h = jnp.concatenate([st_ref[j, rope, :] for j in range(n_sub)], axis=1)
    for hh in range(H_MLA):
        b0 = hh * LANES
        r1 = slice(b0 + MLA_NOPE, b0 + MLA_NOPE + half)
        r2 = slice(b0 + MLA_NOPE + half, b0 + MLA_NOPE + MLA_ROPE)
        rest = slice(b0 + MLA_NOPE + MLA_ROPE, b0 + LANES)
        x1, x2 = qa[r1, :], qa[r2, :]
        put_t(qmt_ref, qa[b0:b0 + MLA_NOPE, :], slice(b0, b0 + MLA_NOPE))
        put_t(qmt_ref, x1 * cos_h - x2 * sin_h, r1)
        put_t(qmt_ref, x2 * cos_h + x1 * sin_h, r2)
        put_t(qmt_ref, qa[rest, :], rest)

    c_kv = seg(C_CKV, C_KR)
    c_kv = c_kv * lax.rsqrt(jnp.mean(c_kv * c_kv, axis=-1, keepdims=True) + EPS)
    c_kv = (c_kv * kvn_ref[...]).astype(MXU_DTYPE)
    k_rope = seg(C_KR, C_KRR) * c_ref[...] + seg(C_KRR, C_KB) * s_ref[...]
    kk = _dot(c_kv, wuk_ref[...])
    for hh in range(H_MLA):
        sl = slice(hh * LANES, (hh + 1) * LANES)
        km_ref[:, sl] = (kk[:, sl] + k_rope).astype(km_ref.dtype)
    put_t(vmt_ref, _dot_nt(wuvt_ref[...], c_kv) + _ones_rows(H_MLA * V_ROWS))

    kd_ref[...] = seg(C_KB, C_KI).astype(kd_ref.dtype)
    ki_ref[...] = seg(C_KI, C_KC).astype(ki_ref.dtype)
    kc_ref[...] = seg(C_KC, N1).astype(kc_ref.dtype)
    put_t(qdt_ref, seg_t(R_QB, R_QI))
    put_t(qit_ref, seg_t(R_QI, R_QC))
    put_t(qct_ref, seg_t(R_QC, R_VB))
    put_t(vdt_ref, seg_t(R_VB, R_VC) + _ones_rows(V_ROWS))
    put_t(vct_ref, seg_t(R_VC, R_WI) + _ones_rows(H_DIFF * V_ROWS))
    put_t(wit_ref, seg_t(R_WI, R_WI + WI_ROWS)[:H_IDX, :])


def _proj_call(x2d, g, w1, w1t, qn, wuqt, kvn, wuk, wuvt, cos_r, sin_r, cos_c, sin_c):
    t = x2d.shape[0]
    tm, tq = TM_PROJ, TQ
    assert t % tm == 0 and tm % tq == 0
    n = t // tq
    row = lambda w: pl.BlockSpec((tm, w), lambda i: (i, 0))
    colmajor = lambda r: pl.BlockSpec((tm // tq, r, tq), lambda i: (i, 0, 0))
    full = lambda a: pl.BlockSpec(a.shape, lambda i: (0, 0))
    bf = MXU_DTYPE
    width_c = 2 * H_DIFF * DIFF_QK
    outs = [
        (colmajor(H_MLA * LANES), (n, H_MLA * LANES, tq), bf),
        (row(H_MLA * LANES), (t, H_MLA * LANES), bf),
        (colmajor(H_MLA * V_ROWS), (n, H_MLA * V_ROWS, tq), bf),
        (colmajor(H_DSA * LANES), (n, H_DSA * LANES, tq), bf),
        (row(LANES), (t, LANES), bf),
        (colmajor(V_ROWS), (n, V_ROWS, tq), bf),
        (colmajor(H_IDX * D_IDX), (n, H_IDX * D_IDX, tq), bf),
        (row(LANES), (t, LANES), bf),
        (colmajor(H_IDX), (n, H_IDX, tq), jnp.float32),
        (colmajor(width_c), (n, width_c, tq), bf),
        (row(width_c), (t, width_c), bf),
        (colmajor(H_DIFF * V_ROWS), (n, H_DIFF * V_ROWS, tq), bf),
    ]
    return pl.pallas_call(
        _proj_kernel,
        grid=(t // tm,),
        in_specs=[row(D_MODEL), full(g), full(w1), full(w1t), full(qn), full(wuqt), full(kvn), full(wuk),
                  full(wuvt), row(LANES), row(LANES), colmajor(LANES), colmajor(LANES)],
        out_specs=[o[0] for o in outs],
        out_shape=[jax.ShapeDtypeStruct(o[1], o[2]) for o in outs],
        compiler_params=pltpu.CompilerParams(dimension_semantics=("arbitrary",), vmem_limit_bytes=VMEM_LIMIT),
        name="proj",
    )(x2d, g, w1, w1t, qn, wuqt, kvn, wuk, wuvt, cos_r, sin_r, cos_c, sin_c)


def _flash_update(s, m_ref, acc_ref, v_t):
    m_prev = m_ref[...]
    m_new = jnp.maximum(m_prev, jnp.max(s, axis=0, keepdims=True))
    alpha = jnp.exp2(m_prev - m_new)
    p = jnp.exp2(s - m_new).astype(MXU_DTYPE)
    acc_ref[...] = alpha * acc_ref[...] + _dot(v_t, p)
    m_ref[...] = m_new


def _normalized(acc, width):
    return acc[:width, :] / acc[ONES_ROW:ONES_ROW + 1, :]


def _causal_tile(tq):
    key = lax.broadcasted_iota(jnp.int32, (tq, tq), 0)
    qry = lax.broadcasted_iota(jnp.int32, (tq, tq), 1)
    return key <= qry


def _pipelined_key_loop(i, n_heads, produce, consume, prepare=None):
    prepare = prepare or (lambda c, kind: None)

    def step(c, slot, kind, nxt):
        ctx = prepare(c, kind)
        for hh in range(n_heads):
            consume(c, slot, hh, kind, ctx)
            if nxt:
                produce(c + 1, 1 - slot, hh)

    for hh in range(n_heads):
        produce(0, 0, hh)
    n_far_pairs = jnp.maximum(i - 1, 0) // 2
    n_far_quads = n_far_pairs // 2

    def quad(q, carry):
        for r in range(4):
            step(4 * q + r, r % 2, "far", True)
        return carry

    def body(p, carry):
        step(2 * p, 0, "far", True)
        step(2 * p + 1, 1, "far", True)
        return carry

    lax.fori_loop(0, n_far_quads, quad, 0)
    lax.fori_loop(2 * n_far_quads, n_far_pairs, body, 0)
    c0 = 2 * n_far_pairs
    tail = i + 1 - c0

    @pl.when(tail == 3)
    def _():
        step(c0, 0, "far", True)
        step(c0 + 1, 1, "prev", True)
        step(c0 + 2, 0, "diag", False)

    @pl.when(tail == 2)
    def _():
        step(c0, 0, "prev", True)
        step(c0 + 1, 1, "diag", False)

    @pl.when(tail == 1)
    def _():
        step(c0, 0, "diag", False)


def _mla_kernel(qt_ref, k_ref, vt_ref, o_ref, s_ref, m_ref, acc_ref):
    i = pl.program_id(2)
    tq = qt_ref.shape[-1]
    nh = MLA_HEADS_PER_STEP
    m_ref[...] = jnp.full(m_ref.shape, NEG, jnp.float32)
    acc_ref[...] = jnp.zeros(acc_ref.shape, jnp.float32)
    causal = _causal_tile(tq)

    def produce(c, slot, hh):
        sl = slice(hh * LANES, (hh + 1) * LANES)
        kb = k_ref[0, pl.ds(pl.multiple_of(c * tq, tq), tq), sl]
        s_ref[slot, hh] = _dot(kb, qt_ref[0, 0, sl, :])

    def consume(c, slot, hh, kind, ctx):
        s = s_ref[slot, hh]
        if kind == "diag":
            s = jnp.where(causal, s, NEG)
        _flash_update(s, m_ref.at[hh], acc_ref.at[hh], vt_ref[0, c, hh * V_ROWS:(hh + 1) * V_ROWS, :])

    _pipelined_key_loop(i, nh, produce, consume)
    o_ref[0] = jnp.concatenate([_normalized(acc_ref[hh], MLA_V).T for hh in range(nh)], axis=-1).astype(o_ref.dtype)


def _mla_call(qmt, km, vmt):
    b, nk, _, tq = qmt.shape
    s = nk * tq
    nh = MLA_HEADS_PER_STEP
    return pl.pallas_call(
        _mla_kernel,
        grid=(b, H_MLA // nh, nk),
        in_specs=[pl.BlockSpec((1, 1, nh * LANES, tq), lambda bb, hp, i: (bb, i, hp, 0)),
                  pl.BlockSpec((1, s, nh * LANES), lambda bb, hp, i: (bb, 0, hp)),
                  pl.BlockSpec((1, nk, nh * V_ROWS, tq), lambda bb, hp, i: (bb, 0, hp, 0))],
        out_specs=pl.BlockSpec((1, tq, nh * MLA_V), lambda bb, hp, i: (bb, i, hp)),
        out_shape=jax.ShapeDtypeStruct((b, s, H_MLA * MLA_V), MXU_DTYPE),
        scratch_shapes=[pltpu.VMEM((2, nh, tq, tq), jnp.float32),
                        pltpu.VMEM((nh, 1, tq), jnp.float32), pltpu.VMEM((nh, V_ROWS, tq), jnp.float32)],
        compiler_params=pltpu.CompilerParams(dimension_semantics=("arbitrary",) * 3, vmem_limit_bytes=VMEM_LIMIT),
        name="mla",
    )(qmt, km, vmt)


def _ordered_to_float(k):
    bits = k ^ ((k >> 31) & jnp.int32(0x7FFFFFFF))
    return lax.bitcast_convert_type(bits, jnp.float32)


def _coarse(x):
    bits = lax.bitcast_convert_type(x, jnp.int32) & jnp.int32(-65536)
    return lax.bitcast_convert_type(bits, jnp.float32).astype(jnp.bfloat16)


def _dsa_kernel(n_sel, qdt_ref, qit_ref, wit_ref, ki_ref, kd_ref, vdt_ref, bd_ref, bp_ref, o_ref,
                sc_ref, sc16_ref, qim_ref, thr_ref, s_ref, m_ref, acc_ref):
    i = pl.program_id(1)
    tq = qdt_ref.shape[-1]
    n_chunks = i + 1
    causal = _causal_tile(tq)
    int_min = jnp.int32(-2 ** 31)

    pad = jnp.zeros((LANES - D_IDX, tq), MXU_DTYPE)
    for hh in range(H_IDX):
        qim_ref[hh] = jnp.concatenate([qit_ref[0, 0, hh * D_IDX:(hh + 1) * D_IDX, :], pad], axis=0)

    def score_chunk(jc, diag):
        kc = ki_ref[0, pl.ds(pl.multiple_of(jc * tq, tq), tq), :]
        w = wit_ref[0, 0]
        acc = jnp.zeros((tq, tq), jnp.float32)
        for hh in range(H_IDX):
            acc = acc + jnp.maximum(_dot(kc, qim_ref[hh]), 0.0) * w[hh:hh + 1, :]
        if diag:
            acc = jnp.where(causal, acc, NEG)
        sc_ref[jc] = acc
        sc16_ref[jc] = acc.astype(jnp.bfloat16)

    def score_quad(jq, carry):
        for r in range(4):
            score_chunk(4 * jq + r, False)
        return carry

    lax.fori_loop(0, i // 4, score_quad, 0)
    rest = i % 4

    @pl.when(rest >= 2)
    def _():
        score_chunk(i - rest, False)
        score_chunk(i - rest + 1, False)

    @pl.when(rest % 2 == 1)
    def _():
        score_chunk(i - 1, False)

    score_chunk(i, True)

    @pl.when(n_chunks % 2 == 1)
    def _():
        sc_ref[n_chunks] = jnp.full((tq, tq), -jnp.inf, jnp.float32)
        sc16_ref[n_chunks] = jnp.full((tq, tq), -jnp.inf, jnp.bfloat16)

    n_pairs = (n_chunks + 1) // 2

    def count_where(hits):
        def fold(jc):
            return jnp.sum(hits(sc_ref[jc], jc).reshape(tq // SUBLANES, SUBLANES, tq), axis=0)

        def body(jp, acc):
            return acc + fold(2 * jp) + fold(2 * jp + 1)
        acc = lax.fori_loop(0, n_pairs, body, jnp.zeros((SUBLANES, tq), jnp.float32))
        return jnp.sum(acc, axis=0, keepdims=True)

    def count_ge(thr):
        return count_where(lambda sc, jc: jnp.where(sc >= thr, 1.0, 0.0))

    pack = 2 * SUBLANES
    one16 = jnp.ones((tq, tq), jnp.bfloat16)
    zero16 = jnp.zeros((tq, tq), jnp.bfloat16)

    def count_ge_coarse(thr):
        thr16 = _coarse(thr)

        def fold(jc):
            hit = jnp.where(sc16_ref[jc] >= thr16, one16, zero16).reshape(tq // pack, pack, tq)
            parts = [hit[r] for r in range(tq // pack)]
            while len(parts) > 1:
                parts = [parts[r] + parts[r + 1] for r in range(0, len(parts), 2)]
            return parts[0].astype(jnp.float32)

        def body(jp, acc):
            return acc + fold(2 * jp) + fold(2 * jp + 1)
        acc = lax.fori_loop(0, n_pairs, body, jnp.zeros((pack, tq), jnp.float32))
        return jnp.sum(acc, axis=0, keepdims=True)

    c0 = count_ge_coarse(jnp.zeros((1, tq), jnp.float32))
    start = (jnp.where(c0 >= n_sel, jnp.int32(0), int_min), jnp.where(c0 >= n_sel, c0, jnp.float32(2 * n_sel)))

    def bit_body(count, top):
        def body(t, carry):
            k, c_k = carry
            cand = k + lax.shift_left(jnp.int32(1), jnp.int32(top) - t)
            c = count(_ordered_to_float(cand))
            take = c >= n_sel
            return jnp.where(take, cand, k), jnp.where(take, c, c_k)
        return body

    k16, _ = lax.fori_loop(0, 15, bit_body(count_ge_coarse, 30), start)
    g_key = k16 + jnp.where(k16 < 0, jnp.int32(0xFFFF), jnp.int32(0))
    base = (g_key - jnp.int32(1 << 16), jnp.full((1, tq), 2.0 * n_sel, jnp.float32))
    kth, c_kth = lax.fori_loop(0, 17, bit_body(count_ge, 16), base)
    thr_ref[...] = _ordered_to_float(kth)

    @pl.when(jnp.max(c_kth) > n_sel)
    def _():
        thr_ = thr_ref[...]
        excess = c_kth - n_sel
        key = lax.broadcasted_iota(jnp.int32, (tq, tq), 0)
        col = lax.broadcasted_iota(jnp.int32, (tq, tq), 1)
        tri = jnp.where(col >= key, 1.0, 0.0).astype(MXU_DTYPE)

        def drop_chunk(jc, after):
            sc = sc_ref[jc]
            tie = jnp.where(sc == thr_, 1.0, 0.0)
            rank = _dot(tri, tie.astype(MXU_DTYPE)) + after
            sc_ref[jc] = jnp.where(rank <= excess, jnp.where(sc == thr_, -jnp.inf, sc), sc)
            return rank[0:1, :]

        def drop_pair(j, after):
            jp = n_pairs - 1 - j
            return drop_chunk(2 * jp, drop_chunk(2 * jp + 1, after))

        lax.fori_loop(0, n_pairs, drop_pair, jnp.zeros((1, tq), jnp.float32))

    thr = thr_ref[...]

    m_ref[...] = jnp.full(m_ref.shape, NEG, jnp.float32)
    acc_ref[...] = jnp.zeros(acc_ref.shape, jnp.float32)

    def produce(c, slot, hh):
        kb = kd_ref[0, pl.ds(pl.multiple_of(c * tq, tq), tq), :]
        s_ref[slot, hh] = _dot(kb, qdt_ref[0, 0, hh * LANES:(hh + 1) * LANES, :])

    def prepare(c, kind):
        return sc_ref[c] >= thr

    def consume(c, slot, hh, kind, sel):
        s = s_ref[slot, hh]
        if kind == "diag":
            s = jnp.where(causal, s + bd_ref[hh], NEG)
        elif kind == "prev":
            s = s + bp_ref[hh]
        s = jnp.where(sel, s, NEG)
        _flash_update(s, m_ref.at[hh], acc_ref.at[hh], vdt_ref[0, c])

    _pipelined_key_loop(i, H_DSA, produce, consume, prepare)
    o_ref[0] = jnp.concatenate([_normalized(acc_ref[hh], DSA_DIM).T for hh in range(H_DSA)], axis=-1).astype(o_ref.dtype)


def _dsa_call(qdt, qit, wit, ki, kd, vdt, bias_diag, bias_prev):
    b, nk, _, tq = qdt.shape
    s = nk * tq
    n_sel = min(TOPK_MAX, s // 4)
    assert tq >= n_sel and tq + 1 >= MAX_DISTANCE
    qblk = lambda r: pl.BlockSpec((1, 1, r, tq), lambda bb, i: (bb, i, 0, 0))
    kblk = lambda w: pl.BlockSpec((1, s, w), lambda bb, i: (bb, 0, 0))
    bblk = pl.BlockSpec((H_DSA, tq, tq), lambda bb, i: (0, 0, 0))
    return pl.pallas_call(
        functools.partial(_dsa_kernel, n_sel),
        grid=(b, nk),
        in_specs=[qblk(H_DSA * LANES), qblk(H_IDX * D_IDX), qblk(H_IDX),
                  kblk(LANES), kblk(LANES),
                  pl.BlockSpec((1, nk, V_ROWS, tq), lambda bb, i: (bb, 0, 0, 0)), bblk, bblk],
        out_specs=pl.BlockSpec((1, tq, H_DSA * DSA_DIM), lambda bb, i: (bb, i, 0)),
        out_shape=jax.ShapeDtypeStruct((b, s, H_DSA * DSA_DIM), MXU_DTYPE),
        scratch_shapes=[pltpu.VMEM((nk + 1, tq, tq), jnp.float32),
                        pltpu.VMEM((nk + 1, tq, tq), jnp.bfloat16),
                        pltpu.VMEM((H_IDX, LANES, tq), MXU_DTYPE),
                        pltpu.VMEM((1, tq), jnp.float32),
                        pltpu.VMEM((2, H_DSA, tq, tq), jnp.float32),
                        pltpu.VMEM((H_DSA, 1, tq), jnp.float32),
                        pltpu.VMEM((H_DSA, V_ROWS, tq), jnp.float32)],
        compiler_params=pltpu.CompilerParams(dimension_semantics=("arbitrary",) * 2, vmem_limit_bytes=VMEM_LIMIT),
        name="dsa",
    )(qdt, qit, wit, ki, kd, vdt, bias_diag, bias_prev)


def _diff_kernel(lam_scale, qct_ref, kc_ref, vct_ref, bd_ref, bp_ref, lam_ref, g_ref, o_ref,
                 qm_ref, s_ref, m_ref, acc_ref):
    i = pl.program_id(1)
    tq = qct_ref.shape[-1]
    n_maps = 2 * H_DIFF
    causal = _causal_tile(tq)

    q = qct_ref[0, 0]
    row_map = lax.broadcasted_iota(jnp.int32, q.shape, 0) // DIFF_QK
    for mm in range(n_maps):
        qm_ref[mm] = jnp.where(row_map == mm, q, jnp.zeros_like(q))
    m_ref[...] = jnp.full(m_ref.shape, NEG, jnp.float32)
    acc_ref[...] = jnp.zeros(acc_ref.shape, jnp.float32)

    def produce(c, slot, mm):
        kb = kc_ref[0, pl.ds(pl.multiple_of(c * tq, tq), tq), :]
        s_ref[slot, mm] = _dot(kb, qm_ref[mm])

    def consume(c, slot, mm, kind, ctx):
        hh = mm // 2
        s = s_ref[slot, mm]
        if kind == "diag":
            s = jnp.where(causal, s + bd_ref[hh], NEG)
        elif kind == "prev":
            s = s + bp_ref[hh]
        _flash_update(s, m_ref.at[mm], acc_ref.at[mm], vct_ref[0, c, hh * V_ROWS:(hh + 1) * V_ROWS, :])

    _pipelined_key_loop(i, n_maps, produce, consume)

    lam = lam_ref[...]
    outs = []
    for hh in range(H_DIFF):
        o = _normalized(acc_ref[2 * hh], DIFF_V) - lam * _normalized(acc_ref[2 * hh + 1], DIFF_V)
        o = o * lax.rsqrt(jnp.mean(o * o, axis=0, keepdims=True) + EPS)
        outs.append((o * g_ref[...] * lam_scale).T)
    o_ref[0] = jnp.concatenate(outs, axis=-1).astype(o_ref.dtype)


def _diff_call(qct, kc, vct, bias_diag, bias_prev, lam, subln, lam_init):
    b, nk, width, tq = qct.shape
    s = nk * tq
    assert tq + 1 >= MAX_DISTANCE
    bblk = pl.BlockSpec((H_DIFF, tq, tq), lambda bb, i: (0, 0, 0))
    return pl.pallas_call(
        functools.partial(_diff_kernel, 1.0 - lam_init),
        grid=(b, nk),
        in_specs=[pl.BlockSpec((1, 1, width, tq), lambda bb, i: (bb, i, 0, 0)),
                  pl.BlockSpec((1, s, width), lambda bb, i: (bb, 0, 0)),
                  pl.BlockSpec((1, nk, H_DIFF * V_ROWS, tq), lambda bb, i: (bb, 0, 0, 0)), bblk, bblk,
                  pl.BlockSpec((1, 1), lambda bb, i: (0, 0)), pl.BlockSpec((DIFF_V, 1), lambda bb, i: (0, 0))],
        out_specs=pl.BlockSpec((1, tq, H_DIFF * DIFF_V), lambda bb, i: (bb, i, 0)),
        out_shape=jax.ShapeDtypeStruct((b, s, H_DIFF * DIFF_V), MXU_DTYPE),
        scratch_shapes=[pltpu.VMEM((2 * H_DIFF, width, tq), MXU_DTYPE),
                        pltpu.VMEM((2, 2 * H_DIFF, tq, tq), jnp.float32),
                        pltpu.VMEM((2 * H_DIFF, 1, tq), jnp.float32),
                        pltpu.VMEM((2 * H_DIFF, V_ROWS, tq), jnp.float32)],
        compiler_params=pltpu.CompilerParams(dimension_semantics=("arbitrary",) * 2, vmem_limit_bytes=VMEM_LIMIT),
        name="diff",
    )(qct, kc, vct, bias_diag, bias_prev, lam, subln)


def _weight_copies(wup_hbm, wdn_hbm, wup_ref, wdn_ref, sem):
    return (pltpu.make_async_copy(wup_hbm, wup_ref, sem.at[0]), pltpu.make_async_copy(wdn_hbm, wdn_ref, sem.at[1]))


def _out_mlp_kernel(final, x_ref, oa_ref, ob_ref, oc_ref, wo_ref, g_ref, wup_hbm, wdn_hbm, gf_ref, y_ref,
                    wup_ref, wdn_ref, sem):
    @pl.when(pl.program_id(0) == 0)
    def _():
        copies = _weight_copies(wup_hbm, wdn_hbm, wup_ref, wdn_ref, sem)
        for c in copies:
            c.start()
        for c in copies:
            c.wait()

    mix = jnp.concatenate([oa_ref[...], ob_ref[...], oc_ref[...]], axis=-1)
    x = x_ref[...] + _dot(mix, wo_ref[...])
    h = x * lax.rsqrt(jnp.mean(x * x, axis=-1, keepdims=True) + EPS)
    h = (h * g_ref[...]).astype(MXU_DTYPE)
    y = x
    for c in range(D_FF // FF_CHUNK):
        cs = slice(c * FF_CHUNK, (c + 1) * FF_CHUNK)
        u = jnp.maximum(_dot(h, wup_ref[:, cs]), 0.0)
        y = y + _dot((u * u).astype(MXU_DTYPE), wdn_ref[cs, :])
    if final:
        y = y * lax.rsqrt(jnp.mean(y * y, axis=-1, keepdims=True) + EPS) * gf_ref[...]
    y_ref[...] = y


def _out_mlp_call(x2d, oa, ob, oc, wo, g, wup, wdn, gf, final):
    t = x2d.shape[0]
    tm = TM_MLP
    assert t % tm == 0
    row = lambda a: pl.BlockSpec((tm, a.shape[1]), lambda i: (i, 0))
    full = lambda a: pl.BlockSpec(a.shape, lambda i: (0, 0))
    hbm = pl.BlockSpec(memory_space=pl.ANY)
    return pl.pallas_call(
        functools.partial(_out_mlp_kernel, final),
        grid=(t // tm,),
        in_specs=[row(x2d), row(oa), row(ob), row(oc), full(wo), full(g), hbm, hbm, full(gf)],
        out_specs=row(x2d),
        out_shape=jax.ShapeDtypeStruct(x2d.shape, jnp.float32),
        scratch_shapes=[pltpu.VMEM(wup.shape, wup.dtype), pltpu.VMEM(wdn.shape, wdn.dtype),
                        pltpu.SemaphoreType.DMA((2,))],
        compiler_params=pltpu.CompilerParams(dimension_semantics=("arbitrary",), vmem_limit_bytes=VMEM_LIMIT),
        name="out_mlp",
    )(x2d, oa, ob, oc, wo, g, wup, wdn, gf)


def _place(dst_cols, pieces):
    rows = pieces[0][1].shape[0]
    blocks, cursor = [], 0
    for off, a in sorted(pieces, key=lambda p: p[0]):
        assert off >= cursor
        if off > cursor:
            blocks.append(jnp.zeros((rows, off - cursor), jnp.float32))
        blocks.append(a.astype(jnp.float32))
        cursor = off + a.shape[1]
    if dst_cols > cursor:
        blocks.append(jnp.zeros((rows, dst_cols - cursor), jnp.float32))
    return jnp.concatenate(blocks, axis=1)


def _rot_half_cols(w):
    half = w.shape[1] // 2
    return jnp.concatenate([-w[:, half:], w[:, :half]], axis=1)


def _prep_in_proj(w_in):
    o = _OFF
    col = lambda k: w_in[:, o[k]:o[k + 1]]
    kr = col(2)
    s_dsa = (DSA_DIM ** -0.5) * LOG2E
    s_diff = (DIFF_QK ** -0.5) * LOG2E
    s_idx = (D_IDX ** -0.5) * (H_IDX ** -0.5)
    pieces = [(C_CQ, col(0)), (C_CKV, col(1)),
              (C_KR + MLA_NOPE, kr), (C_KRR + MLA_NOPE, _rot_half_cols(kr)),
              (C_KB, col(4)), (C_KC, col(10))]
    pieces.append((C_KI, col(7)))
    t_pieces = [(R_QI, col(6)), (R_QC, col(9) * s_diff), (R_VB, col(5)), (R_WI, col(8) * s_idx)]
    qb, vc = col(3), col(11)
    for hh in range(H_DSA):
        t_pieces.append((R_QB + hh * LANES, qb[:, hh * DSA_DIM:(hh + 1) * DSA_DIM] * s_dsa))
    for hh in range(H_DIFF):
        t_pieces.append((R_VC + hh * V_ROWS, vc[:, hh * DIFF_V:(hh + 1) * DIFF_V]))
    return _place(N1, pieces).astype(MXU_DTYPE), _place(N1T, t_pieces).T.astype(MXU_DTYPE)


def _prep_mla_up(w_uq, w_ukv):
    dq = MLA_NOPE + MLA_ROPE
    s_mla = (dq ** -0.5) * LOG2E
    q_p, k_p, v_p = [], [], []
    for hh in range(H_MLA):
        wq = w_uq[:, hh * dq:(hh + 1) * dq] * s_mla
        q_p.append((hh * LANES, wq))
        wkv = w_ukv[:, hh * (MLA_NOPE + MLA_V):(hh + 1) * (MLA_NOPE + MLA_V)]
        k_p.append((hh * LANES, wkv[:, :MLA_NOPE]))
        v_p.append((hh * V_ROWS, wkv[:, MLA_NOPE:]))
    width = H_MLA * LANES
    wuq, wuk = (_place(width, p) for p in (q_p, k_p))
    wuv = _place(H_MLA * V_ROWS, v_p)
    return wuq.T.astype(MXU_DTYPE), wuk.astype(MXU_DTYPE), wuv.T.astype(MXU_DTYPE)


def _rope_tables(positions, tm):
    half = MLA_ROPE // 2
    freqs = ROPE_BASE ** (-jnp.arange(half, dtype=jnp.float32) / half)
    ang = positions.astype(jnp.float32)[..., None] * freqs
    cos, sin = lax.optimization_barrier((jnp.cos(ang), jnp.sin(ang)))
    lead = cos.shape[:-1]
    ones = jnp.ones(lead + (MLA_NOPE,), jnp.float32)
    zeros = jnp.zeros(lead + (LANES - MLA_NOPE - MLA_ROPE,), jnp.float32)
    cos_r = jnp.concatenate([ones, cos, cos, zeros], axis=-1).reshape(-1, LANES)
    sin_r = jnp.concatenate([0.0 * ones, sin, sin, zeros], axis=-1).reshape(-1, LANES)
    to_cols = lambda a: jnp.transpose(a.reshape(-1, tm, LANES), (0, 2, 1))
    return cos_r, sin_r, to_cols(cos_r), to_cols(sin_r)


def _t5_bucket(rel):
    n = jnp.maximum(rel, 0)
    max_exact = N_BUCKETS // 2
    nf = jnp.maximum(n, 1).astype(jnp.float32)
    large = max_exact + (jnp.log(nf / max_exact) / math.log(MAX_DISTANCE / max_exact)
                         * (N_BUCKETS - max_exact)).astype(jnp.int32)
    large = jnp.minimum(large, N_BUCKETS - 1)
    return jnp.where(n < max_exact, n, large)


def _bias_tiles(rel_bias, tq):
    n_heads = rel_bias.shape[1]
    dist = jnp.arange(2 * tq, dtype=jnp.int32)
    b = ((rel_bias[_t5_bucket(dist)] - rel_bias[N_BUCKETS - 1]) * LOG2E).T
    b = lax.optimization_barrier(b)

    def toeplitz(u):
        flat = jnp.tile(u, (1, tq))[:, :tq * (2 * tq - 1)]
        return flat.reshape(n_heads, tq, 2 * tq - 1)[:, :, :tq]

    diag = toeplitz(jnp.concatenate([b[:, :tq], jnp.zeros_like(b[:, :tq])], axis=1))
    prev = toeplitz(jnp.concatenate([b[:, tq:], b[:, :tq]], axis=1))
    return diag, prev


def kernel(x, positions, rel_bias, norm_attn, w_in, q_norm, w_uq, kv_norm, w_ukv, diff_lambda, diff_subln,
           w_out, norm_mlp, w_up, w_down, norm_final):
    b, s, d = x.shape
    depth = w_in.shape[0]
    assert s % TQ == 0
    nk = s // TQ
    cos_r, sin_r, cos_c, sin_c = _rope_tables(positions, TQ)
    bias_diag, bias_prev = _bias_tiles(rel_bias.astype(jnp.float32), TQ)
    row2 = lambda v: v.reshape(1, -1).astype(jnp.float32)
    x2d = x.reshape(b * s, d)
    for l in range(depth):
        w1, w1t = _prep_in_proj(w_in[l])
        wuqt, wuk, wuvt = _prep_mla_up(w_uq[l], w_ukv[l])
        qmt, km, vmt, qdt, kd, vdt, qit, ki, wit, qct, kc, vct = _proj_call(
            x2d, row2(norm_attn[l]), w1, w1t, row2(q_norm[l]), wuqt, row2(kv_norm[l]), wuk, wuvt,
            cos_r, sin_r, cos_c, sin_c)
        r3 = lambda a: a.reshape(b, s, a.shape[-1])
        r4 = lambda a: a.reshape(b, nk, a.shape[-2], a.shape[-1])
        o_a = _mla_call(r4(qmt), r3(km), r4(vmt))
        o_b = _dsa_call(r4(qdt), r4(qit), r4(wit), r3(ki), r3(kd), r4(vdt), bias_diag[:H_DSA], bias_prev[:H_DSA])
        lam_init = 0.8 - 0.6 * math.exp(-0.3 * l)
        lp = diff_lambda[l].astype(jnp.float32)
        lam = (jnp.exp(jnp.sum(lp[0] * lp[1])) - jnp.exp(jnp.sum(lp[2] * lp[3])) + lam_init).reshape(1, 1)
        o_c = _diff_call(r4(qct), r3(kc), r4(vct), bias_diag[H_DSA:], bias_prev[H_DSA:], lam,
                         diff_subln[l].reshape(-1, 1).astype(jnp.float32), lam_init)
        f2 = lambda a: a.reshape(b * s, a.shape[-1])
        x2d = _out_mlp_call(x2d, f2(o_a), f2(o_b), f2(o_c), w_out[l].astype(MXU_DTYPE), row2(norm_mlp[l]),
                            w_up[l].astype(MXU_DTYPE), w_down[l].astype(MXU_DTYPE), row2(norm_final), l == depth - 1)
    return x2d.reshape(b, s, d)
```

```python
import functools
import math

import jax
import jax.numpy as jnp
import numpy as np
from jax import lax
from jax.experimental import pallas as pl
from jax.experimental.pallas import tpu as pltpu

D_MODEL = 1024
H_MLA, MLA_NOPE, MLA_ROPE, MLA_V = 8, 64, 32, 64
Q_LORA, KV_LORA = 384, 256
H_DSA, DSA_DIM, H_IDX, D_IDX, TOPK_MAX = 4, 64, 8, 32, 256
H_DIFF, DIFF_QK = 4, 32
DIFF_V = 2 * DIFF_QK
D_FF = 4 * D_MODEL
N_BUCKETS, MAX_DISTANCE = 32, 128
ROPE_BASE = 10000.0
EPS = 1e-6
NEG = -1e30
LOG2E = math.log2(math.e)

_SPLITS = (Q_LORA, KV_LORA, MLA_ROPE, H_DSA * DSA_DIM, DSA_DIM, DSA_DIM, H_IDX * D_IDX, D_IDX, H_IDX,
           H_DIFF * 2 * DIFF_QK, H_DIFF * 2 * DIFF_QK, H_DIFF * DIFF_V)
_OFF = tuple(int(o) for o in np.concatenate([[0], np.cumsum(_SPLITS)]))

LANES = 128
SUBLANES = 8
MXU_DTYPE = jnp.bfloat16
VMEM_LIMIT = 56 * 1024 * 1024

C_CQ = 0
C_CKV = C_CQ + Q_LORA
C_KR = C_CKV + KV_LORA
C_KRR = C_KR + LANES
C_KB = C_KRR + LANES
C_KC = C_KB + LANES
N1 = C_KC + H_DIFF * 2 * DIFF_QK
R_QB = 0
R_QI = R_QB + H_DSA * DSA_DIM
R_QC = R_QI + H_IDX * D_IDX
R_VB = R_QC + H_DIFF * 2 * DIFF_QK
V_ROWS = 80
R_VC = R_VB + V_ROWS
R_WI = R_VC + H_DIFF * V_ROWS
WI_ROWS = 16
N1T = R_WI + WI_ROWS

ONES_ROW = 64

TQ = 256
TM_PROJ = 512
MLA_HEADS_PER_STEP = 8
TM_MLP = 512
FF_CHUNK = 1024

_NT = (((1,), (1,)), ((), ()))


def _dot(a, b):
    return jnp.dot(a, b, preferred_element_type=jnp.float32)


def _dot_nt(a, b):
    return lax.dot_general(a, b, _NT, preferred_element_type=jnp.float32)


def _ones_rows(rows):
    r = lax.broadcasted_iota(jnp.int32, (rows, 1), 0)
    return jnp.where(r % V_ROWS == ONES_ROW, 1.0, 0.0).astype(jnp.float32)


def _proj_kernel(x_ref, g_ref, w1_ref, w1t_ref, qn_ref, wuqt_ref, kvn_ref, wuk_ref, wuvt_ref,
                 c_ref, s_ref, ct_ref, st_ref,
                 qmt_ref, km_ref, vmt_ref, qdt_ref, kd_ref, vdt_ref, qit_ref, wit_ref, qct_ref, kc_ref, vct_ref):
    x = x_ref[...]
    h = x * lax.rsqrt(jnp.mean(x * x, axis=-1, keepdims=True) + EPS)
    h = (h * g_ref[...]).astype(MXU_DTYPE)

    def seg(a, b):
        return _dot(h, w1_ref[:, a:b])

    def seg_t(a, b):
        return _dot_nt(w1t_ref[a:b, :], h)

    n_sub = ct_ref.shape[0]
    tq = ct_ref.shape[2]

    def put_t(ref, val, rows=slice(None)):
        for j in range(n_sub):
            ref[j, rows, :] = val[:, j * tq:(j + 1) * tq].astype(ref.dtype)

    c_q = seg(C_CQ, C_CKV)
    c_q = c_q * lax.rsqrt(jnp.mean(c_q * c_q, axis=-1, keepdims=True) + EPS)
    c_q = (c_q * qn_ref[...]).astype(MXU_DTYPE)
    qa = _dot_nt(wuqt_ref[...], c_q)
    half = MLA_ROPE // 2
    rope = slice(MLA_NOPE, MLA_NOPE + half)
    cos_h = jnp.concatenate([ct_ref[j, rope, :] for j in range(n_sub)], axis=1)
    sin_h = jnp.concatenate([st_ref[j, rope, :] for j in range(n_sub)], axis=1)
    for hh in range(H_MLA):
        b0 = hh * LANES
        r1 = slice(b0 + MLA_NOPE, b0 + MLA_NOPE + half)
        r2 = slice(b0 + MLA_NOPE + half, b0 + MLA_NOPE + MLA_ROPE)
        rest = slice(b0 + MLA_NOPE + MLA_ROPE, b0 + LANES)
        x1, x2 = qa[r1, :], qa[r2, :]
        put_t(qmt_ref, qa[b0:b0 + MLA_NOPE, :], slice(b0, b0 + MLA_NOPE))
        put_t(qmt_ref, x1 * cos_h - x2 * sin_h, r1)
        put_t(qmt_ref, x2 * cos_h + x1 * sin_h, r2)
        put_t(qmt_ref, qa[rest, :], rest)

    c_kv = seg(C_CKV, C_KR)
    c_kv = c_kv * lax.rsqrt(jnp.mean(c_kv * c_kv, axis=-1, keepdims=True) + EPS)
    c_kv = (c_kv * kvn_ref[...]).astype(MXU_DTYPE)
    k_rope = seg(C_KR, C_KRR) * c_ref[...] + seg(C_KRR, C_KB) * s_ref[...]
    kk = _dot(c_kv, wuk_ref[...])
    for hh in range(H_MLA):
        sl = slice(hh * LANES, (hh + 1) * LANES)
        km_ref[:, sl] = (kk[:, sl] + k_rope).astype(km_ref.dtype)
    put_t(vmt_ref, _dot_nt(wuvt_ref[...], c_kv) + _ones_rows(H_MLA * V_ROWS))

    kd_ref[...] = seg(C_KB, C_KC).astype(kd_ref.dtype)
    kc_ref[...] = seg(C_KC, N1).astype(kc_ref.dtype)
    put_t(qdt_ref, seg_t(R_QB, R_QI))
    put_t(qit_ref, seg_t(R_QI, R_QC))
    put_t(qct_ref, seg_t(R_QC, R_VB))
    put_t(vdt_ref, seg_t(R_VB, R_VC) + _ones_rows(V_ROWS))
    put_t(vct_ref, seg_t(R_VC, R_WI) + _ones_rows(H_DIFF * V_ROWS))
    put_t(wit_ref, seg_t(R_WI, R_WI + WI_ROWS)[:H_IDX, :])


def _proj_call(x2d, g, w1, w1t, qn, wuqt, kvn, wuk, wuvt, cos_r, sin_r, cos_c, sin_c):
    t = x2d.shape[0]
    tm, tq = TM_PROJ, TQ
    assert t % tm == 0 and tm % tq == 0
    n = t // tq
    row = lambda w: pl.BlockSpec((tm, w), lambda i: (i, 0))
    colmajor = lambda r: pl.BlockSpec((tm // tq, r, tq), lambda i: (i, 0, 0))
    full = lambda a: pl.BlockSpec(a.shape, lambda i: (0, 0))
    bf = MXU_DTYPE
    width_c = 2 * H_DIFF * DIFF_QK
    outs = [
        (colmajor(H_MLA * LANES), (n, H_MLA * LANES, tq), bf),
        (row(H_MLA * LANES), (t, H_MLA * LANES), bf),
        (colmajor(H_MLA * V_ROWS), (n, H_MLA * V_ROWS, tq), bf),
        (colmajor(H_DSA * DSA_DIM), (n, H_DSA * DSA_DIM, tq), bf),
        (row(LANES), (t, LANES), bf),
        (colmajor(V_ROWS), (n, V_ROWS, tq), bf),
        (colmajor(H_IDX * D_IDX), (n, H_IDX * D_IDX, tq), bf),
        (colmajor(H_IDX), (n, H_IDX, tq), jnp.float32),
        (colmajor(width_c), (n, width_c, tq), bf),
        (row(width_c), (t, width_c), bf),
        (colmajor(H_DIFF * V_ROWS), (n, H_DIFF * V_ROWS, tq), bf),
    ]
    return pl.pallas_call(
        _proj_kernel,
        grid=(t // tm,),
        in_specs=[row(D_MODEL), full(g), full(w1), full(w1t), full(qn), full(wuqt), full(kvn), full(wuk),
                  full(wuvt), row(LANES), row(LANES), colmajor(LANES), colmajor(LANES)],
        out_specs=[o[0] for o in outs],
        out_shape=[jax.ShapeDtypeStruct(o[1], o[2]) for o in outs],
        compiler_params=pltpu.CompilerParams(dimension_semantics=("arbitrary",), vmem_limit_bytes=VMEM_LIMIT),
        name="proj",
    )(x2d, g, w1, w1t, qn, wuqt, kvn, wuk, wuvt, cos_r, sin_r, cos_c, sin_c)


def _flash_update(s, m_ref, acc_ref, v_t):
    m_prev = m_ref[...]
    m_new = jnp.maximum(m_prev, jnp.max(s, axis=0, keepdims=True))
    alpha = jnp.exp2(m_prev - m_new)
    p = jnp.exp2(s - m_new).astype(MXU_DTYPE)
    acc_ref[...] = alpha * acc_ref[...] + _dot(v_t, p)
    m_ref[...] = m_new


def _normalized(acc, width):
    return acc[:width, :] / acc[ONES_ROW:ONES_ROW + 1, :]


def _causal_tile(tq):
    key = lax.broadcasted_iota(jnp.int32, (tq, tq), 0)
    qry = lax.broadcasted_iota(jnp.int32, (tq, tq), 1)
    return key <= qry


def _pipelined_key_loop(i, n_heads, produce, consume, prepare=None):
    prepare = prepare or (lambda c, kind: None)

    def step(c, slot, kind, nxt):
        ctx = prepare(c, kind)
        for hh in range(n_heads):
            consume(c, slot, hh, kind, ctx)
            if nxt:
                produce(c + 1, 1 - slot, hh)

    for hh in range(n_heads):
        produce(0, 0, hh)
    n_far_pairs = jnp.maximum(i - 1, 0) // 2
    n_far_quads = n_far_pairs // 2

    def quad(q, carry):
        for r in range(4):
            step(4 * q + r, r % 2, "far", True)
        return carry

    def body(p, carry):
        step(2 * p, 0, "far", True)
        step(2 * p + 1, 1, "far", True)
        return carry

    lax.fori_loop(0, n_far_quads, quad, 0)
    lax.fori_loop(2 * n_far_quads, n_far_pairs, body, 0)
    c0 = 2 * n_far_pairs
    tail = i + 1 - c0

    @pl.when(tail == 3)
    def _():
        step(c0, 0, "far", True)
        step(c0 + 1, 1, "prev", True)
        step(c0 + 2, 0, "diag", False)

    @pl.when(tail == 2)
    def _():
        step(c0, 0, "prev", True)
        step(c0 + 1, 1, "diag", False)

    @pl.when(tail == 1)
    def _():
        step(c0, 0, "diag", False)


def _mla_kernel(qt_ref, k_ref, vt_ref, o_ref, s_ref, m_ref, acc_ref):
    i = pl.program_id(2)
    tq = qt_ref.shape[-1]
    nh = MLA_HEADS_PER_STEP
    m_ref[...] = jnp.full(m_ref.shape, NEG, jnp.float32)
    acc_ref[...] = jnp.zeros(acc_ref.shape, jnp.float32)
    causal = _causal_tile(tq)

    def produce(c, slot, hh):
        sl = slice(hh * LANES, (hh + 1) * LANES)
        kb = k_ref[0, pl.ds(pl.multiple_of(c * tq, tq), tq), sl]
        s_ref[slot, hh] = _dot(kb, qt_ref[0, 0, sl, :])

    def consume(c, slot, hh, kind, ctx):
        s = s_ref[slot, hh]
        if kind == "diag":
            s = jnp.where(causal, s, NEG)
        _flash_update(s, m_ref.at[hh], acc_ref.at[hh], vt_ref[0, c, hh * V_ROWS:(hh + 1) * V_ROWS, :])

    _pipelined_key_loop(i, nh, produce, consume)
    o_ref[0] = jnp.concatenate([_normalized(acc_ref[hh], MLA_V).T for hh in range(nh)], axis=-1).astype(o_ref.dtype)


def _mla_call(qmt, km, vmt):
    b, nk, _, tq = qmt.shape
    s = nk * tq
    nh = MLA_HEADS_PER_STEP
    return pl.pallas_call(
        _mla_kernel,
        grid=(b, H_MLA // nh, nk),
        in_specs=[pl.BlockSpec((1, 1, nh * LANES, tq), lambda bb, hp, i: (bb, i, hp, 0)),
                  pl.BlockSpec((1, s, nh * LANES), lambda bb, hp, i: (bb, 0, hp)),
                  pl.BlockSpec((1, nk, nh * V_ROWS, tq), lambda bb, hp, i: (bb, 0, hp, 0))],
        out_specs=pl.BlockSpec((1, tq, nh * MLA_V), lambda bb, hp, i: (bb, i, hp)),
        out_shape=jax.ShapeDtypeStruct((b, s, H_MLA * MLA_V), MXU_DTYPE),
        scratch_shapes=[pltpu.VMEM((2, nh, tq, tq), jnp.float32),
                        pltpu.VMEM((nh, 1, tq), jnp.float32), pltpu.VMEM((nh, V_ROWS, tq), jnp.float32)],
        compiler_params=pltpu.CompilerParams(dimension_semantics=("arbitrary",) * 3, vmem_limit_bytes=VMEM_LIMIT),
        name="mla",
    )(qmt, km, vmt)


def _ordered_to_float(k):
    bits = k ^ ((k >> 31) & jnp.int32(0x7FFFFFFF))
    return lax.bitcast_convert_type(bits, jnp.float32)


def _coarse(x):
    bits = lax.bitcast_convert_type(x, jnp.int32) & jnp.int32(-65536)
    return lax.bitcast_convert_type(bits, jnp.float32).astype(jnp.bfloat16)


def _dsa_kernel(n_sel, qdt_ref, qit_ref, wit_ref, kd_ref, vdt_ref, bd_ref, bp_ref, o_ref,
                sc_ref, sc16_ref, qim_ref, qdp_ref, thr_ref, s_ref, m_ref, acc_ref):
    i = pl.program_id(1)
    tq = qdt_ref.shape[-1]
    n_chunks = i + 1
    causal = _causal_tile(tq)
    int_min = jnp.int32(-2 ** 31)

    zeros = lambda rows: jnp.zeros((rows, tq), MXU_DTYPE)
    for hh in range(H_IDX):
        qi_h = qit_ref[0, 0, hh * D_IDX:(hh + 1) * D_IDX, :]
        qim_ref[hh] = jnp.concatenate([zeros(DSA_DIM), qi_h, zeros(LANES - DSA_DIM - D_IDX)], axis=0)
    for hh in range(H_DSA):
        qdp_ref[hh] = jnp.concatenate([qdt_ref[0, 0, hh * DSA_DIM:(hh + 1) * DSA_DIM, :], zeros(LANES - DSA_DIM)], axis=0)

    def score_chunk(jc, diag):
        kc = kd_ref[0, pl.ds(pl.multiple_of(jc * tq, tq), tq), :]
        w = wit_ref[0, 0]
        acc = jnp.zeros((tq, tq), jnp.float32)
        for hh in range(H_IDX):
            acc = acc + jnp.maximum(_dot(kc, qim_ref[hh]), 0.0) * w[hh:hh + 1, :]
        if diag:
            acc = jnp.where(causal, acc, NEG)
        sc_ref[jc] = acc
        sc16_ref[jc] = acc.astype(jnp.bfloat16)

    def score_quad(jq, carry):
        for r in range(4):
            score_chunk(4 * jq + r, False)
        return carry

    lax.fori_loop(0, i // 4, score_quad, 0)
    rest = i % 4

    @pl.when(rest >= 2)
    def _():
        score_chunk(i - rest, False)
        score_chunk(i - rest + 1, False)

    @pl.when(rest % 2 == 1)
    def _():
        score_chunk(i - 1, False)

    score_chunk(i, True)

    @pl.when(n_chunks % 2 == 1)
    def _():
        sc_ref[n_chunks] = jnp.full((tq, tq), -jnp.inf, jnp.float32)
        sc16_ref[n_chunks] = jnp.full((tq, tq), -jnp.inf, jnp.bfloat16)

    n_pairs = (n_chunks + 1) // 2

    def count_where(hits):
        def fold(jc):
            return jnp.sum(hits(sc_ref[jc], jc).reshape(tq // SUBLANES, SUBLANES, tq), axis=0)

        def body(jp, acc):
            return acc + fold(2 * jp) + fold(2 * jp + 1)
        acc = lax.fori_loop(0, n_pairs, body, jnp.zeros((SUBLANES, tq), jnp.float32))
        return jnp.sum(acc, axis=0, keepdims=True)

    def count_ge(thr):
        return count_where(lambda sc, jc: jnp.where(sc >= thr, 1.0, 0.0))

    pack = 2 * SUBLANES
    one16 = jnp.ones((tq, tq), jnp.bfloat16)
    zero16 = jnp.zeros((tq, tq), jnp.bfloat16)

    def count_ge_coarse(thr):
        thr16 = _coarse(thr)

        def fold(jc):
            hit = jnp.where(sc16_ref[jc] >= thr16, one16, zero16).reshape(tq // pack, pack, tq)
            parts = [hit[r] for r in range(tq // pack)]
            while len(parts) > 1:
                parts = [parts[r] + parts[r + 1] for r in range(0, len(parts), 2)]
            return parts[0].astype(jnp.float32)

        def body(jp, acc):
            return acc + fold(2 * jp) + fold(2 * jp + 1)
        acc = lax.fori_loop(0, n_pairs, body, jnp.zeros((pack, tq), jnp.float32))
        return jnp.sum(acc, axis=0, keepdims=True)

    c0 = count_ge_coarse(jnp.zeros((1, tq), jnp.float32))
    start = (jnp.where(c0 >= n_sel, jnp.int32(0), int_min), jnp.where(c0 >= n_sel, c0, jnp.float32(2 * n_sel)))

    def bit_body(count, top):
        def body(t, carry):
            k, c_k = carry
            cand = k + lax.shift_left(jnp.int32(1), jnp.int32(top) - t)
            c = count(_ordered_to_float(cand))
            take = c >= n_sel
            return jnp.where(take, cand, k), jnp.where(take, c, c_k)
        return body

    k16, _ = lax.fori_loop(0, 15, bit_body(count_ge_coarse, 30), start)
    g_key = k16 + jnp.where(k16 < 0, jnp.int32(0xFFFF), jnp.int32(0))
    base = (g_key - jnp.int32(1 << 16), jnp.full((1, tq), 2.0 * n_sel, jnp.float32))
    kth, c_kth = lax.fori_loop(0, 17, bit_body(count_ge, 16), base)
    thr_ref[...] = _ordered_to_float(kth)

    @pl.when(jnp.max(c_kth) > n_sel)
    def _():
        thr_ = thr_ref[...]
        excess = c_kth - n_sel
        key = lax.broadcasted_iota(jnp.int32, (tq, tq), 0)
        col = lax.broadcasted_iota(jnp.int32, (tq, tq), 1)
        tri = jnp.where(col >= key, 1.0, 0.0).astype(MXU_DTYPE)

        def drop_chunk(jc, after):
            sc = sc_ref[jc]
            tie = jnp.where(sc == thr_, 1.0, 0.0)
            rank = _dot(tri, tie.astype(MXU_DTYPE)) + after
            sc_ref[jc] = jnp.where(rank <= excess, jnp.where(sc == thr_, -jnp.inf, sc), sc)
            return rank[0:1, :]

        def drop_pair(j, after):
            jp = n_pairs - 1 - j
            return drop_chunk(2 * jp, drop_chunk(2 * jp + 1, after))

        lax.fori_loop(0, n_pairs, drop_pair, jnp.zeros((1, tq), jnp.float32))

    thr = thr_ref[...]

    m_ref[...] = jnp.full(m_ref.shape, NEG, jnp.float32)
    acc_ref[...] = jnp.zeros(acc_ref.shape, jnp.float32)

    def produce(c, slot, hh):
        kb = kd_ref[0, pl.ds(pl.multiple_of(c * tq, tq), tq), :]
        s_ref[slot, hh] = _dot(kb, qdp_ref[hh])

    def prepare(c, kind):
        return sc_ref[c] >= thr

    def consume(c, slot, hh, kind, sel):
        s = s_ref[slot, hh]
        if kind == "diag":
            s = jnp.where(causal, s + bd_ref[hh], NEG)
        elif kind == "prev":
            s = s + bp_ref[hh]
        s = jnp.where(sel, s, NEG)
        _flash_update(s, m_ref.at[hh], acc_ref.at[hh], vdt_ref[0, c])

    _pipelined_key_loop(i, H_DSA, produce, consume, prepare)
    o_ref[0] = jnp.concatenate([_normalized(acc_ref[hh], DSA_DIM).T for hh in range(H_DSA)], axis=-1).astype(o_ref.dtype)


def _dsa_call(qdt, qit, wit, kd, vdt, bias_diag, bias_prev):
    b, nk, _, tq = qdt.shape
    s = nk * tq
    n_sel = min(TOPK_MAX, s // 4)
    assert tq >= n_sel and tq + 1 >= MAX_DISTANCE
    qblk = lambda r: pl.BlockSpec((1, 1, r, tq), lambda bb, i: (bb, i, 0, 0))
    kblk = lambda w: pl.BlockSpec((1, s, w), lambda bb, i: (bb, 0, 0))
    bblk = pl.BlockSpec((H_DSA, tq, tq), lambda bb, i: (0, 0, 0))
    return pl.pallas_call(
        functools.partial(_dsa_kernel, n_sel),
        grid=(b, nk),
        in_specs=[qblk(H_DSA * DSA_DIM), qblk(H_IDX * D_IDX), qblk(H_IDX),
                  kblk(LANES),
                  pl.BlockSpec((1, nk, V_ROWS, tq), lambda bb, i: (bb, 0, 0, 0)), bblk, bblk],
        out_specs=pl.BlockSpec((1, tq, H_DSA * DSA_DIM), lambda bb, i: (bb, i, 0)),
        out_shape=jax.ShapeDtypeStruct((b, s, H_DSA * DSA_DIM), MXU_DTYPE),
        scratch_shapes=[pltpu.VMEM((nk + 1, tq, tq), jnp.float32),
                        pltpu.VMEM((nk + 1, tq, tq), jnp.bfloat16),
                        pltpu.VMEM((H_IDX, LANES, tq), MXU_DTYPE),
                        pltpu.VMEM((H_DSA, LANES, tq), MXU_DTYPE),
                        pltpu.VMEM((1, tq), jnp.float32),
                        pltpu.VMEM((2, H_DSA, tq, tq), jnp.float32),
                        pltpu.VMEM((H_DSA, 1, tq), jnp.float32),
                        pltpu.VMEM((H_DSA, V_ROWS, tq), jnp.float32)],
        compiler_params=pltpu.CompilerParams(dimension_semantics=("arbitrary",) * 2, vmem_limit_bytes=VMEM_LIMIT),
        name="dsa",
    )(qdt, qit, wit, kd, vdt, bias_diag, bias_prev)


def _diff_kernel(lam_scale, qct_ref, kc_ref, vct_ref, bd_ref, bp_ref, lam_ref, g_ref, o_ref,
                 qm_ref, s_ref, m_ref, acc_ref):
    i = pl.program_id(1)
    tq = qct_ref.shape[-1]
    n_maps = 2 * H_DIFF
    causal = _causal_tile(tq)

    q = qct_ref[0, 0]
    row_map = lax.broadcasted_iota(jnp.int32, q.shape, 0) // DIFF_QK
    for mm in range(n_maps):
        qm_ref[mm] = jnp.where(row_map == mm, q, jnp.zeros_like(q))
    m_ref[...] = jnp.full(m_ref.shape, NEG, jnp.float32)
    acc_ref[...] = jnp.zeros(acc_ref.shape, jnp.float32)

    def produce(c, slot, mm):
        kb = kc_ref[0, pl.ds(pl.multiple_of(c * tq, tq), tq), :]
        s_ref[slot, mm] = _dot(kb, qm_ref[mm])

    def consume(c, slot, mm, kind, ctx):
        hh = mm // 2
        s = s_ref[slot, mm]
        if kind == "diag":
            s = jnp.where(causal, s + bd_ref[hh], NEG)
        elif kind == "prev":
            s = s + bp_ref[hh]
        _flash_update(s, m_ref.at[mm], acc_ref.at[mm], vct_ref[0, c, hh * V_ROWS:(hh + 1) * V_ROWS, :])

    _pipelined_key_loop(i, n_maps, produce, consume)

    lam = lam_ref[...]
    outs = []
    for hh in range(H_DIFF):
        o = _normalized(acc_ref[2 * hh], DIFF_V) - lam * _normalized(acc_ref[2 * hh + 1], DIFF_V)
        o = o * lax.rsqrt(jnp.mean(o * o, axis=0, keepdims=True) + EPS)
        outs.append((o * g_ref[...] * lam_scale).T)
    o_ref[0] = jnp.concatenate(outs, axis=-1).astype(o_ref.dtype)


def _diff_call(qct, kc, vct, bias_diag, bias_prev, lam, subln, lam_init):
    b, nk, width, tq = qct.shape
    s = nk * tq
    assert tq + 1 >= MAX_DISTANCE
    bblk = pl.BlockSpec((H_DIFF, tq, tq), lambda bb, i: (0, 0, 0))
    return pl.pallas_call(
        functools.partial(_diff_kernel, 1.0 - lam_init),
        grid=(b, nk),
        in_specs=[pl.BlockSpec((1, 1, width, tq), lambda bb, i: (bb, i, 0, 0)),
                  pl.BlockSpec((1, s, width), lambda bb, i: (bb, 0, 0)),
                  pl.BlockSpec((1, nk, H_DIFF * V_ROWS, tq), lambda bb, i: (bb, 0, 0, 0)), bblk, bblk,
                  pl.BlockSpec((1, 1), lambda bb, i: (0, 0)), pl.BlockSpec((DIFF_V, 1), lambda bb, i: (0, 0))],
        out_specs=pl.BlockSpec((1, tq, H_DIFF * DIFF_V), lambda bb, i: (bb, i, 0)),
        out_shape=jax.ShapeDtypeStruct((b, s, H_DIFF * DIFF_V), MXU_DTYPE),
        scratch_shapes=[pltpu.VMEM((2 * H_DIFF, width, tq), MXU_DTYPE),
                        pltpu.VMEM((2, 2 * H_DIFF, tq, tq), jnp.float32),
                        pltpu.VMEM((2 * H_DIFF, 1, tq), jnp.float32),
                        pltpu.VMEM((2 * H_DIFF, V_ROWS, tq), jnp.float32)],
        compiler_params=pltpu.CompilerParams(dimension_semantics=("arbitrary",) * 2, vmem_limit_bytes=VMEM_LIMIT),
        name="diff",
    )(qct, kc, vct, bias_diag, bias_prev, lam, subln)


def _weight_copies(wup_hbm, wdn_hbm, wup_ref, wdn_ref, sem):
    return (pltpu.make_async_copy(wup_hbm, wup_ref, sem.at[0]), pltpu.make_async_copy(wdn_hbm, wdn_ref, sem.at[1]))


def _out_mlp_kernel(final, x_ref, oa_ref, ob_ref, oc_ref, wo_ref, g_ref, wup_hbm, wdn_hbm, gf_ref, y_ref,
                    wup_ref, wdn_ref, sem):
    @pl.when(pl.program_id(0) == 0)
    def _():
        copies = _weight_copies(wup_hbm, wdn_hbm, wup_ref, wdn_ref, sem)
        for c in copies:
            c.start()
        for c in copies:
            c.wait()

    mix = jnp.concatenate([oa_ref[...], ob_ref[...], oc_ref[...]], axis=-1)
    x = x_ref[...] + _dot(mix, wo_ref[...])
    h = x * lax.rsqrt(jnp.mean(x * x, axis=-1, keepdims=True) + EPS)
    h = (h * g_ref[...]).astype(MXU_DTYPE)
    y = x
    for c in range(D_FF // FF_CHUNK):
        cs = slice(c * FF_CHUNK, (c + 1) * FF_CHUNK)
        u = jnp.maximum(_dot(h, wup_ref[:, cs]), 0.0)
        y = y + _dot((u * u).astype(MXU_DTYPE), wdn_ref[cs, :])
    if final:
        y = y * lax.rsqrt(jnp.mean(y * y, axis=-1, keepdims=True) + EPS) * gf_ref[...]
    y_ref[...] = y


def _out_mlp_call(x2d, oa, ob, oc, wo, g, wup, wdn, gf, final):
    t = x2d.shape[0]
    tm = TM_MLP
    assert t % tm == 0
    row = lambda a: pl.BlockSpec((tm, a.shape[1]), lambda i: (i, 0))
    full = lambda a: pl.BlockSpec(a.shape, lambda i: (0, 0))
    hbm = pl.BlockSpec(memory_space=pl.ANY)
    return pl.pallas_call(
        functools.partial(_out_mlp_kernel, final),
        grid=(t // tm,),
        in_specs=[row(x2d), row(oa), row(ob), row(oc), full(wo), full(g), hbm, hbm, full(gf)],
        out_specs=row(x2d),
        out_shape=jax.ShapeDtypeStruct(x2d.shape, jnp.float32),
        scratch_shapes=[pltpu.VMEM(wup.shape, wup.dtype), pltpu.VMEM(wdn.shape, wdn.dtype),
                        pltpu.SemaphoreType.DMA((2,))],
        compiler_params=pltpu.CompilerParams(dimension_semantics=("arbitrary",), vmem_limit_bytes=VMEM_LIMIT),
        name="out_mlp",
    )(x2d, oa, ob, oc, wo, g, wup, wdn, gf)


def _place(dst_cols, pieces):
    rows = pieces[0][1].shape[0]
    out = jnp.zeros((rows, dst_cols), jnp.float32)
    for off, a in pieces:
        out = lax.dynamic_update_slice(out, a.astype(jnp.float32), (0, off))
    return out


def _rot_half_cols(w):
    half = w.shape[1] // 2
    return jnp.concatenate([-w[:, half:], w[:, :half]], axis=1)


def _prep_in_proj(w_in):
    o = _OFF
    col = lambda k: w_in[:, o[k]:o[k + 1]]
    kr = col(2)
    s_dsa = (DSA_DIM ** -0.5) * LOG2E
    s_diff = (DIFF_QK ** -0.5) * LOG2E
    s_idx = (D_IDX ** -0.5) * (H_IDX ** -0.5)
    pieces = [(C_CQ, col(0)), (C_CKV, col(1)),
              (C_KR + MLA_NOPE, kr), (C_KRR + MLA_NOPE, _rot_half_cols(kr)),
              (C_KB, col(4)), (C_KB + DSA_DIM, col(7)), (C_KC, col(10))]
    t_pieces = [(R_QB, col(3) * s_dsa), (R_QI, col(6)), (R_QC, col(9) * s_diff), (R_VB, col(5)),
                (R_WI, col(8) * s_idx)]
    vc = col(11)
    for hh in range(H_DIFF):
        t_pieces.append((R_VC + hh * V_ROWS, vc[:, hh * DIFF_V:(hh + 1) * DIFF_V]))
    return _place(N1, pieces).astype(MXU_DTYPE), _place(N1T, t_pieces).T.astype(MXU_DTYPE)


def _prep_mla_up(w_uq, w_ukv):
    dq = MLA_NOPE + MLA_ROPE
    s_mla = (dq ** -0.5) * LOG2E
    q_p, k_p, v_p = [], [], []
    for hh in range(H_MLA):
        wq = w_uq[:, hh * dq:(hh + 1) * dq] * s_mla
        q_p.append((hh * LANES, wq))
        wkv = w_ukv[:, hh * (MLA_NOPE + MLA_V):(hh + 1) * (MLA_NOPE + MLA_V)]
        k_p.append((hh * LANES, wkv[:, :MLA_NOPE]))
        v_p.append((hh * V_ROWS, wkv[:, MLA_NOPE:]))
    width = H_MLA * LANES
    wuq, wuk = (_place(width, p) for p in (q_p, k_p))
    wuv = _place(H_MLA * V_ROWS, v_p)
    return wuq.T.astype(MXU_DTYPE), wuk.astype(MXU_DTYPE), wuv.T.astype(MXU_DTYPE)


def _rope_tables(positions, tm):
    half = MLA_ROPE // 2
    freqs = ROPE_BASE ** (-jnp.arange(half, dtype=jnp.float32) / half)
    ang = positions.astype(jnp.float32)[..., None] * freqs
    cos, sin = jnp.cos(ang), jnp.sin(ang)
    lead = cos.shape[:-1]
    ones = jnp.ones(lead + (MLA_NOPE,), jnp.float32)
    zeros = jnp.zeros(lead + (LANES - MLA_NOPE - MLA_ROPE,), jnp.float32)
    cos_r = jnp.concatenate([ones, cos, cos, zeros], axis=-1).reshape(-1, LANES)
    sin_r = jnp.concatenate([0.0 * ones, sin, sin, zeros], axis=-1).reshape(-1, LANES)
    to_cols = lambda a: jnp.transpose(a.reshape(-1, tm, LANES), (0, 2, 1))
    return cos_r, sin_r, to_cols(cos_r), to_cols(sin_r)


def _t5_bucket(rel):
    n = jnp.maximum(rel, 0)
    max_exact = N_BUCKETS // 2
    nf = jnp.maximum(n, 1).astype(jnp.float32)
    large = max_exact + (jnp.log(nf / max_exact) / math.log(MAX_DISTANCE / max_exact)
                         * (N_BUCKETS - max_exact)).astype(jnp.int32)
    large = jnp.minimum(large, N_BUCKETS - 1)
    return jnp.where(n < max_exact, n, large)


def _bias_tiles(rel_bias, tq):
    n_heads = rel_bias.shape[1]
    dist = jnp.arange(2 * tq, dtype=jnp.int32)
    b = ((rel_bias[_t5_bucket(dist)] - rel_bias[N_BUCKETS - 1]) * LOG2E).T

    def toeplitz(u):
        flat = jnp.tile(u, (1, tq))[:, :tq * (2 * tq - 1)]
        return flat.reshape(n_heads, tq, 2 * tq - 1)[:, :, :tq]

    diag = toeplitz(jnp.concatenate([b[:, :tq], jnp.zeros_like(b[:, :tq])], axis=1))
    prev = toeplitz(jnp.concatenate([b[:, tq:], b[:, :tq]], axis=1))
    return diag, prev


def kernel(x, positions, rel_bias, norm_attn, w_in, q_norm, w_uq, kv_norm, w_ukv, diff_lambda, diff_subln,
           w_out, norm_mlp, w_up, w_down, norm_final):
    b, s, d = x.shape
    depth = w_in.shape[0]
    assert s % TQ == 0
    nk = s // TQ
    cos_r, sin_r, cos_c, sin_c = _rope_tables(positions, TQ)
    bias_diag, bias_prev = _bias_tiles(rel_bias.astype(jnp.float32), TQ)
    row2 = lambda v: v.reshape(1, -1).astype(jnp.float32)
    x2d = x.reshape(b * s, d)
    for l in range(depth):
        w1, w1t = _prep_in_proj(w_in[l])
        wuqt, wuk, wuvt = _prep_mla_up(w_uq[l], w_ukv[l])
        qmt, km, vmt, qdt, kd, vdt, qit, wit, qct, kc, vct = _proj_call(
            x2d, row2(norm_attn[l]), w1, w1t, row2(q_norm[l]), wuqt, row2(kv_norm[l]), wuk, wuvt,
            cos_r, sin_r, cos_c, sin_c)
        r3 = lambda a: a.reshape(b, s, a.shape[-1])
        r4 = lambda a: a.reshape(b, nk, a.shape[-2], a.shape[-1])
        o_a = _mla_call(r4(qmt), r3(km), r4(vmt))
        o_b = _dsa_call(r4(qdt), r4(qit), r4(wit), r3(kd), r4(vdt), bias_diag[:H_DSA], bias_prev[:H_DSA])
        lam_init = 0.8 - 0.6 * math.exp(-0.3 * l)
        lp = diff_lambda[l].astype(jnp.float32)
        lam = (jnp.exp(jnp.sum(lp[0] * lp[1])) - jnp.exp(jnp.sum(lp[2] * lp[3])) + lam_init).reshape(1, 1)
        o_c = _diff_call(r4(qct), r3(kc), r4(vct), bias_diag[H_DSA:], bias_prev[H_DSA:], lam,
                         diff_subln[l].reshape(-1, 1).astype(jnp.float32), lam_init)
        f2 = lambda a: a.reshape(b * s, a.shape[-1])
        x2d = _out_mlp_call(x2d, f2(o_a), f2(o_b), f2(o_c), w_out[l].astype(MXU_DTYPE), row2(norm_mlp[l]),
                            w_up[l].astype(MXU_DTYPE), w_down[l].astype(MXU_DTYPE), row2(norm_final), l == depth - 1)
    return x2d.reshape(b, s, d)
```

```python
import functools
import math

import jax
import jax.numpy as jnp
import numpy as np
from jax import lax
from jax.experimental import pallas as pl
from jax.experimental.pallas import tpu as pltpu

D_MODEL = 1024
H_MLA, MLA_NOPE, MLA_ROPE, MLA_V = 8, 64, 32, 64
Q_LORA, KV_LORA = 384, 256
H_DSA, DSA_DIM, H_IDX, D_IDX, TOPK_MAX = 4, 64, 8, 32, 256
H_DIFF, DIFF_QK = 4, 32
DIFF_V = 2 * DIFF_QK
D_FF = 4 * D_MODEL
N_BUCKETS, MAX_DISTANCE = 32, 128
ROPE_BASE = 10000.0
EPS = 1e-6
NEG = -1e30
LOG2E = math.log2(math.e)

_SPLITS = (Q_LORA, KV_LORA, MLA_ROPE, H_DSA * DSA_DIM, DSA_DIM, DSA_DIM, H_IDX * D_IDX, D_IDX, H_IDX,
           H_DIFF * 2 * DIFF_QK, H_DIFF * 2 * DIFF_QK, H_DIFF * DIFF_V)
_OFF = tuple(int(o) for o in np.concatenate([[0], np.cumsum(_SPLITS)]))

LANES = 128
SUBLANES = 8
MXU_DTYPE = jnp.bfloat16
VMEM_LIMIT = 56 * 1024 * 1024

C_CQ = 0
C_CKV = C_CQ + Q_LORA
C_KR = C_CKV + KV_LORA
C_KRR = C_KR + LANES
C_KB = C_KRR + LANES
C_KC = C_KB + LANES
N1 = C_KC + H_DIFF * 2 * DIFF_QK
R_QB = 0
R_QI = R_QB + H_DSA * DSA_DIM
R_QC = R_QI + H_IDX * D_IDX
R_VB = R_QC + H_DIFF * 2 * DIFF_QK
V_ROWS = 80
R_VC = R_VB + V_ROWS
R_WI = R_VC + H_DIFF * V_ROWS
WI_ROWS = 16
N1T = R_WI + WI_ROWS

ONES_ROW = 64

TQ = 256
TM_PROJ = 512
TM_MLP = 512
FF_CHUNK = 1024

_NT = (((1,), (1,)), ((), ()))


def _dot(a, b):
    return jnp.dot(a, b, preferred_element_type=jnp.float32)


def _dot_nt(a, b):
    return lax.dot_general(a, b, _NT, preferred_element_type=jnp.float32)


def _ones_rows(rows):
    r = lax.broadcasted_iota(jnp.int32, (rows, 1), 0)
    return jnp.where(r % V_ROWS == ONES_ROW, 1.0, 0.0).astype(jnp.float32)


def _proj_kernel(x_ref, g_ref, w1_ref, w1t_ref, qn_ref, wuqt_ref, kvn_ref, wuk_ref, wuvt_ref,
                 c_ref, s_ref, ct_ref, st_ref,
                 qmt_ref, km_ref, vmt_ref, qdt_ref, kd_ref, vdt_ref, qit_ref, wit_ref, qct_ref, kc_ref, vct_ref):
    x = x_ref[...]
    h = x * lax.rsqrt(jnp.mean(x * x, axis=-1, keepdims=True) + EPS)
    h = (h * g_ref[...]).astype(MXU_DTYPE)

    def seg(a, b):
        return _dot(h, w1_ref[:, a:b])

    def seg_t(a, b):
        return _dot_nt(w1t_ref[a:b, :], h)

    n_sub = ct_ref.shape[0]
    tq = ct_ref.shape[2]

    def put_t(ref, val, rows=slice(None)):
        for j in range(n_sub):
            ref[j, rows, :] = val[:, j * tq:(j + 1) * tq].astype(ref.dtype)

    c_q = seg(C_CQ, C_CKV)
    c_q = c_q * lax.rsqrt(jnp.mean(c_q * c_q, axis=-1, keepdims=True) + EPS)
    c_q = (c_q * qn_ref[...]).astype(MXU_DTYPE)
    qa = _dot_nt(wuqt_ref[...], c_q)
    half = MLA_ROPE // 2
    rope = slice(MLA_NOPE, MLA_NOPE + half)
    cos_h = jnp.concatenate([ct_ref[j, rope, :] for j in range(n_sub)], axis=1)
    sin_h = jnp.concatenate([st_ref[j, rope, :] for j in range(n_sub)], axis=1)
    for hh in range(H_MLA):
        b0 = hh * LANES
        r1 = slice(b0 + MLA_NOPE, b0 + MLA_NOPE + half)
        r2 = slice(b0 + MLA_NOPE + half, b0 + MLA_NOPE + MLA_ROPE)
        rest = slice(b0 + MLA_NOPE + MLA_ROPE, b0 + LANES)
        x1, x2 = qa[r1, :], qa[r2, :]
        put_t(qmt_ref, qa[b0:b0 + MLA_NOPE, :], slice(b0, b0 + MLA_NOPE))
        put_t(qmt_ref, x1 * cos_h - x2 * sin_h, r1)
        put_t(qmt_ref, x2 * cos_h + x1 * sin_h, r2)
        put_t(qmt_ref, qa[rest, :], rest)

    c_kv = seg(C_CKV, C_KR)
    c_kv = c_kv * lax.rsqrt(jnp.mean(c_kv * c_kv, axis=-1, keepdims=True) + EPS)
    c_kv = (c_kv * kvn_ref[...]).astype(MXU_DTYPE)
    k_rope = seg(C_KR, C_KRR) * c_ref[...] + seg(C_KRR, C_KB) * s_ref[...]
    kk = _dot(c_kv, wuk_ref[...])
    for hh in range(H_MLA):
        sl = slice(hh * LANES, (hh + 1) * LANES)
        km_ref[:, sl] = (kk[:, sl] + k_rope).astype(km_ref.dtype)
    put_t(vmt_ref, _dot_nt(wuvt_ref[...], c_kv) + _ones_rows(H_MLA * V_ROWS))

    kd_ref[...] = seg(C_KB, C_KC).astype(kd_ref.dtype)
    kc_ref[...] = seg(C_KC, N1).astype(kc_ref.dtype)
    put_t(qdt_ref, seg_t(R_QB, R_QI))
    put_t(qit_ref, seg_t(R_QI, R_QC))
    put_t(qct_ref, seg_t(R_QC, R_VB))
    put_t(vdt_ref, seg_t(R_VB, R_VC) + _ones_rows(V_ROWS))
    put_t(vct_ref, seg_t(R_VC, R_WI) + _ones_rows(H_DIFF * V_ROWS))
    put_t(wit_ref, seg_t(R_WI, R_WI + WI_ROWS)[:H_IDX, :])


def _proj_call(x2d, g, w1, w1t, qn, wuqt, kvn, wuk, wuvt, cos_r, sin_r, cos_c, sin_c):
    t = x2d.shape[0]
    tm, tq = TM_PROJ, TQ
    assert t % tm == 0 and tm % tq == 0
    n = t // tq
    row = lambda w: pl.BlockSpec((tm, w), lambda i: (i, 0))
    colmajor = lambda r: pl.BlockSpec((tm // tq, r, tq), lambda i: (i, 0, 0))
    full = lambda a: pl.BlockSpec(a.shape, lambda i: (0, 0))
    bf = MXU_DTYPE
    width_c = 2 * H_DIFF * DIFF_QK
    outs = [
        (colmajor(H_MLA * LANES), (n, H_MLA * LANES, tq), bf),
        (row(H_MLA * LANES), (t, H_MLA * LANES), bf),
        (colmajor(H_MLA * V_ROWS), (n, H_MLA * V_ROWS, tq), bf),
        (colmajor(H_DSA * DSA_DIM), (n, H_DSA * DSA_DIM, tq), bf),
        (row(LANES), (t, LANES), bf),
        (colmajor(V_ROWS), (n, V_ROWS, tq), bf),
        (colmajor(H_IDX * D_IDX), (n, H_IDX * D_IDX, tq), bf),
        (colmajor(H_IDX), (n, H_IDX, tq), jnp.float32),
        (colmajor(width_c), (n, width_c, tq), bf),
        (row(width_c), (t, width_c), bf),
        (colmajor(H_DIFF * V_ROWS), (n, H_DIFF * V_ROWS, tq), bf),
    ]
    return pl.pallas_call(
        _proj_kernel,
        grid=(t // tm,),
        in_specs=[row(D_MODEL), full(g), full(w1), full(w1t), full(qn), full(wuqt), full(kvn), full(wuk),
                  full(wuvt), row(LANES), row(LANES), colmajor(LANES), colmajor(LANES)],
        out_specs=[o[0] for o in outs],
        out_shape=[jax.ShapeDtypeStruct(o[1], o[2]) for o in outs],
        compiler_params=pltpu.CompilerParams(dimension_semantics=("arbitrary",), vmem_limit_bytes=VMEM_LIMIT),
        name="proj",
    )(x2d, g, w1, w1t, qn, wuqt, kvn, wuk, wuvt, cos_r, sin_r, cos_c, sin_c)


def _flash_update(s, m_ref, acc_ref, v_t):
    m_prev = m_ref[...]
    m_new = jnp.maximum(m_prev, jnp.max(s, axis=0, keepdims=True))
    alpha = jnp.exp2(m_prev - m_new)
    p = jnp.exp2(s - m_new).astype(MXU_DTYPE)
    acc_ref[...] = alpha * acc_ref[...] + _dot(v_t, p)
    m_ref[...] = m_new


def _normalized(acc, width):
    return acc[:width, :] / acc[ONES_ROW:ONES_ROW + 1, :]


def _causal_tile(tq):
    key = lax.broadcasted_iota(jnp.int32, (tq, tq), 0)
    qry = lax.broadcasted_iota(jnp.int32, (tq, tq), 1)
    return key <= qry


def _pipelined_key_loop(i, n_heads, produce, consume, prepare=None):
    prepare = prepare or (lambda c, kind: None)

    def step(c, slot, kind, nxt):
        ctx = prepare(c, kind)
        for hh in range(n_heads):
            consume(c, slot, hh, kind, ctx)
            if nxt:
                produce(c + 1, 1 - slot, hh)

    for hh in range(n_heads):
        produce(0, 0, hh)
    n_far_pairs = jnp.maximum(i - 1, 0) // 2
    n_far_quads = n_far_pairs // 2

    def quad(q, carry):
        for r in range(4):
            step(4 * q + r, r % 2, "far", True)
        return carry

    def body(p, carry):
        step(2 * p, 0, "far", True)
        step(2 * p + 1, 1, "far", True)
        return carry

    lax.fori_loop(0, n_far_quads, quad, 0)
    lax.fori_loop(2 * n_far_quads, n_far_pairs, body, 0)
    c0 = 2 * n_far_pairs
    tail = i + 1 - c0

    @pl.when(tail == 3)
    def _():
        step(c0, 0, "far", True)
        step(c0 + 1, 1, "prev", True)
        step(c0 + 2, 0, "diag", False)

    @pl.when(tail == 2)
    def _():
        step(c0, 0, "prev", True)
        step(c0 + 1, 1, "diag", False)

    @pl.when(tail == 1)
    def _():
        step(c0, 0, "diag", False)


def _ordered_to_float(k):
    bits = k ^ ((k >> 31) & jnp.int32(0x7FFFFFFF))
    return lax.bitcast_convert_type(bits, jnp.float32)


def _coarse(x):
    bits = lax.bitcast_convert_type(x, jnp.int32) & jnp.int32(-65536)
    return lax.bitcast_convert_type(bits, jnp.float32).astype(jnp.bfloat16)


def _dsa_kernel(n_sel, qdt_ref, qit_ref, wit_ref, kd_ref, vdt_ref, bd_ref, bp_ref, o_ref,
                sc_ref, sc16_ref, qim_ref, qdp_ref, thr_ref, s_ref, m_ref, acc_ref):
    i = pl.program_id(1)
    tq = qdt_ref.shape[-1]
    n_chunks = i + 1
    causal = _causal_tile(tq)
    int_min = jnp.int32(-2 ** 31)

    zeros = lambda rows: jnp.zeros((rows, tq), MXU_DTYPE)
    for hh in range(H_IDX):
        qi_h = qit_ref[0, 0, hh * D_IDX:(hh + 1) * D_IDX, :]
        qim_ref[hh] = jnp.concatenate([zeros(DSA_DIM), qi_h, zeros(LANES - DSA_DIM - D_IDX)], axis=0)
    for hh in range(H_DSA):
        qdp_ref[hh] = jnp.concatenate([qdt_ref[0, 0, hh * DSA_DIM:(hh + 1) * DSA_DIM, :], zeros(LANES - DSA_DIM)], axis=0)

    def score_chunk(jc, diag):
        kc = kd_ref[0, pl.ds(pl.multiple_of(jc * tq, tq), tq), :]
        w = wit_ref[0, 0]
        acc = jnp.zeros((tq, tq), jnp.float32)
        for hh in range(H_IDX):
            acc = acc + jnp.maximum(_dot(kc, qim_ref[hh]), 0.0) * w[hh:hh + 1, :]
        if diag:
            acc = jnp.where(causal, acc, NEG)
        sc_ref[jc] = acc
        sc16_ref[jc] = acc.astype(jnp.bfloat16)

    def score_quad(jq, carry):
        for r in range(4):
            score_chunk(4 * jq + r, False)
        return carry

    lax.fori_loop(0, i // 4, score_quad, 0)
    rest = i % 4

    @pl.when(rest >= 2)
    def _():
        score_chunk(i - rest, False)
        score_chunk(i - rest + 1, False)

    @pl.when(rest % 2 == 1)
    def _():
        score_chunk(i - 1, False)

    score_chunk(i, True)

    @pl.when(n_chunks % 2 == 1)
    def _():
        sc_ref[n_chunks] = jnp.full((tq, tq), -jnp.inf, jnp.float32)
        sc16_ref[n_chunks] = jnp.full((tq, tq), -jnp.inf, jnp.bfloat16)

    n_pairs = (n_chunks + 1) // 2

    def count_where(hits):
        def fold(jc):
            return jnp.sum(hits(sc_ref[jc], jc).reshape(tq // SUBLANES, SUBLANES, tq), axis=0)

        def body(jp, acc):
            return acc + fold(2 * jp) + fold(2 * jp + 1)
        acc = lax.fori_loop(0, n_pairs, body, jnp.zeros((SUBLANES, tq), jnp.float32))
        return jnp.sum(acc, axis=0, keepdims=True)

    def count_ge(thr):
        return count_where(lambda sc, jc: jnp.where(sc >= thr, 1.0, 0.0))

    pack = 2 * SUBLANES
    one16 = jnp.ones((tq, tq), jnp.bfloat16)
    zero16 = jnp.zeros((tq, tq), jnp.bfloat16)

    def count_ge_coarse(thr):
        thr16 = _coarse(thr)

        def fold(jc):
            hit = jnp.where(sc16_ref[jc] >= thr16, one16, zero16).reshape(tq // pack, pack, tq)
            parts = [hit[r] for r in range(tq // pack)]
            while len(parts) > 1:
                parts = [parts[r] + parts[r + 1] for r in range(0, len(parts), 2)]
            return parts[0].astype(jnp.float32)

        def body(jp, acc):
            return acc + fold(2 * jp) + fold(2 * jp + 1)
        acc = lax.fori_loop(0, n_pairs, body, jnp.zeros((pack, tq), jnp.float32))
        return jnp.sum(acc, axis=0, keepdims=True)

    c0 = count_ge_coarse(jnp.zeros((1, tq), jnp.float32))
    start = (jnp.where(c0 >= n_sel, jnp.int32(0), int_min), jnp.where(c0 >= n_sel, c0, jnp.float32(2 * n_sel)))

    def bit_body(count, top):
        def body(t, carry):
            k, c_k = carry
            cand = k + lax.shift_left(jnp.int32(1), jnp.int32(top) - t)
            c = count(_ordered_to_float(cand))
            take = c >= n_sel
            return jnp.where(take, cand, k), jnp.where(take, c, c_k)
        return body

    k16, _ = lax.fori_loop(0, 15, bit_body(count_ge_coarse, 30), start)
    g_key = k16 + jnp.where(k16 < 0, jnp.int32(0xFFFF), jnp.int32(0))
    base = (g_key - jnp.int32(1 << 16), jnp.full((1, tq), 2.0 * n_sel, jnp.float32))
    kth, c_kth = lax.fori_loop(0, 17, bit_body(count_ge, 16), base)
    thr_ref[...] = _ordered_to_float(kth)

    @pl.when(jnp.max(c_kth) > n_sel)
    def _():
        thr_ = thr_ref[...]
        excess = c_kth - n_sel
        key = lax.broadcasted_iota(jnp.int32, (tq, tq), 0)
        col = lax.broadcasted_iota(jnp.int32, (tq, tq), 1)
        tri = jnp.where(col >= key, 1.0, 0.0).astype(MXU_DTYPE)

        def drop_chunk(jc, after):
            sc = sc_ref[jc]
            tie = jnp.where(sc == thr_, 1.0, 0.0)
            rank = _dot(tri, tie.astype(MXU_DTYPE)) + after
            sc_ref[jc] = jnp.where(rank <= excess, jnp.where(sc == thr_, -jnp.inf, sc), sc)
            return rank[0:1, :]

        def drop_pair(j, after):
            jp = n_pairs - 1 - j
            return drop_chunk(2 * jp, drop_chunk(2 * jp + 1, after))

        lax.fori_loop(0, n_pairs, drop_pair, jnp.zeros((1, tq), jnp.float32))

    thr = thr_ref[...]

    m_ref[...] = jnp.full(m_ref.shape, NEG, jnp.float32)
    acc_ref[...] = jnp.zeros(acc_ref.shape, jnp.float32)

    def produce(c, slot, hh):
        kb = kd_ref[0, pl.ds(pl.multiple_of(c * tq, tq), tq), :]
        s_ref[slot, hh] = _dot(kb, qdp_ref[hh])

    def prepare(c, kind):
        return sc_ref[c] >= thr

    def consume(c, slot, hh, kind, sel):
        s = s_ref[slot, hh]
        if kind == "diag":
            s = jnp.where(causal, s + bd_ref[hh], NEG)
        elif kind == "prev":
            s = s + bp_ref[hh]
        s = jnp.where(sel, s, NEG)
        _flash_update(s, m_ref.at[hh], acc_ref.at[hh], vdt_ref[0, c])

    _pipelined_key_loop(i, H_DSA, produce, consume, prepare)
    o_ref[0] = jnp.concatenate([_normalized(acc_ref[hh], DSA_DIM).T for hh in range(H_DSA)], axis=-1).astype(o_ref.dtype)


def _dsa_call(qdt, qit, wit, kd, vdt, bias_diag, bias_prev):
    b, nk, _, tq = qdt.shape
    s = nk * tq
    n_sel = min(TOPK_MAX, s // 4)
    assert tq >= n_sel and tq + 1 >= MAX_DISTANCE
    qblk = lambda r: pl.BlockSpec((1, 1, r, tq), lambda bb, i: (bb, i, 0, 0))
    kblk = lambda w: pl.BlockSpec((1, s, w), lambda bb, i: (bb, 0, 0))
    bblk = pl.BlockSpec((H_DSA, tq, tq), lambda bb, i: (0, 0, 0))
    return pl.pallas_call(
        functools.partial(_dsa_kernel, n_sel),
        grid=(b, nk),
        in_specs=[qblk(H_DSA * DSA_DIM), qblk(H_IDX * D_IDX), qblk(H_IDX),
                  kblk(LANES),
                  pl.BlockSpec((1, nk, V_ROWS, tq), lambda bb, i: (bb, 0, 0, 0)), bblk, bblk],
        out_specs=pl.BlockSpec((1, tq, H_DSA * DSA_DIM), lambda bb, i: (bb, i, 0)),
        out_shape=jax.ShapeDtypeStruct((b, s, H_DSA * DSA_DIM), MXU_DTYPE),
        scratch_shapes=[pltpu.VMEM((nk + 1, tq, tq), jnp.float32),
                        pltpu.VMEM((nk + 1, tq, tq), jnp.bfloat16),
                        pltpu.VMEM((H_IDX, LANES, tq), MXU_DTYPE),
                        pltpu.VMEM((H_DSA, LANES, tq), MXU_DTYPE),
                        pltpu.VMEM((1, tq), jnp.float32),
                        pltpu.VMEM((2, H_DSA, tq, tq), jnp.float32),
                        pltpu.VMEM((H_DSA, 1, tq), jnp.float32),
                        pltpu.VMEM((H_DSA, V_ROWS, tq), jnp.float32)],
        compiler_params=pltpu.CompilerParams(dimension_semantics=("arbitrary",) * 2, vmem_limit_bytes=VMEM_LIMIT),
        name="dsa",
    )(qdt, qit, wit, kd, vdt, bias_diag, bias_prev)


def _dense_attn_kernel(lam_scale, qmt_ref, km_ref, vmt_ref, qct_ref, kc_ref, vct_ref, bd_ref, bp_ref, lam_ref, g_ref,
                       oa_ref, oc_ref, qm_ref, s_ref, m_ref, acc_ref):
    i = pl.program_id(1)
    tq = qmt_ref.shape[-1]
    n_maps = 2 * H_DIFF
    causal = _causal_tile(tq)

    q = qct_ref[0, 0]
    row_map = lax.broadcasted_iota(jnp.int32, q.shape, 0) // DIFF_QK
    for mm in range(n_maps):
        qm_ref[mm] = jnp.where(row_map == mm, q, jnp.zeros_like(q))
    m_ref[...] = jnp.full(m_ref.shape, NEG, jnp.float32)
    acc_ref[...] = jnp.zeros(acc_ref.shape, jnp.float32)

    def produce(c, slot, ch):
        rows = pl.ds(pl.multiple_of(c * tq, tq), tq)
        if ch < H_MLA:
            sl = slice(ch * LANES, (ch + 1) * LANES)
            s_ref[slot, ch] = _dot(km_ref[0, rows, sl], qmt_ref[0, 0, sl, :])
        else:
            s_ref[slot, ch] = _dot(kc_ref[0, rows, :], qm_ref[ch - H_MLA])

    def consume(c, slot, ch, kind, ctx):
        s = s_ref[slot, ch]
        if ch < H_MLA:
            if kind == "diag":
                s = jnp.where(causal, s, NEG)
            v_t = vmt_ref[0, c, ch * V_ROWS:(ch + 1) * V_ROWS, :]
        else:
            hh = (ch - H_MLA) // 2
            if kind == "diag":
                s = jnp.where(causal, s + bd_ref[hh], NEG)
            elif kind == "prev":
                s = s + bp_ref[hh]
            v_t = vct_ref[0, c, hh * V_ROWS:(hh + 1) * V_ROWS, :]
        _flash_update(s, m_ref.at[ch], acc_ref.at[ch], v_t)

    _pipelined_key_loop(i, H_MLA + n_maps, produce, consume)

    oa_ref[0] = jnp.concatenate([_normalized(acc_ref[hh], MLA_V).T for hh in range(H_MLA)], axis=-1).astype(oa_ref.dtype)
    lam = lam_ref[...]
    outs = []
    for hh in range(H_DIFF):
        a0, a1 = acc_ref[H_MLA + 2 * hh], acc_ref[H_MLA + 2 * hh + 1]
        o = _normalized(a0, DIFF_V) - lam * _normalized(a1, DIFF_V)
        o = o * lax.rsqrt(jnp.mean(o * o, axis=0, keepdims=True) + EPS)
        outs.append((o * g_ref[...] * lam_scale).T)
    oc_ref[0] = jnp.concatenate(outs, axis=-1).astype(oc_ref.dtype)


def _dense_attn_call(qmt, km, vmt, qct, kc, vct, bias_diag, bias_prev, lam, subln, lam_init):
    b, nk, width, tq = qct.shape
    s = nk * tq
    assert tq + 1 >= MAX_DISTANCE
    n_chains = H_MLA + 2 * H_DIFF
    qblk = lambda r: pl.BlockSpec((1, 1, r, tq), lambda bb, i: (bb, i, 0, 0))
    kblk = lambda w: pl.BlockSpec((1, s, w), lambda bb, i: (bb, 0, 0))
    vblk = lambda r: pl.BlockSpec((1, nk, r, tq), lambda bb, i: (bb, 0, 0, 0))
    oblk = lambda w: pl.BlockSpec((1, tq, w), lambda bb, i: (bb, i, 0))
    bblk = pl.BlockSpec((H_DIFF, tq, tq), lambda bb, i: (0, 0, 0))
    return pl.pallas_call(
        functools.partial(_dense_attn_kernel, 1.0 - lam_init),
        grid=(b, nk),
        in_specs=[qblk(H_MLA * LANES), kblk(H_MLA * LANES), vblk(H_MLA * V_ROWS),
                  qblk(width), kblk(width), vblk(H_DIFF * V_ROWS), bblk, bblk,
                  pl.BlockSpec((1, 1), lambda bb, i: (0, 0)), pl.BlockSpec((DIFF_V, 1), lambda bb, i: (0, 0))],
        out_specs=[oblk(H_MLA * MLA_V), oblk(H_DIFF * DIFF_V)],
        out_shape=[jax.ShapeDtypeStruct((b, s, H_MLA * MLA_V), MXU_DTYPE),
                   jax.ShapeDtypeStruct((b, s, H_DIFF * DIFF_V), MXU_DTYPE)],
        scratch_shapes=[pltpu.VMEM((2 * H_DIFF, width, tq), MXU_DTYPE),
                        pltpu.VMEM((2, n_chains, tq, tq), jnp.float32),
                        pltpu.VMEM((n_chains, 1, tq), jnp.float32),
                        pltpu.VMEM((n_chains, V_ROWS, tq), jnp.float32)],
        compiler_params=pltpu.CompilerParams(dimension_semantics=("arbitrary",) * 2, vmem_limit_bytes=VMEM_LIMIT),
        name="dense_attn",
    )(qmt, km, vmt, qct, kc, vct, bias_diag, bias_prev, lam, subln)


def _weight_copies(wup_hbm, wdn_hbm, wup_ref, wdn_ref, sem):
    return (pltpu.make_async_copy(wup_hbm, wup_ref, sem.at[0]), pltpu.make_async_copy(wdn_hbm, wdn_ref, sem.at[1]))


def _out_mlp_kernel(final, x_ref, oa_ref, ob_ref, oc_ref, wo_ref, g_ref, wup_hbm, wdn_hbm, gf_ref, y_ref,
                    wup_ref, wdn_ref, sem):
    @pl.when(pl.program_id(0) == 0)
    def _():
        copies = _weight_copies(wup_hbm, wdn_hbm, wup_ref, wdn_ref, sem)
        for c in copies:
            c.start()
        for c in copies:
            c.wait()

    mix = jnp.concatenate([oa_ref[...], ob_ref[...], oc_ref[...]], axis=-1)
    x = x_ref[...] + _dot(mix, wo_ref[...])
    h = x * lax.rsqrt(jnp.mean(x * x, axis=-1, keepdims=True) + EPS)
    h = (h * g_ref[...]).astype(MXU_DTYPE)
    y = x
    for c in range(D_FF // FF_CHUNK):
        cs = slice(c * FF_CHUNK, (c + 1) * FF_CHUNK)
        u = jnp.maximum(_dot(h, wup_ref[:, cs]), 0.0)
        y = y + _dot((u * u).astype(MXU_DTYPE), wdn_ref[cs, :])
    if final:
        y = y * lax.rsqrt(jnp.mean(y * y, axis=-1, keepdims=True) + EPS) * gf_ref[...]
    y_ref[...] = y


def _out_mlp_call(x2d, oa, ob, oc, wo, g, wup, wdn, gf, final):
    t = x2d.shape[0]
    tm = TM_MLP
    assert t % tm == 0
    row = lambda a: pl.BlockSpec((tm, a.shape[1]), lambda i: (i, 0))
    full = lambda a: pl.BlockSpec(a.shape, lambda i: (0, 0))
    hbm = pl.BlockSpec(memory_space=pl.ANY)
    return pl.pallas_call(
        functools.partial(_out_mlp_kernel, final),
        grid=(t // tm,),
        in_specs=[row(x2d), row(oa), row(ob), row(oc), full(wo), full(g), hbm, hbm, full(gf)],
        out_specs=row(x2d),
        out_shape=jax.ShapeDtypeStruct(x2d.shape, jnp.float32),
        scratch_shapes=[pltpu.VMEM(wup.shape, wup.dtype), pltpu.VMEM(wdn.shape, wdn.dtype),
                        pltpu.SemaphoreType.DMA((2,))],
        compiler_params=pltpu.CompilerParams(dimension_semantics=("arbitrary",), vmem_limit_bytes=VMEM_LIMIT),
        name="out_mlp",
    )(x2d, oa, ob, oc, wo, g, wup, wdn, gf)


def _place(dst_cols, pieces):
    rows = pieces[0][1].shape[0]
    out = jnp.zeros((rows, dst_cols), jnp.float32)
    for off, a in pieces:
        out = lax.dynamic_update_slice(out, a.astype(jnp.float32), (0, off))
    return out


def _rot_half_cols(w):
    half = w.shape[1] // 2
    return jnp.concatenate([-w[:, half:], w[:, :half]], axis=1)


def _prep_in_proj(w_in):
    o = _OFF
    col = lambda k: w_in[:, o[k]:o[k + 1]]
    kr = col(2)
    s_dsa = (DSA_DIM ** -0.5) * LOG2E
    s_diff = (DIFF_QK ** -0.5) * LOG2E
    s_idx = (D_IDX ** -0.5) * (H_IDX ** -0.5)
    pieces = [(C_CQ, col(0)), (C_CKV, col(1)),
              (C_KR + MLA_NOPE, kr), (C_KRR + MLA_NOPE, _rot_half_cols(kr)),
              (C_KB, col(4)), (C_KB + DSA_DIM, col(7)), (C_KC, col(10))]
    t_pieces = [(R_QB, col(3) * s_dsa), (R_QI, col(6)), (R_QC, col(9) * s_diff), (R_VB, col(5)),
                (R_WI, col(8) * s_idx)]
    vc = col(11)
    for hh in range(H_DIFF):
        t_pieces.append((R_VC + hh * V_ROWS, vc[:, hh * DIFF_V:(hh + 1) * DIFF_V]))
    return _place(N1, pieces).astype(MXU_DTYPE), _place(N1T, t_pieces).T.astype(MXU_DTYPE)


def _prep_mla_up(w_uq, w_ukv):
    dq = MLA_NOPE + MLA_ROPE
    s_mla = (dq ** -0.5) * LOG2E
    q_p, k_p, v_p = [], [], []
    for hh in range(H_MLA):
        wq = w_uq[:, hh * dq:(hh + 1) * dq] * s_mla
        q_p.append((hh * LANES, wq))
        wkv = w_ukv[:, hh * (MLA_NOPE + MLA_V):(hh + 1) * (MLA_NOPE + MLA_V)]
        k_p.append((hh * LANES, wkv[:, :MLA_NOPE]))
        v_p.append((hh * V_ROWS, wkv[:, MLA_NOPE:]))
    width = H_MLA * LANES
    wuq, wuk = (_place(width, p) for p in (q_p, k_p))
    wuv = _place(H_MLA * V_ROWS, v_p)
    return wuq.T.astype(MXU_DTYPE), wuk.astype(MXU_DTYPE), wuv.T.astype(MXU_DTYPE)


def _rope_tables(positions, tm):
    half = MLA_ROPE // 2
    freqs = ROPE_BASE ** (-jnp.arange(half, dtype=jnp.float32) / half)
    ang = positions.astype(jnp.float32)[..., None] * freqs
    cos, sin = jnp.cos(ang), jnp.sin(ang)
    lead = cos.shape[:-1]
    ones = jnp.ones(lead + (MLA_NOPE,), jnp.float32)
    zeros = jnp.zeros(lead + (LANES - MLA_NOPE - MLA_ROPE,), jnp.float32)
    cos_r = jnp.concatenate([ones, cos, cos, zeros], axis=-1).reshape(-1, LANES)
    sin_r = jnp.concatenate([0.0 * ones, sin, sin, zeros], axis=-1).reshape(-1, LANES)
    to_cols = lambda a: jnp.transpose(a.reshape(-1, tm, LANES), (0, 2, 1))
    return cos_r, sin_r, to_cols(cos_r), to_cols(sin_r)


def _t5_bucket(rel):
    n = jnp.maximum(rel, 0)
    max_exact = N_BUCKETS // 2
    nf = jnp.maximum(n, 1).astype(jnp.float32)
    large = max_exact + (jnp.log(nf / max_exact) / math.log(MAX_DISTANCE / max_exact)
                         * (N_BUCKETS - max_exact)).astype(jnp.int32)
    large = jnp.minimum(large, N_BUCKETS - 1)
    return jnp.where(n < max_exact, n, large)


def _bias_tiles(rel_bias, tq):
    n_heads = rel_bias.shape[1]
    dist = jnp.arange(2 * tq, dtype=jnp.int32)
    b = ((rel_bias[_t5_bucket(dist)] - rel_bias[N_BUCKETS - 1]) * LOG2E).T

    def toeplitz(u):
        flat = jnp.tile(u, (1, tq))[:, :tq * (2 * tq - 1)]
        return flat.reshape(n_heads, tq, 2 * tq - 1)[:, :, :tq]

    diag = toeplitz(jnp.concatenate([b[:, :tq], jnp.zeros_like(b[:, :tq])], axis=1))
    prev = toeplitz(jnp.concatenate([b[:, tq:], b[:, :tq]], axis=1))
    return diag, prev


def kernel(x, positions, rel_bias, norm_attn, w_in, q_norm, w_uq, kv_norm, w_ukv, diff_lambda, diff_subln,
           w_out, norm_mlp, w_up, w_down, norm_final):
    b, s, d = x.shape
    depth = w_in.shape[0]
    assert s % TQ == 0
    nk = s // TQ
    cos_r, sin_r, cos_c, sin_c = _rope_tables(positions, TQ)
    bias_diag, bias_prev = _bias_tiles(rel_bias.astype(jnp.float32), TQ)
    row2 = lambda v: v.reshape(1, -1).astype(jnp.float32)
    x2d = x.reshape(b * s, d)
    for l in range(depth):
        w1, w1t = _prep_in_proj(w_in[l])
        wuqt, wuk, wuvt = _prep_mla_up(w_uq[l], w_ukv[l])
        qmt, km, vmt, qdt, kd, vdt, qit, wit, qct, kc, vct = _proj_call(
            x2d, row2(norm_attn[l]), w1, w1t, row2(q_norm[l]), wuqt, row2(kv_norm[l]), wuk, wuvt,
            cos_r, sin_r, cos_c, sin_c)
        r3 = lambda a: a.reshape(b, s, a.shape[-1])
        r4 = lambda a: a.reshape(b, nk, a.shape[-2], a.shape[-1])
        o_b = _dsa_call(r4(qdt), r4(qit), r4(wit), r3(kd), r4(vdt), bias_diag[:H_DSA], bias_prev[:H_DSA])
        lam_init = 0.8 - 0.6 * math.exp(-0.3 * l)
        lp = diff_lambda[l].astype(jnp.float32)
        lam = (jnp.exp(jnp.sum(lp[0] * lp[1])) - jnp.exp(jnp.sum(lp[2] * lp[3])) + lam_init).reshape(1, 1)
        o_a, o_c = _dense_attn_call(r4(qmt), r3(km), r4(vmt), r4(qct), r3(kc), r4(vct), bias_diag[H_DSA:],
                                    bias_prev[H_DSA:], lam, diff_subln[l].reshape(-1, 1).astype(jnp.float32), lam_init)
        f2 = lambda a: a.reshape(b * s, a.shape[-1])
        x2d = _out_mlp_call(x2d, f2(o_a), f2(o_b), f2(o_c), w_out[l].astype(MXU_DTYPE), row2(norm_mlp[l]),
                            w_up[l].astype(MXU_DTYPE), w_down[l].astype(MXU_DTYPE), row2(norm_final), l == depth - 1)
    return x2d.reshape(b, s, d)
```
